```python
import jax, jax.numpy as jnp
from jax import lax
import numpy as np

D_MODEL = 1024
BATCH = 16
SEQ = 4096
DEPTH = 4

H_A = 4
D_A = D_MODEL
DH_A = D_A // H_A
CONV_K = 4
CHUNK = 64
H_B = 8
D_B = D_MODEL
DH_B = D_B // H_B
DH_NOPE = 128
DH_ROPE = 64
DH_V = 128
H_C = D_MODEL // DH_V
D_C = H_C * DH_V
Q_RANK = 3 * D_MODEL // 8
KV_RANK = D_MODEL // 4
ROPE_BASE = 10000.0
Q_BLOCK = 128
IN_SPLITS = (D_A, D_A, D_A, H_A, H_A, D_A,
             D_B, D_B, D_B, H_B, D_B,
             Q_RANK, KV_RANK, DH_ROPE, D_C,
             D_MODEL, D_MODEL, D_MODEL)
D_IN = sum(IN_SPLITS)
LN_EPS = 1e-5
RMS_EPS = 1e-6

kernel_name = 'hybrid_mlstm_fox_mla_gated_block'


def layer_norm(x, g, b):
    xf = x.astype(jnp.float32)
    mu = jnp.mean(xf, axis=-1, keepdims=True)
    var = jnp.mean(jnp.square(xf - mu), axis=-1, keepdims=True)
    return ((xf - mu) * lax.rsqrt(var + LN_EPS)).astype(x.dtype) * g + b


def rms_norm(x, g):
    xf = x.astype(jnp.float32)
    ms = jnp.mean(jnp.square(xf), axis=-1, keepdims=True)
    return (xf * lax.rsqrt(ms + RMS_EPS)).astype(x.dtype) * g


def split_columns(p):
    idx = []
    acc = 0
    for s in IN_SPLITS[:-1]:
        acc += s
        idx.append(acc)
    return jnp.split(p, idx, axis=-1)


def rope_tables(positions):
    inv_freq = ROPE_BASE ** (-jnp.arange(0, DH_ROPE, 2, dtype=jnp.float32) / DH_ROPE)
    ang = positions.astype(jnp.float32)[..., None] * inv_freq
    return jnp.cos(ang), jnp.sin(ang)


def apply_rope(x, cos, sin):
    x1, x2 = jnp.split(x, 2, axis=-1)
    return jnp.concatenate([x1 * cos - x2 * sin, x2 * cos + x1 * sin], axis=-1)


def causal_depthwise_conv(u, w, b):
    T = u.shape[1]
    up = jnp.pad(u, ((0, 0), (CONV_K - 1, 0), (0, 0)))
    out = b
    for j in range(CONV_K):
        out = out + up[:, j:j + T] * w[j]
    return out


def causal_block_attention(q, k, v, scale, log_decay=None):
    B, H, T, dk = q.shape
    dv = v.shape[-1]
    nb = T // Q_BLOCK
    q_blocks = jnp.moveaxis(q.reshape(B, H, nb, Q_BLOCK, dk), 2, 0)
    starts = jnp.arange(nb) * Q_BLOCK
    key_pos = jnp.arange(T)

    def one(args):
        qi, start = args
        s = jnp.einsum('bhqd,bhkd->bhqk', qi, k).astype(jnp.float32) * scale
        if log_decay is not None:
            ci = lax.dynamic_slice_in_dim(log_decay, start, Q_BLOCK, axis=2)
            s = s + ci[..., None] - log_decay[:, :, None, :]
        q_pos = start + jnp.arange(Q_BLOCK)
        s = jnp.where(key_pos[None, :] <= q_pos[:, None], s, -jnp.inf)
        p = jax.nn.softmax(s, axis=-1)
        return jnp.einsum('bhqk,bhkd->bhqd', p.astype(v.dtype), v)

    out = lax.map(one, (q_blocks, starts))
    return jnp.moveaxis(out, 0, 2).reshape(B, H, T, dv)


def mlstm_branch(c_in, v_in, o_pre, i_pre, f_pre, conv_w, conv_b, w_q, w_k, b_i, b_f, norm_g):
    B, T, _ = v_in.shape
    nc = T // CHUNK
    f32 = jnp.float32
    u = jax.nn.silu(causal_depthwise_conv(c_in, conv_w, conv_b)).reshape(B, T, H_A, DH_A)
    q = jnp.einsum('bthd,hde->bhte', u, w_q).astype(f32)
    k = jnp.einsum('bthd,hde->bhte', u, w_k).astype(f32) * (DH_A ** -0.5)
    v = v_in.reshape(B, T, H_A, DH_A).transpose(0, 2, 1, 3).astype(f32)
    log_i = (i_pre + b_i).astype(f32).transpose(0, 2, 1)
    log_f = jax.nn.log_sigmoid((f_pre + b_f).astype(f32)).transpose(0, 2, 1)

    def chunks(a):
        return jnp.moveaxis(a.reshape(B, H_A, nc, CHUNK, *a.shape[3:]), 2, 0)

    qc, kc, vc, ic = chunks(q), chunks(k), chunks(v), chunks(log_i)
    bc = jnp.cumsum(chunks(log_f), axis=-1)
    causal = jnp.tril(jnp.ones((CHUNK, CHUNK), dtype=bool))

    def step(carry, xs):
        C, n, m = carry
        q_, k_, v_, b_, i_ = xs
        d_mat = jnp.where(causal, b_[..., :, None] - b_[..., None, :] + i_[..., None, :], -jnp.inf)
        inter = b_ + m[..., None]
        m_t = jnp.maximum(jnp.max(d_mat, axis=-1), inter)
        s = jnp.einsum('bhld,bhsd->bhls', q_, k_) * jnp.exp(d_mat - m_t[..., None])
        decay = jnp.exp(inter - m_t)
        num = jnp.einsum('bhls,bhse->bhle', s, v_) + decay[..., None] * jnp.einsum('bhld,bhde->bhle', q_, C)
        den = jnp.sum(s, axis=-1) + decay * jnp.einsum('bhld,bhd->bhl', q_, n)
        h = num / jnp.maximum(jnp.abs(den), jnp.exp(-m_t))[..., None]
        g = b_[..., -1]
        a = g[..., None] - b_ + i_
        m_new = jnp.maximum(g + m, jnp.max(a, axis=-1))
        wa = jnp.exp(a - m_new[..., None])
        sc = jnp.exp(g + m - m_new)
        C_new = sc[..., None, None] * C + jnp.einsum('bhs,bhsd,bhse->bhde', wa, k_, v_)
        n_new = sc[..., None] * n + jnp.einsum('bhs,bhsd->bhd', wa, k_)
        return (C_new, n_new, m_new), h

    init = (jnp.zeros((B, H_A, DH_A, DH_A), f32), jnp.zeros((B, H_A, DH_A), f32), jnp.zeros((B, H_A), f32))
    _, hc = lax.scan(step, init, (qc, kc, vc, bc, ic))
    h = jnp.moveaxis(hc, 0, 2).reshape(B, H_A, T, DH_A).transpose(0, 2, 1, 3)
    mu = jnp.mean(h, axis=-1, keepdims=True)
    var = jnp.mean(jnp.square(h - mu), axis=-1, keepdims=True)
    hn = (h - mu) * lax.rsqrt(var + LN_EPS) * norm_g.reshape(H_A, DH_A).astype(f32)
    o = jax.nn.sigmoid(o_pre.astype(f32)).reshape(B, T, H_A, DH_A)
    return (o * hn).reshape(B, T, D_A).astype(v_in.dtype)


def fox_branch(q_in, k_in, v_in, f_pre, b_f):
    B, T, _ = q_in.shape
    heads = lambda a: a.reshape(B, T, H_B, DH_B).transpose(0, 2, 1, 3)
    log_f = jax.nn.log_sigmoid((f_pre + b_f).astype(jnp.float32)).transpose(0, 2, 1)
    cum = jnp.cumsum(log_f, axis=-1)
    o = causal_block_attention(heads(q_in), heads(k_in), heads(v_in), DH_B ** -0.5, cum)
    return o.transpose(0, 2, 1, 3).reshape(B, T, D_B)


def mla_branch(cq, ckv, kr, cos, sin, g_cq, w_uq, g_ckv, w_ukv):
    B, T, _ = cq.shape
    qf = (rms_norm(cq, g_cq) @ w_uq).reshape(B, T, H_C, DH_NOPE + DH_ROPE)
    q_nope = qf[..., :DH_NOPE]
    q_rope = apply_rope(qf[..., DH_NOPE:], cos[:, :, None, :], sin[:, :, None, :])
    kv = (rms_norm(ckv, g_ckv) @ w_ukv).reshape(B, T, H_C, DH_NOPE + DH_V)
    k_nope = kv[..., :DH_NOPE]
    v = kv[..., DH_NOPE:]
    k_rope = apply_rope(kr, cos, sin)
    k_rope = jnp.broadcast_to(k_rope[:, :, None, :], (B, T, H_C, DH_ROPE))
    q = jnp.concatenate([q_nope, q_rope], axis=-1).transpose(0, 2, 1, 3)
    k = jnp.concatenate([k_nope, k_rope], axis=-1).transpose(0, 2, 1, 3)
    o = causal_block_attention(q, k, v.transpose(0, 2, 1, 3), (DH_NOPE + DH_ROPE) ** -0.5)
    return o.transpose(0, 2, 1, 3).reshape(B, T, D_C)


def hybrid_layer(h, cos, sin, w_in, conv_w, conv_b, w_qa, w_ka, b_ia, b_fa, norm_ga, b_fb,
                 g_cq, w_uq, g_ckv, w_ukv, p_a, p_b, p_c, w_out, ln_g, ln_b, alpha):
    p = h @ w_in
    (c_a, v_a, o_a, i_a, f_a, z_a,
     q_b, k_b, v_b, f_b, z_b,
     c_q, c_kv, k_r, z_c,
     g_a, g_b, g_c) = split_columns(p)
    y_a = mlstm_branch(c_a, v_a, o_a, i_a, f_a, conv_w, conv_b, w_qa, w_ka, b_ia, b_fa, norm_ga) * jax.nn.silu(z_a)
    y_b = fox_branch(q_b, k_b, v_b, f_b, b_fb) * jax.nn.silu(z_b)
    y_c = mla_branch(c_q, c_kv, k_r, cos, sin, g_cq, w_uq, g_ckv, w_ukv) * jax.nn.silu(z_c)
    merged = (jax.nn.sigmoid(g_a) * (y_a @ p_a)
              + jax.nn.sigmoid(g_b) * (y_b @ p_b)
              + jax.nn.sigmoid(g_c) * (y_c @ p_c))
    out = merged @ w_out
    return layer_norm(alpha * h + out, ln_g, ln_b)


def setup_inputs(seed: int = 0) -> dict:
    key = jax.random.key(seed)
    ks = jax.random.split(key, 24)
    nrm = lambda k, shape, s: jax.random.normal(k, shape, jnp.float32) * s
    beta = (8.0 * DEPTH) ** -0.25
    x = jax.random.normal(ks[0], (BATCH, SEQ, D_MODEL), jnp.float32)
    offsets = jax.random.randint(ks[1], (BATCH, 1), 0, 1024, dtype=jnp.int32)
    positions = offsets + jnp.arange(SEQ, dtype=jnp.int32)[None, :]
    return {
        'x': x,
        'positions': positions,
        'ln_in_g': 1.0 + nrm(ks[2], (D_MODEL,), 0.02),
        'ln_in_b': nrm(ks[3], (D_MODEL,), 0.02),
        'w_in': nrm(ks[4], (DEPTH, D_MODEL, D_IN), D_MODEL ** -0.5),
        'conv_w': nrm(ks[5], (DEPTH, CONV_K, D_A), CONV_K ** -0.5),
        'conv_b': nrm(ks[6], (DEPTH, D_A), 0.02),
        'w_qa': nrm(ks[7], (DEPTH, H_A, DH_A, DH_A), DH_A ** -0.5),
        'w_ka': nrm(ks[8], (DEPTH, H_A, DH_A, DH_A), DH_A ** -0.5),
        'b_ia': nrm(ks[9], (DEPTH, H_A), 0.1),
        'b_fa': jnp.linspace(3.0, 6.0, H_A, dtype=jnp.float32)[None, :] + nrm(ks[10], (DEPTH, H_A), 0.1),
        'norm_ga': 1.0 + nrm(ks[11], (DEPTH, D_A), 0.02),
        'b_fb': jnp.linspace(1.0, 5.0, H_B, dtype=jnp.float32)[None, :] + nrm(ks[12], (DEPTH, H_B), 0.1),
        'g_cq': 1.0 + nrm(ks[13], (DEPTH, Q_RANK), 0.02),
        'w_uq': nrm(ks[14], (DEPTH, Q_RANK, H_C * (DH_NOPE + DH_ROPE)), Q_RANK ** -0.5),
        'g_ckv': 1.0 + nrm(ks[15], (DEPTH, KV_RANK), 0.02),
        'w_ukv': nrm(ks[16], (DEPTH, KV_RANK, H_C * (DH_NOPE + DH_V)), KV_RANK ** -0.5),
        'p_a': nrm(ks[17], (DEPTH, D_A, D_MODEL), beta * D_A ** -0.5),
        'p_b': nrm(ks[18], (DEPTH, D_B, D_MODEL), beta * D_B ** -0.5),
        'p_c': nrm(ks[19], (DEPTH, D_C, D_MODEL), beta * D_C ** -0.5),
        'w_out': nrm(ks[20], (DEPTH, D_MODEL, D_MODEL), beta * D_MODEL ** -0.5),
        'ln_g': 1.0 + nrm(ks[21], (DEPTH, D_MODEL), 0.02),
        'ln_b': nrm(ks[22], (DEPTH, D_MODEL), 0.02),
    }


def reference(x, positions, ln_in_g, ln_in_b, w_in, conv_w, conv_b, w_qa, w_ka, b_ia, b_fa, norm_ga,
              b_fb, g_cq, w_uq, g_ckv, w_ukv, p_a, p_b, p_c, w_out, ln_g, ln_b):
    alpha = (2.0 * DEPTH) ** 0.25
    cos, sin = rope_tables(positions)
    cos = cos.astype(x.dtype)
    sin = sin.astype(x.dtype)
    h = layer_norm(x, ln_in_g, ln_in_b)
    for l in range(DEPTH):
        h = hybrid_layer(h, cos, sin, w_in[l], conv_w[l], conv_b[l], w_qa[l], w_ka[l], b_ia[l], b_fa[l],
                         norm_ga[l], b_fb[l], g_cq[l], w_uq[l], g_ckv[l], w_ukv[l],
                         p_a[l], p_b[l], p_c[l], w_out[l], ln_g[l], ln_b[l], alpha)
    return h
```

```python
import functools
import math

import jax
import jax.numpy as jnp
from jax import lax
from jax.experimental import pallas as pl
from jax.experimental.pallas import tpu as pltpu

F32 = jnp.float32
BF16 = jnp.bfloat16

D_MODEL = 1024
H_A, DH_A, CONV_K = 4, 256, 4
H_B, DH_B = 8, 128
H_C, DH_NOPE, DH_ROPE, DH_V = 8, 128, 64, 128
Q_RANK, KV_RANK = 384, 256
ROPE_BASE = 10000.0
LN_EPS = 1e-5
RMS_EPS = 1e-6
IN_SPLITS = (1024, 1024, 1024, 4, 4, 1024,
             1024, 1024, 1024, 8, 1024,
             Q_RANK, KV_RANK, DH_ROPE, 1024,
             1024, 1024, 1024)

COL_CA, COL_VA, COL_OA, COL_ZA, COL_QB, COL_KB, COL_VB, COL_ZB, COL_ZC, COL_GA, COL_GB, COL_GC = range(12)
LOWRANK_OFF = 12 * 1024
LOWRANK_W = 768
N_BIG = LOWRANK_OFF + LOWRANK_W
DK_C = 256

MLSTM_CHUNK = 256
VMEM_LIMIT = 56 * 1024 * 1024

NT_DIMS = (((1,), (1,)), ((), ()))
TN_DIMS = (((0,), (0,)), ((), ()))


def _cparams(sem):
    return pltpu.CompilerParams(dimension_semantics=sem, vmem_limit_bytes=VMEM_LIMIT)


def _sigmoid(x):
    return 1.0 / (1.0 + jnp.exp(-x))


def _log_sigmoid(x):
    return jnp.minimum(x, 0.0) - jnp.log1p(jnp.exp(-jnp.abs(x)))


def _split3(x):
    hi = x.astype(BF16)
    r = x - hi.astype(F32)
    mid = r.astype(BF16)
    lo = (r - mid.astype(F32)).astype(BF16)
    return hi, mid, lo


def _ln_kernel(x_ref, g_ref, b_ref, hf_ref, hb_ref):
    x = x_ref[...]
    mu = jnp.mean(x, axis=-1, keepdims=True)
    xc = x - mu
    var = jnp.mean(xc * xc, axis=-1, keepdims=True)
    y = xc * lax.rsqrt(var + LN_EPS) * g_ref[...] + b_ref[...]
    hf_ref[...] = y
    hb_ref[...] = y.astype(BF16)


def _input_ln(x2, g, b):
    n = x2.shape[0]
    tm = min(512, n)
    row = pl.BlockSpec((tm, D_MODEL), lambda i: (i, 0))
    vec = pl.BlockSpec((1, D_MODEL), lambda i: (0, 0))
    return pl.pallas_call(
        _ln_kernel, grid=(n // tm,), in_specs=[row, vec, vec], out_specs=[row, row],
        out_shape=[jax.ShapeDtypeStruct((n, D_MODEL), F32), jax.ShapeDtypeStruct((n, D_MODEL), BF16)],
        compiler_params=_cparams(("parallel",)), name="input_ln",
    )(x2, g.reshape(1, -1), b.reshape(1, -1))


def _rope_kernel(pos_ref, freq_ref, tab_ref):
    ang = pos_ref[...].astype(F32) * freq_ref[...]
    lane = lax.broadcasted_iota(jnp.int32, ang.shape, 1)
    tab_ref[...] = jnp.where(lane < DH_ROPE, jnp.cos(ang), jnp.sin(ang))


def _rope_table(positions):
    n = positions.size
    tm = min(1024, n)
    inv_freq = ROPE_BASE ** (-jnp.arange(0, DH_ROPE, 2, dtype=F32) / DH_ROPE)
    freq = jnp.tile(inv_freq, 4).reshape(1, 128)
    return pl.pallas_call(
        _rope_kernel, grid=(n // tm,),
        in_specs=[pl.BlockSpec((tm, 1), lambda i: (i, 0)), pl.BlockSpec((1, 128), lambda i: (0, 0))],
        out_specs=pl.BlockSpec((tm, 128), lambda i: (i, 0)),
        out_shape=jax.ShapeDtypeStruct((n, 128), F32),
        compiler_params=_cparams(("parallel",)), name="rope_table",
    )(positions.reshape(n, 1), freq)


def _matmul_kernel(x_ref, w_ref, o_ref):
    o_ref[...] = jnp.dot(x_ref[...], w_ref[...], preferred_element_type=F32).astype(o_ref.dtype)


def _in_proj(hb, w_big):
    n = hb.shape[0]
    tm = min(1024, n)
    tn = 768
    return pl.pallas_call(
        _matmul_kernel, grid=(n // tm, N_BIG // tn),
        in_specs=[pl.BlockSpec((tm, D_MODEL), lambda i, j: (i, 0)),
                  pl.BlockSpec((D_MODEL, tn), lambda i, j: (0, j))],
        out_specs=pl.BlockSpec((tm, tn), lambda i, j: (i, j)),
        out_shape=jax.ShapeDtypeStruct((n, N_BIG), BF16),
        compiler_params=_cparams(("parallel", "arbitrary")), name="in_proj",
    )(hb, w_big)


def _gates_kernel(h_ref, wf_ref, wi_ref, wft_ref, wit_ref, bfc_ref, bic_ref, bfr_ref, bir_ref,
                  rows_ref, cols_ref, carry_ref, *, tg, chunk):
    t = pl.program_id(1)

    @pl.when(t == 0)
    def _():
        carry_ref[...] = jnp.zeros_like(carry_ref)

    hb = h_ref[0]
    r_i = lax.broadcasted_iota(jnp.int32, (tg, tg), 0)
    c_i = lax.broadcasted_iota(jnp.int32, (tg, tg), 1)
    same_chunk = (r_i // chunk) == (c_i // chunk)
    ind = lambda mask: jnp.where(mask, 1.0, 0.0).astype(BF16)
    upper = ind(r_i <= c_i)
    upper_bd = ind((r_i <= c_i) & same_chunk)
    lower_bd = ind((r_i >= c_i) & same_chunk)

    gf = lax.dot_general(wft_ref[...], hb, NT_DIMS, preferred_element_type=F32) + bfc_ref[...]
    gi = lax.dot_general(wit_ref[...], hb, NT_DIMS, preferred_element_type=F32) + bic_ref[...]
    lf = _log_sigmoid(gf)
    y_full = jnp.zeros((16, tg), F32)
    y_bd = jnp.zeros((16, tg), F32)
    for part in _split3(lf):
        y_full = y_full + jnp.dot(part, upper, preferred_element_type=F32)
        y_bd = y_bd + jnp.dot(part, upper_bd, preferred_element_type=F32)
    cum = y_full + carry_ref[:, 0:1]
    row = lax.broadcasted_iota(jnp.int32, (16, tg), 0)
    rows_ref[0] = jnp.where(row < 4, y_bd, jnp.where(row < 8, gi - y_bd, cum))
    carry_ref[...] = jnp.broadcast_to(cum[:, tg - 1:tg], carry_ref.shape)

    gfc = jnp.dot(hb, wf_ref[...], preferred_element_type=F32) + bfr_ref[...]
    gic = jnp.dot(hb, wi_ref[...], preferred_element_type=F32) + bir_ref[...]
    lfc = _log_sigmoid(gfc)
    yc = jnp.zeros((tg, 128), F32)
    for part in _split3(lfc):
        yc = yc + jnp.dot(lower_bd, part, preferred_element_type=F32)
    col = lax.broadcasted_iota(jnp.int32, (tg, 128), 1)
    cols_ref[0] = jnp.where(col < 4, yc, gic - yc)


def _gates(hb3, gw):
    bsz, t, _ = hb3.shape
    tg = min(512, t)
    full = lambda shape: pl.BlockSpec(shape, lambda b, i: tuple(0 for _ in shape))
    return pl.pallas_call(
        functools.partial(_gates_kernel, tg=tg, chunk=min(MLSTM_CHUNK, t)),
        grid=(bsz, t // tg),
        in_specs=[pl.BlockSpec((1, tg, D_MODEL), lambda b, i: (b, i, 0)),
                  full((D_MODEL, 128)), full((D_MODEL, 128)), full((16, D_MODEL)), full((16, D_MODEL)),
                  full((16, 1)), full((16, 1)), full((1, 128)), full((1, 128))],
        out_specs=[pl.BlockSpec((1, 16, tg), lambda b, i: (b, 0, i)),
                   pl.BlockSpec((1, tg, 128), lambda b, i: (b, i, 0))],
        out_shape=[jax.ShapeDtypeStruct((bsz, 16, t), F32), jax.ShapeDtypeStruct((bsz, t, 128), F32)],
        scratch_shapes=[pltpu.VMEM((16, 128), F32)],
        compiler_params=_cparams(("parallel", "arbitrary")), name="gates",
    )(hb3, *gw)


def _mlstm_kernel(c_ref, v_ref, o_ref, rows_ref, cols_ref, cw_ref, cb_ref, wq_ref, wk_ref, ng_ref,
                  y_ref, cbuf, c_state, n_state, m_state, *, chunk):
    L = chunk
    t = pl.program_id(1)

    @pl.when(t == 0)
    def _():
        cbuf[0:8, :] = jnp.zeros((8, D_MODEL), F32)
        c_state[...] = jnp.zeros_like(c_state)
        n_state[...] = jnp.zeros_like(n_state)
        m_state[...] = jnp.zeros_like(m_state)

    x = c_ref[0].astype(F32)
    cbuf[8:8 + L, :] = x
    conv = cb_ref[...]
    for j in range(CONV_K):
        conv = conv + cbuf[8 - (CONV_K - 1) + j:8 - (CONV_K - 1) + j + L, :] * cw_ref[j:j + 1, :]
    cbuf[0:8, :] = x[L - 8:L, :]
    ub = (conv * _sigmoid(conv)).astype(BF16)

    r_i = lax.broadcasted_iota(jnp.int32, (L, L), 0)
    c_i = lax.broadcasted_iota(jnp.int32, (L, L), 1)
    causal = c_i <= r_i
    for h in range(H_A):
        hs = slice(h * DH_A, (h + 1) * DH_A)
        uh = ub[:, hs]
        q = jnp.dot(uh, wq_ref[h], preferred_element_type=F32)
        k = jnp.dot(uh, wk_ref[h], preferred_element_type=F32) * (DH_A ** -0.5)
        qb = q.astype(BF16)
        kb = k.astype(BF16)
        vb = v_ref[0, :, hs]
        b_col = cols_ref[0, :, h:h + 1]
        c_col = cols_ref[0, :, H_A + h:H_A + h + 1]
        b_row = rows_ref[0, h:h + 1, :]
        c_row = rows_ref[0, H_A + h:H_A + h + 1, :]
        m_prev = m_state[h, 0:1, 0:1]

        dmat = jnp.where(causal, b_col + c_row, -jnp.inf)
        inter = b_col + m_prev
        m_t = jnp.maximum(jnp.max(dmat, axis=1, keepdims=True), inter)
        s = lax.dot_general(qb, kb, NT_DIMS, preferred_element_type=F32) * jnp.exp(dmat - m_t)
        decay = jnp.exp(inter - m_t)
        c_old = c_state[h]
        n_old = n_state[h]
        num = (jnp.dot(s.astype(BF16), vb, preferred_element_type=F32)
               + decay * jnp.dot(qb, c_old.astype(BF16), preferred_element_type=F32))
        den = jnp.sum(s, axis=1, keepdims=True) + decay * jnp.sum(q * n_old, axis=1, keepdims=True)
        hh = num / jnp.maximum(jnp.abs(den), jnp.exp(-m_t))

        g = b_row[:, L - 1:L]
        a_col = g + c_col
        m_new = jnp.maximum(g + m_prev, jnp.max(a_col, axis=0, keepdims=True))
        kw = k * jnp.exp(a_col - m_new)
        sc = jnp.exp(g + m_prev - m_new)
        c_state[h] = sc * c_old + lax.dot_general(kw.astype(BF16), vb, TN_DIMS, preferred_element_type=F32)
        n_state[h] = sc * n_old + jnp.sum(kw, axis=0, keepdims=True)
        m_state[h] = jnp.broadcast_to(m_new, m_state.shape[1:])

        mu = jnp.mean(hh, axis=1, keepdims=True)
        hc = hh - mu
        var = jnp.mean(hc * hc, axis=1, keepdims=True)
        hn = hc * lax.rsqrt(var + LN_EPS) * ng_ref[:, hs]
        og = _sigmoid(o_ref[0, :, hs].astype(F32))
        y_ref[0, :, hs] = (og * hn).astype(BF16)


def _mlstm(p3, rows, cols, conv_w, conv_b, wq, wk, norm_g):
    bsz, t, _ = p3.shape
    L = min(MLSTM_CHUNK, t)
    colblk = lambda c: pl.BlockSpec((1, L, 1024), lambda b, i, c=c: (b, i, c))
    full = lambda shape: pl.BlockSpec(shape, lambda b, i: tuple(0 for _ in shape))
    return pl.pallas_call(
        functools.partial(_mlstm_kernel, chunk=L),
        grid=(bsz, t // L),
        in_specs=[colblk(COL_CA), colblk(COL_VA), colblk(COL_OA),
                  pl.BlockSpec((1, 16, L), lambda b, i: (b, 0, i)),
                  pl.BlockSpec((1, L, 128), lambda b, i: (b, i, 0)),
                  full((CONV_K, 1024)), full((1, 1024)), full((H_A, DH_A, DH_A)), full((H_A, DH_A, DH_A)),
                  full((1, 1024))],
        out_specs=pl.BlockSpec((1, L, 1024), lambda b, i: (b, i, 0)),
        out_shape=jax.ShapeDtypeStruct((bsz, t, 1024), BF16),
        scratch_shapes=[pltpu.VMEM((8 + L, 1024), F32), pltpu.VMEM((H_A, DH_A, DH_A), F32),
                        pltpu.VMEM((H_A, 1, DH_A), F32), pltpu.VMEM((H_A, 8, 128), F32)],
        compiler_params=_cparams(("parallel", "arbitrary")), name="mlstm",
    )(p3, p3, p3, rows, cols, conv_w, conv_b.reshape(1, -1), wq, wk, norm_g.reshape(1, -1))


def _flash_kernel(*refs, scale, tq, has_bias):
    if has_bias:
        q_ref, k_ref, v_ref, bias_ref, o_ref, m_sc, l_sc, acc_sc = refs
    else:
        q_ref, k_ref, v_ref, o_ref, m_sc, l_sc, acc_sc = refs
        bias_ref = None
    i = pl.program_id(2)
    q = (q_ref[0].astype(F32) * scale).astype(BF16)
    m_sc[...] = jnp.full(m_sc.shape, -jnp.inf, F32)
    l_sc[...] = jnp.zeros_like(l_sc)
    acc_sc[...] = jnp.zeros_like(acc_sc)

    def step(j, masked):
        start = pl.multiple_of(j * tq, tq)
        kj = k_ref[0, pl.ds(start, tq), :]
        vj = v_ref[0, pl.ds(start, tq), :]
        s = lax.dot_general(q, kj, NT_DIMS, preferred_element_type=F32)
        if has_bias:
            s = s - bias_ref[0, j]
        if masked:
            r_i = lax.broadcasted_iota(jnp.int32, (tq, tq), 0)
            c_i = lax.broadcasted_iota(jnp.int32, (tq, tq), 1)
            s = jnp.where(c_i <= r_i, s, -jnp.inf)
        m_prev = m_sc[...]
        m_new = jnp.maximum(m_prev, jnp.max(s, axis=1, keepdims=True))
        alpha = jnp.exp(m_prev - m_new)
        p = jnp.exp(s - m_new)
        l_sc[...] = alpha * l_sc[...] + jnp.sum(p, axis=1, keepdims=True)
        acc_sc[...] = alpha * acc_sc[...] + jnp.dot(p.astype(BF16), vj, preferred_element_type=F32)
        m_sc[...] = m_new

    def body(j, carry):
        step(j, False)
        return carry

    lax.fori_loop(0, i, body, 0)
    step(i, True)
    o_ref[0] = (acc_sc[...] / l_sc[...]).astype(o_ref.dtype)


def _flash(q_arr, k_arr, v_arr, bias, *, n_heads, dk, q_blk, k_blk, v_blk, scale):
    bsz, t, _ = q_arr.shape
    tq = min(512, t)
    in_specs = [pl.BlockSpec((1, tq, dk), lambda b, h, i: (b, i, q_blk + h)),
                pl.BlockSpec((1, t, dk), lambda b, h, i: (b, 0, k_blk + h)),
                pl.BlockSpec((1, t, 128), lambda b, h, i: (b, 0, v_blk + h))]
    args = [q_arr, k_arr, v_arr]
    if bias is not None:
        in_specs.append(pl.BlockSpec((1, t // tq, 1, tq), lambda b, h, i: (b * n_heads + h, 0, 0, 0)))
        args.append(bias)
    return pl.pallas_call(
        functools.partial(_flash_kernel, scale=scale, tq=tq, has_bias=bias is not None),
        grid=(bsz, n_heads, t // tq),
        in_specs=in_specs,
        out_specs=pl.BlockSpec((1, tq, 128), lambda b, h, i: (b, i, h)),
        out_shape=jax.ShapeDtypeStruct((bsz, t, n_heads * 128), BF16),
        scratch_shapes=[pltpu.VMEM((tq, 1), F32), pltpu.VMEM((tq, 1), F32), pltpu.VMEM((tq, 128), F32)],
        compiler_params=_cparams(("parallel", "parallel", "arbitrary")),
        name="flash_fox" if bias is not None else "flash_mla",
    )(*args)


def _mla_prep_kernel(c_ref, tab_ref, gq_ref, gkv_ref, wq_ref, wkv_ref, q_out, k_out, v_out):
    c = c_ref[...].astype(F32)
    ckv = c[:, 0:KV_RANK]
    cq = c[:, KV_RANK:KV_RANK + Q_RANK]
    kr = c[:, KV_RANK + Q_RANK:]

    def rms(x, g):
        ms = jnp.mean(x * x, axis=-1, keepdims=True)
        return (x * lax.rsqrt(ms + RMS_EPS) * g).astype(BF16)

    qf = jnp.dot(rms(cq, gq_ref[...]), wq_ref[...], preferred_element_type=F32)
    kvf = jnp.dot(rms(ckv, gkv_ref[...]), wkv_ref[...], preferred_element_type=F32)
    tab = tab_ref[...]

    def rope(xx):
        pr = xx * tab
        return pr + pltpu.roll(pr, DH_ROPE, 1)

    lane = lax.broadcasted_iota(jnp.int32, kr.shape, 1)
    k_rope = jnp.where(lane < DH_ROPE, rope(kr), 0.0).astype(BF16)
    for h in range(H_C):
        lo = h * DK_C
        q_out[:, lo:lo + 128] = qf[:, lo:lo + 128].astype(BF16)
        q_out[:, lo + 128:lo + 256] = rope(qf[:, lo + 128:lo + 256]).astype(BF16)
        k_out[:, lo:lo + 128] = kvf[:, lo:lo + 128].astype(BF16)
        k_out[:, lo + 128:lo + 256] = k_rope
        v_out[:, h * DH_V:(h + 1) * DH_V] = kvf[:, lo + 128:lo + 256].astype(BF16)


def _mla_prep(p2, tab, g_cq, g_ckv, wq, wkv):
    n = p2.shape[0]
    tm = min(512, n)
    full = lambda shape: pl.BlockSpec(shape, lambda i: tuple(0 for _ in shape))
    return pl.pallas_call(
        _mla_prep_kernel, grid=(n // tm,),
        in_specs=[pl.BlockSpec((tm, LOWRANK_W), lambda i: (i, LOWRANK_OFF // LOWRANK_W)),
                  pl.BlockSpec((tm, 128), lambda i: (i, 0)),
                  full((1, Q_RANK)), full((1, KV_RANK)), full((Q_RANK, H_C * DK_C)), full((KV_RANK, H_C * DK_C))],
        out_specs=[pl.BlockSpec((tm, H_C * DK_C), lambda i: (i, 0)),
                   pl.BlockSpec((tm, H_C * DK_C), lambda i: (i, 0)),
                   pl.BlockSpec((tm, H_C * DH_V), lambda i: (i, 0))],
        out_shape=[jax.ShapeDtypeStruct((n, H_C * DK_C), BF16), jax.ShapeDtypeStruct((n, H_C * DK_C), BF16),
                   jax.ShapeDtypeStruct((n, H_C * DH_V), BF16)],
        compiler_params=_cparams(("parallel",)), name="mla_prep",
    )(p2, tab, g_cq.reshape(1, -1), g_ckv.reshape(1, -1), wq, wkv)


def _merge_kernel(ya_ref, yb_ref, yc_ref, za_ref, zb_ref, zc_ref, ga_ref, gb_ref, gc_ref, h_ref,
                  pa_ref, pb_ref, pc_ref, wo_ref, lg_ref, lb_ref, hf_ref, hb_ref, *, alpha):
    def branch(y_ref, z_ref, g_ref, p_ref):
        z = z_ref[...].astype(F32)
        y = (y_ref[...].astype(F32) * (z * _sigmoid(z))).astype(BF16)
        return _sigmoid(g_ref[...].astype(F32)) * jnp.dot(y, p_ref[...], preferred_element_type=F32)

    merged = (branch(ya_ref, za_ref, ga_ref, pa_ref) + branch(yb_ref, zb_ref, gb_ref, pb_ref)
              + branch(yc_ref, zc_ref, gc_ref, pc_ref))
    out = jnp.dot(merged.astype(BF16), wo_ref[...], preferred_element_type=F32)
    r = alpha * h_ref[...] + out
    mu = jnp.mean(r, axis=-1, keepdims=True)
    rc = r - mu
    var = jnp.mean(rc * rc, axis=-1, keepdims=True)
    y = rc * lax.rsqrt(var + LN_EPS) * lg_ref[...] + lb_ref[...]
    hf_ref[...] = y
    hb_ref[...] = y.astype(BF16)


def _merge(ya, yb, yc, p2, hf, pa, pb, pc, wo, ln_g, ln_b, alpha):
    n = hf.shape[0]
    tm = min(256, n)
    row = pl.BlockSpec((tm, D_MODEL), lambda i: (i, 0))
    colblk = lambda c: pl.BlockSpec((tm, 1024), lambda i, c=c: (i, c))
    wfull = pl.BlockSpec((1024, D_MODEL), lambda i: (0, 0))
    vec = pl.BlockSpec((1, D_MODEL), lambda i: (0, 0))
    return pl.pallas_call(
        functools.partial(_merge_kernel, alpha=alpha), grid=(n // tm,),
        in_specs=[row, row, row, colblk(COL_ZA), colblk(COL_ZB), colblk(COL_ZC),
                  colblk(COL_GA), colblk(COL_GB), colblk(COL_GC), row,
                  wfull, wfull, wfull, wfull, vec, vec],
        out_specs=[row, row],
        out_shape=[jax.ShapeDtypeStruct((n, D_MODEL), F32), jax.ShapeDtypeStruct((n, D_MODEL), BF16)],
        compiler_params=_cparams(("parallel",)), name="merge",
    )(ya, yb, yc, p2, p2, p2, p2, p2, p2, hf, pa, pb, pc, wo, ln_g.reshape(1, -1), ln_b.reshape(1, -1))


def _rot_cols(w):
    half = w.shape[-1] // 2
    return jnp.concatenate([-w[..., half:], w[..., :half]], axis=-1)


def _layer_weights(w_in, b_ia, b_fa, b_fb, w_uq, w_ukv):
    idx = []
    acc = 0
    for s in IN_SPLITS[:-1]:
        acc += s
        idx.append(acc)
    (w_ca, w_va, w_oa, w_ia, w_fa, w_za, w_qb, w_kb, w_vb, w_fb, w_zb,
     w_cq, w_ckv, w_kr, w_zc, w_ga, w_gb, w_gc) = jnp.split(w_in, idx, axis=-1)
    w_big = jnp.concatenate([w_ca, w_va, w_oa, w_za, w_qb, w_kb, w_vb, w_zb, w_zc, w_ga, w_gb, w_gc,
                             w_ckv, w_cq, w_kr, _rot_cols(w_kr)], axis=-1).astype(BF16)
    zpad = lambda k: jnp.zeros((D_MODEL, k), F32)
    w_f = jnp.concatenate([w_fa, w_fa, w_fb, zpad(112)], axis=-1)
    w_i = jnp.concatenate([zpad(4), w_ia, zpad(120)], axis=-1)
    b_f = jnp.concatenate([b_fa, b_fa, b_fb, jnp.zeros((112,), F32)])
    b_i = jnp.concatenate([jnp.zeros((4,), F32), b_ia, jnp.zeros((120,), F32)])
    gw = (w_f.astype(BF16), w_i.astype(BF16), w_f[:, :16].T.astype(BF16), w_i[:, :16].T.astype(BF16),
          b_f[:16].reshape(16, 1), b_i[:16].reshape(16, 1), b_f.reshape(1, 128), b_i.reshape(1, 128))
    wq3 = w_uq.reshape(Q_RANK, H_C, DH_NOPE + DH_ROPE)
    wq_rope = wq3[..., DH_NOPE:]
    wq = jnp.concatenate([wq3, _rot_cols(wq_rope)], axis=-1).reshape(Q_RANK, H_C * DK_C).astype(BF16)
    return w_big, gw, wq, w_ukv.astype(BF16)


def kernel(x, positions, ln_in_g, ln_in_b, w_in, conv_w, conv_b, w_qa, w_ka, b_ia, b_fa, norm_ga, b_fb,
           g_cq, w_uq, g_ckv, w_ukv, p_a, p_b, p_c, w_out, ln_g, ln_b):
    bsz, t, _ = x.shape
    n = bsz * t
    depth = w_in.shape[0]
    alpha = (2.0 * depth) ** 0.25
    tq = min(512, t)

    tab = _rope_table(positions)
    hf, hb = _input_ln(x.reshape(n, D_MODEL), ln_in_g, ln_in_b)
    for l in range(depth):
        w_big, gw, wq, wkv = _layer_weights(w_in[l], b_ia[l], b_fa[l], b_fb[l], w_uq[l], w_ukv[l])
        p2 = _in_proj(hb, w_big)
        p3 = p2.reshape(bsz, t, N_BIG)
        rows, cols = _gates(hb.reshape(bsz, t, D_MODEL), gw)
        ya = _mlstm(p3, rows, cols, conv_w[l], conv_b[l], w_qa[l].astype(BF16), w_ka[l].astype(BF16), norm_ga[l])
        fox_bias = rows[:, 8:16, :].reshape(bsz * H_B, t // tq, 1, tq)
        yb = _flash(p3, p3, p3, fox_bias, n_heads=H_B, dk=DH_B, q_blk=COL_QB * 8, k_blk=COL_KB * 8,
                    v_blk=COL_VB * 8, scale=DH_B ** -0.5)
        qc, kc, vc = _mla_prep(p2, tab, g_cq[l], g_ckv[l], wq, wkv)
        yc = _flash(qc.reshape(bsz, t, -1), kc.reshape(bsz, t, -1), vc.reshape(bsz, t, -1), None,
                    n_heads=H_C, dk=DK_C, q_blk=0, k_blk=0, v_blk=0, scale=(DH_NOPE + DH_ROPE) ** -0.5)
        hf, hb = _merge(ya.reshape(n, -1), yb.reshape(n, -1), yc.reshape(n, -1), p2, hf,
                        p_a[l].astype(BF16), p_b[l].astype(BF16), p_c[l].astype(BF16), w_out[l].astype(BF16),
                        ln_g[l], ln_b[l], alpha)
    return hf.reshape(bsz, t, D_MODEL)
```

```python
import functools
import math

import jax
import jax.numpy as jnp
from jax import lax
from jax.experimental import pallas as pl
from jax.experimental.pallas import tpu as pltpu

F32 = jnp.float32
BF16 = jnp.bfloat16

D_MODEL = 1024
H_A, DH_A, CONV_K = 4, 256, 4
H_B, DH_B = 8, 128
H_C, DH_NOPE, DH_ROPE, DH_V = 8, 128, 64, 128
Q_RANK, KV_RANK = 384, 256
ROPE_BASE = 10000.0
LN_EPS = 1e-5
RMS_EPS = 1e-6
IN_SPLITS = (1024, 1024, 1024, 4, 4, 1024,
             1024, 1024, 1024, 8, 1024,
             Q_RANK, KV_RANK, DH_ROPE, 1024,
             1024, 1024, 1024)

COL_CA, COL_VA, COL_OA, COL_ZA, COL_QB, COL_KB, COL_VB, COL_ZB, COL_ZC, COL_GA, COL_GB, COL_GC = range(12)
LOWRANK_OFF = 12 * 1024
LOWRANK_W = 768
N_BIG = LOWRANK_OFF + LOWRANK_W
DK_C = 256

MLSTM_CHUNK = 256
VMEM_LIMIT = 56 * 1024 * 1024

NT_DIMS = (((1,), (1,)), ((), ()))
TN_DIMS = (((0,), (0,)), ((), ()))


def _cparams(sem):
    return pltpu.CompilerParams(dimension_semantics=sem, vmem_limit_bytes=VMEM_LIMIT)


def _sigmoid(x):
    return 1.0 / (1.0 + jnp.exp(-x))


def _log_sigmoid(x):
    return jnp.minimum(x, 0.0) - jnp.log1p(jnp.exp(-jnp.abs(x)))


def _split3(x):
    hi = x.astype(BF16)
    r = x - hi.astype(F32)
    mid = r.astype(BF16)
    lo = (r - mid.astype(F32)).astype(BF16)
    return hi, mid, lo


def _ln_kernel(x_ref, g_ref, b_ref, hf_ref, hb_ref):
    x = x_ref[...]
    mu = jnp.mean(x, axis=-1, keepdims=True)
    xc = x - mu
    var = jnp.mean(xc * xc, axis=-1, keepdims=True)
    y = xc * lax.rsqrt(var + LN_EPS) * g_ref[...] + b_ref[...]
    hf_ref[...] = y
    hb_ref[...] = y.astype(BF16)


def _input_ln(x2, g, b):
    n = x2.shape[0]
    tm = min(512, n)
    row = pl.BlockSpec((tm, D_MODEL), lambda i: (i, 0))
    vec = pl.BlockSpec((1, D_MODEL), lambda i: (0, 0))
    return pl.pallas_call(
        _ln_kernel, grid=(n // tm,), in_specs=[row, vec, vec], out_specs=[row, row],
        out_shape=[jax.ShapeDtypeStruct((n, D_MODEL), F32), jax.ShapeDtypeStruct((n, D_MODEL), BF16)],
        compiler_params=_cparams(("parallel",)), name="input_ln",
    )(x2, g.reshape(1, -1), b.reshape(1, -1))


def _rope_kernel(pos_ref, freq_ref, tab_ref):
    ang = pos_ref[...].astype(F32) * freq_ref[...]
    lane = lax.broadcasted_iota(jnp.int32, ang.shape, 1)
    tab_ref[...] = jnp.where(lane < DH_ROPE, jnp.cos(ang), jnp.sin(ang))


def _rope_table(positions):
    n = positions.size
    tm = min(1024, n)
    inv_freq = ROPE_BASE ** (-jnp.arange(0, DH_ROPE, 2, dtype=F32) / DH_ROPE)
    freq = jnp.tile(inv_freq, 4).reshape(1, 128)
    return pl.pallas_call(
        _rope_kernel, grid=(n // tm,),
        in_specs=[pl.BlockSpec((tm, 1), lambda i: (i, 0)), pl.BlockSpec((1, 128), lambda i: (0, 0))],
        out_specs=pl.BlockSpec((tm, 128), lambda i: (i, 0)),
        out_shape=jax.ShapeDtypeStruct((n, 128), F32),
        compiler_params=_cparams(("parallel",)), name="rope_table",
    )(positions.reshape(n, 1), freq)


def _matmul_kernel(x_ref, w_ref, o_ref):
    o_ref[...] = jnp.dot(x_ref[...], w_ref[...], preferred_element_type=F32).astype(o_ref.dtype)


def _in_proj(hb, w_big):
    n = hb.shape[0]
    tm = min(1024, n)
    tn = 768
    return pl.pallas_call(
        _matmul_kernel, grid=(n // tm, N_BIG // tn),
        in_specs=[pl.BlockSpec((tm, D_MODEL), lambda i, j: (i, 0)),
                  pl.BlockSpec((D_MODEL, tn), lambda i, j: (0, j))],
        out_specs=pl.BlockSpec((tm, tn), lambda i, j: (i, j)),
        out_shape=jax.ShapeDtypeStruct((n, N_BIG), BF16),
        compiler_params=_cparams(("parallel", "arbitrary")), name="in_proj",
    )(hb, w_big)


def _gates_kernel(h_ref, wf_ref, wi_ref, wft_ref, wit_ref, bfc_ref, bic_ref, bfr_ref, bir_ref,
                  rows_ref, cols_ref, carry_ref, *, tg, chunk):
    t = pl.program_id(1)

    @pl.when(t == 0)
    def _():
        carry_ref[...] = jnp.zeros_like(carry_ref)

    hb = h_ref[0]
    r_i = lax.broadcasted_iota(jnp.int32, (tg, tg), 0)
    c_i = lax.broadcasted_iota(jnp.int32, (tg, tg), 1)
    same_chunk = (r_i // chunk) == (c_i // chunk)
    ind = lambda mask: jnp.where(mask, 1.0, 0.0).astype(BF16)
    upper = ind(r_i <= c_i)
    upper_bd = ind((r_i <= c_i) & same_chunk)
    lower_bd = ind((r_i >= c_i) & same_chunk)

    gf = lax.dot_general(wft_ref[...], hb, NT_DIMS, preferred_element_type=F32) + bfc_ref[...]
    gi = lax.dot_general(wit_ref[...], hb, NT_DIMS, preferred_element_type=F32) + bic_ref[...]
    lf = _log_sigmoid(gf)
    y_full = jnp.zeros((16, tg), F32)
    y_bd = jnp.zeros((16, tg), F32)
    for part in _split3(lf):
        y_full = y_full + jnp.dot(part, upper, preferred_element_type=F32)
        y_bd = y_bd + jnp.dot(part, upper_bd, preferred_element_type=F32)
    cum = y_full + carry_ref[:, 0:1]
    row = lax.broadcasted_iota(jnp.int32, (16, tg), 0)
    rows_ref[0] = jnp.where(row < 4, y_bd, jnp.where(row < 8, gi - y_bd, cum))
    carry_ref[...] = jnp.broadcast_to(cum[:, tg - 1:tg], carry_ref.shape)

    gfc = jnp.dot(hb, wf_ref[...], preferred_element_type=F32) + bfr_ref[...]
    gic = jnp.dot(hb, wi_ref[...], preferred_element_type=F32) + bir_ref[...]
    lfc = _log_sigmoid(gfc)
    yc = jnp.zeros((tg, 128), F32)
    for part in _split3(lfc):
        yc = yc + jnp.dot(lower_bd, part, preferred_element_type=F32)
    col = lax.broadcasted_iota(jnp.int32, (tg, 128), 1)
    cols_ref[0] = jnp.where(col < 4, yc, gic - yc)


def _gates(hb3, gw):
    bsz, t, _ = hb3.shape
    tg = min(512, t)
    full = lambda shape: pl.BlockSpec(shape, lambda b, i: tuple(0 for _ in shape))
    return pl.pallas_call(
        functools.partial(_gates_kernel, tg=tg, chunk=min(MLSTM_CHUNK, t)),
        grid=(bsz, t // tg),
        in_specs=[pl.BlockSpec((1, tg, D_MODEL), lambda b, i: (b, i, 0)),
                  full((D_MODEL, 128)), full((D_MODEL, 128)), full((16, D_MODEL)), full((16, D_MODEL)),
                  full((16, 1)), full((16, 1)), full((1, 128)), full((1, 128))],
        out_specs=[pl.BlockSpec((1, 16, tg), lambda b, i: (b, 0, i)),
                   pl.BlockSpec((1, tg, 128), lambda b, i: (b, i, 0))],
        out_shape=[jax.ShapeDtypeStruct((bsz, 16, t), F32), jax.ShapeDtypeStruct((bsz, t, 128), F32)],
        scratch_shapes=[pltpu.VMEM((16, 128), F32)],
        compiler_params=_cparams(("parallel", "arbitrary")), name="gates",
    )(hb3, *gw)


def _mlstm_kernel(c_ref, v_ref, o_ref, rows_ref, cols_ref, cw_ref, cb_ref, wq_ref, wk_ref, ng_ref,
                  y_ref, cbuf, c_state, n_state, m_state, *, chunk):
    L = chunk
    t = pl.program_id(1)

    @pl.when(t == 0)
    def _():
        cbuf[0:8, :] = jnp.zeros((8, D_MODEL), F32)
        c_state[...] = jnp.zeros_like(c_state)
        n_state[...] = jnp.zeros_like(n_state)
        m_state[...] = jnp.zeros_like(m_state)

    x = c_ref[0].astype(F32)
    cbuf[8:8 + L, :] = x
    conv = cb_ref[...]
    for j in range(CONV_K):
        conv = conv + cbuf[8 - (CONV_K - 1) + j:8 - (CONV_K - 1) + j + L, :] * cw_ref[j:j + 1, :]
    cbuf[0:8, :] = x[L - 8:L, :]
    ub = (conv * _sigmoid(conv)).astype(BF16)

    r_i = lax.broadcasted_iota(jnp.int32, (L, L), 0)
    c_i = lax.broadcasted_iota(jnp.int32, (L, L), 1)
    causal = c_i <= r_i
    for h in range(H_A):
        hs = slice(h * DH_A, (h + 1) * DH_A)
        uh = ub[:, hs]
        q = jnp.dot(uh, wq_ref[h], preferred_element_type=F32)
        k = jnp.dot(uh, wk_ref[h], preferred_element_type=F32) * (DH_A ** -0.5)
        qb = q.astype(BF16)
        kb = k.astype(BF16)
        vb = v_ref[0, :, hs]
        b_col = cols_ref[0, :, h:h + 1]
        c_col = cols_ref[0, :, H_A + h:H_A + h + 1]
        b_row = rows_ref[0, h:h + 1, :]
        c_row = rows_ref[0, H_A + h:H_A + h + 1, :]
        m_prev = m_state[h, 0:1, 0:1]

        dmat = jnp.where(causal, b_col + c_row, -jnp.inf)
        inter = b_col + m_prev
        m_t = jnp.maximum(jnp.max(dmat, axis=1, keepdims=True), inter)
        s = lax.dot_general(qb, kb, NT_DIMS, preferred_element_type=F32) * jnp.exp(dmat - m_t)
        decay = jnp.exp(inter - m_t)
        c_old = c_state[h]
        n_old = n_state[h]
        num = (jnp.dot(s.astype(BF16), vb, preferred_element_type=F32)
               + decay * jnp.dot(qb, c_old.astype(BF16), preferred_element_type=F32))
        den = jnp.sum(s, axis=1, keepdims=True) + decay * jnp.sum(q * n_old, axis=1, keepdims=True)
        hh = num / jnp.maximum(jnp.abs(den), jnp.exp(-m_t))

        g = b_row[:, L - 1:L]
        a_col = g + c_col
        m_new = jnp.maximum(g + m_prev, jnp.max(a_col, axis=0, keepdims=True))
        kw = k * jnp.exp(a_col - m_new)
        sc = jnp.exp(g + m_prev - m_new)
        c_state[h] = sc * c_old + lax.dot_general(kw.astype(BF16), vb, TN_DIMS, preferred_element_type=F32)
        n_state[h] = sc * n_old + jnp.sum(kw, axis=0, keepdims=True)
        m_state[h] = jnp.broadcast_to(m_new, m_state.shape[1:])

        mu = jnp.mean(hh, axis=1, keepdims=True)
        hc = hh - mu
        var = jnp.mean(hc * hc, axis=1, keepdims=True)
        hn = hc * lax.rsqrt(var + LN_EPS) * ng_ref[:, hs]
        og = _sigmoid(o_ref[0, :, hs].astype(F32))
        y_ref[0, :, hs] = (og * hn).astype(BF16)


def _mlstm(p3, rows, cols, conv_w, conv_b, wq, wk, norm_g):
    bsz, t, _ = p3.shape
    L = min(MLSTM_CHUNK, t)
    colblk = lambda c: pl.BlockSpec((1, L, 1024), lambda b, i, c=c: (b, i, c))
    full = lambda shape: pl.BlockSpec(shape, lambda b, i: tuple(0 for _ in shape))
    return pl.pallas_call(
        functools.partial(_mlstm_kernel, chunk=L),
        grid=(bsz, t // L),
        in_specs=[colblk(COL_CA), colblk(COL_VA), colblk(COL_OA),
                  pl.BlockSpec((1, 16, L), lambda b, i: (b, 0, i)),
                  pl.BlockSpec((1, L, 128), lambda b, i: (b, i, 0)),
                  full((CONV_K, 1024)), full((1, 1024)), full((H_A, DH_A, DH_A)), full((H_A, DH_A, DH_A)),
                  full((1, 1024))],
        out_specs=pl.BlockSpec((1, L, 1024), lambda b, i: (b, i, 0)),
        out_shape=jax.ShapeDtypeStruct((bsz, t, 1024), BF16),
        scratch_shapes=[pltpu.VMEM((8 + L, 1024), F32), pltpu.VMEM((H_A, DH_A, DH_A), F32),
                        pltpu.VMEM((H_A, 1, DH_A), F32), pltpu.VMEM((H_A, 8, 128), F32)],
        compiler_params=_cparams(("parallel", "arbitrary")), name="mlstm",
    )(p3, p3, p3, rows, cols, conv_w, conv_b.reshape(1, -1), wq, wk, norm_g.reshape(1, -1))


FLASH_HEADS_PER_STEP = 2
LOG2E = math.log2(math.e)


def _flash_kernel(*refs, scale, tq, dk, heads, has_bias):
    if has_bias:
        q_ref, k_ref, v_ref, bias_ref, o_ref, m_sc, l_sc, acc_sc = refs
    else:
        q_ref, k_ref, v_ref, o_ref, m_sc, l_sc, acc_sc = refs
        bias_ref = None
    i = pl.program_id(2)
    m_sc[...] = jnp.full(m_sc.shape, -jnp.inf, F32)
    l_sc[...] = jnp.zeros_like(l_sc)
    acc_sc[...] = jnp.zeros_like(acc_sc)
    qs = [(q_ref[0, :, g * dk:(g + 1) * dk].astype(F32) * (scale * LOG2E)).astype(BF16) for g in range(heads)]

    def step(g, j, masked):
        start = pl.multiple_of(j * tq, tq)
        kj = k_ref[0, pl.ds(start, tq), g * dk:(g + 1) * dk]
        vj = v_ref[0, pl.ds(start, tq), g * 128:(g + 1) * 128]
        s = lax.dot_general(qs[g], kj, NT_DIMS, preferred_element_type=F32)
        if has_bias:
            s = s - bias_ref[g, j] * LOG2E
        if masked:
            r_i = lax.broadcasted_iota(jnp.int32, (tq, tq), 0)
            c_i = lax.broadcasted_iota(jnp.int32, (tq, tq), 1)
            s = jnp.where(c_i <= r_i, s, -jnp.inf)
        m_prev = m_sc[g]
        m_new = jnp.maximum(m_prev, jnp.max(s, axis=1, keepdims=True))
        alpha = jnp.exp2(m_prev - m_new)
        l_part = alpha * l_sc[g]
        ps = []
        for c in range(tq // 128):
            p_c = jnp.exp2(s[:, c * 128:(c + 1) * 128] - m_new)
            l_part = l_part + p_c
            ps.append(p_c.astype(BF16))
        p = jnp.concatenate(ps, axis=1)
        acc_sc[g] = alpha * acc_sc[g] + jnp.dot(p, vj, preferred_element_type=F32)
        l_sc[g] = l_part
        m_sc[g] = m_new

    def body(j, carry):
        for g in range(heads):
            step(g, j, False)
        return carry

    lax.fori_loop(0, i, body, 0)
    for g in range(heads):
        step(g, i, True)
        l = jnp.sum(l_sc[g], axis=1, keepdims=True)
        o_ref[0, :, g * 128:(g + 1) * 128] = (acc_sc[g] / l).astype(o_ref.dtype)


def _flash(q_arr, k_arr, v_arr, bias, *, n_heads, dk, q_blk, k_blk, v_blk, scale):
    bsz, t, _ = q_arr.shape
    tq = min(512, t)
    g = FLASH_HEADS_PER_STEP
    hp = n_heads // g
    in_specs = [pl.BlockSpec((1, tq, g * dk), lambda b, h, i: (b, i, q_blk // g + h)),
                pl.BlockSpec((1, t, g * dk), lambda b, h, i: (b, 0, k_blk // g + h)),
                pl.BlockSpec((1, t, g * 128), lambda b, h, i: (b, 0, v_blk // g + h))]
    args = [q_arr, k_arr, v_arr]
    if bias is not None:
        in_specs.append(pl.BlockSpec((g, t // tq, 1, tq), lambda b, h, i: (b * hp + h, 0, 0, 0)))
        args.append(bias)
    return pl.pallas_call(
        functools.partial(_flash_kernel, scale=scale, tq=tq, dk=dk, heads=g, has_bias=bias is not None),
        grid=(bsz, hp, t // tq),
        in_specs=in_specs,
        out_specs=pl.BlockSpec((1, tq, g * 128), lambda b, h, i: (b, i, h)),
        out_shape=jax.ShapeDtypeStruct((bsz, t, n_heads * 128), BF16),
        scratch_shapes=[pltpu.VMEM((g, tq, 128), F32), pltpu.VMEM((g, tq, 128), F32), pltpu.VMEM((g, tq, 128), F32)],
        compiler_params=_cparams(("parallel", "parallel", "arbitrary")),
        name="flash_fox" if bias is not None else "flash_mla",
    )(*args)


def _mla_prep_kernel(c_ref, tab_ref, gq_ref, gkv_ref, wq_ref, wkv_ref, q_out, k_out, v_out):
    c = c_ref[...].astype(F32)
    ckv = c[:, 0:KV_RANK]
    cq = c[:, KV_RANK:KV_RANK + Q_RANK]
    kr = c[:, KV_RANK + Q_RANK:]

    def rms(x, g):
        ms = jnp.mean(x * x, axis=-1, keepdims=True)
        return (x * lax.rsqrt(ms + RMS_EPS) * g).astype(BF16)

    qf = jnp.dot(rms(cq, gq_ref[...]), wq_ref[...], preferred_element_type=F32)
    kvf = jnp.dot(rms(ckv, gkv_ref[...]), wkv_ref[...], preferred_element_type=F32)
    tab = tab_ref[...]

    def rope(xx):
        pr = xx * tab
        return pr + pltpu.roll(pr, DH_ROPE, 1)

    lane = lax.broadcasted_iota(jnp.int32, kr.shape, 1)
    k_rope = jnp.where(lane < DH_ROPE, rope(kr), 0.0).astype(BF16)
    for h in range(H_C):
        lo = h * DK_C
        q_out[:, lo:lo + 128] = qf[:, lo:lo + 128].astype(BF16)
        q_out[:, lo + 128:lo + 256] = rope(qf[:, lo + 128:lo + 256]).astype(BF16)
        k_out[:, lo:lo + 128] = kvf[:, lo:lo + 128].astype(BF16)
        k_out[:, lo + 128:lo + 256] = k_rope
        v_out[:, h * DH_V:(h + 1) * DH_V] = kvf[:, lo + 128:lo + 256].astype(BF16)


def _mla_prep(p2, tab, g_cq, g_ckv, wq, wkv):
    n = p2.shape[0]
    tm = min(512, n)
    full = lambda shape: pl.BlockSpec(shape, lambda i: tuple(0 for _ in shape))
    return pl.pallas_call(
        _mla_prep_kernel, grid=(n // tm,),
        in_specs=[pl.BlockSpec((tm, LOWRANK_W), lambda i: (i, LOWRANK_OFF // LOWRANK_W)),
                  pl.BlockSpec((tm, 128), lambda i: (i, 0)),
                  full((1, Q_RANK)), full((1, KV_RANK)), full((Q_RANK, H_C * DK_C)), full((KV_RANK, H_C * DK_C))],
        out_specs=[pl.BlockSpec((tm, H_C * DK_C), lambda i: (i, 0)),
                   pl.BlockSpec((tm, H_C * DK_C), lambda i: (i, 0)),
                   pl.BlockSpec((tm, H_C * DH_V), lambda i: (i, 0))],
        out_shape=[jax.ShapeDtypeStruct((n, H_C * DK_C), BF16), jax.ShapeDtypeStruct((n, H_C * DK_C), BF16),
                   jax.ShapeDtypeStruct((n, H_C * DH_V), BF16)],
        compiler_params=_cparams(("parallel",)), name="mla_prep",
    )(p2, tab, g_cq.reshape(1, -1), g_ckv.reshape(1, -1), wq, wkv)


def _merge_kernel(ya_ref, yb_ref, yc_ref, za_ref, zb_ref, zc_ref, ga_ref, gb_ref, gc_ref, h_ref,
                  pa_ref, pb_ref, pc_ref, wo_ref, lg_ref, lb_ref, hf_ref, hb_ref, *, alpha):
    def branch(y_ref, z_ref, g_ref, p_ref):
        z = z_ref[...].astype(F32)
        y = (y_ref[...].astype(F32) * (z * _sigmoid(z))).astype(BF16)
        return _sigmoid(g_ref[...].astype(F32)) * jnp.dot(y, p_ref[...], preferred_element_type=F32)

    merged = (branch(ya_ref, za_ref, ga_ref, pa_ref) + branch(yb_ref, zb_ref, gb_ref, pb_ref)
              + branch(yc_ref, zc_ref, gc_ref, pc_ref))
    out = jnp.dot(merged.astype(BF16), wo_ref[...], preferred_element_type=F32)
    r = alpha * h_ref[...] + out
    mu = jnp.mean(r, axis=-1, keepdims=True)
    rc = r - mu
    var = jnp.mean(rc * rc, axis=-1, keepdims=True)
    y = rc * lax.rsqrt(var + LN_EPS) * lg_ref[...] + lb_ref[...]
    hf_ref[...] = y
    hb_ref[...] = y.astype(BF16)


def _merge(ya, yb, yc, p2, hf, pa, pb, pc, wo, ln_g, ln_b, alpha):
    n = hf.shape[0]
    tm = min(256, n)
    row = pl.BlockSpec((tm, D_MODEL), lambda i: (i, 0))
    colblk = lambda c: pl.BlockSpec((tm, 1024), lambda i, c=c: (i, c))
    wfull = pl.BlockSpec((1024, D_MODEL), lambda i: (0, 0))
    vec = pl.BlockSpec((1, D_MODEL), lambda i: (0, 0))
    return pl.pallas_call(
        functools.partial(_merge_kernel, alpha=alpha), grid=(n // tm,),
        in_specs=[row, row, row, colblk(COL_ZA), colblk(COL_ZB), colblk(COL_ZC),
                  colblk(COL_GA), colblk(COL_GB), colblk(COL_GC), row,
                  wfull, wfull, wfull, wfull, vec, vec],
        out_specs=[row, row],
        out_shape=[jax.ShapeDtypeStruct((n, D_MODEL), F32), jax.ShapeDtypeStruct((n, D_MODEL), BF16)],
        compiler_params=_cparams(("parallel",)), name="merge",
    )(ya, yb, yc, p2, p2, p2, p2, p2, p2, hf, pa, pb, pc, wo, ln_g.reshape(1, -1), ln_b.reshape(1, -1))


def _rot_cols(w):
    half = w.shape[-1] // 2
    return jnp.concatenate([-w[..., half:], w[..., :half]], axis=-1)


def _layer_weights(w_in, b_ia, b_fa, b_fb, w_uq, w_ukv):
    idx = []
    acc = 0
    for s in IN_SPLITS[:-1]:
        acc += s
        idx.append(acc)
    (w_ca, w_va, w_oa, w_ia, w_fa, w_za, w_qb, w_kb, w_vb, w_fb, w_zb,
     w_cq, w_ckv, w_kr, w_zc, w_ga, w_gb, w_gc) = jnp.split(w_in, idx, axis=-1)
    w_big = jnp.concatenate([w_ca, w_va, w_oa, w_za, w_qb, w_kb, w_vb, w_zb, w_zc, w_ga, w_gb, w_gc,
                             w_ckv, w_cq, w_kr, _rot_cols(w_kr)], axis=-1).astype(BF16)
    zpad = lambda k: jnp.zeros((D_MODEL, k), F32)
    w_f = jnp.concatenate([w_fa, w_fa, w_fb, zpad(112)], axis=-1)
    w_i = jnp.concatenate([zpad(4), w_ia, zpad(120)], axis=-1)
    b_f = jnp.concatenate([b_fa, b_fa, b_fb, jnp.zeros((112,), F32)])
    b_i = jnp.concatenate([jnp.zeros((4,), F32), b_ia, jnp.zeros((120,), F32)])
    gw = (w_f.astype(BF16), w_i.astype(BF16), w_f[:, :16].T.astype(BF16), w_i[:, :16].T.astype(BF16),
          b_f[:16].reshape(16, 1), b_i[:16].reshape(16, 1), b_f.reshape(1, 128), b_i.reshape(1, 128))
    wq3 = w_uq.reshape(Q_RANK, H_C, DH_NOPE + DH_ROPE)
    wq_rope = wq3[..., DH_NOPE:]
    wq = jnp.concatenate([wq3, _rot_cols(wq_rope)], axis=-1).reshape(Q_RANK, H_C * DK_C).astype(BF16)
    return w_big, gw, wq, w_ukv.astype(BF16)


def kernel(x, positions, ln_in_g, ln_in_b, w_in, conv_w, conv_b, w_qa, w_ka, b_ia, b_fa, norm_ga, b_fb,
           g_cq, w_uq, g_ckv, w_ukv, p_a, p_b, p_c, w_out, ln_g, ln_b):
    bsz, t, _ = x.shape
    n = bsz * t
    depth = w_in.shape[0]
    alpha = (2.0 * depth) ** 0.25
    tq = min(512, t)

    tab = _rope_table(positions)
    hf, hb = _input_ln(x.reshape(n, D_MODEL), ln_in_g, ln_in_b)
    for l in range(depth):
        w_big, gw, wq, wkv = _layer_weights(w_in[l], b_ia[l], b_fa[l], b_fb[l], w_uq[l], w_ukv[l])
        p2 = _in_proj(hb, w_big)
        p3 = p2.reshape(bsz, t, N_BIG)
        rows, cols = _gates(hb.reshape(bsz, t, D_MODEL), gw)
        ya = _mlstm(p3, rows, cols, conv_w[l], conv_b[l], w_qa[l].astype(BF16), w_ka[l].astype(BF16), norm_ga[l])
        fox_bias = rows[:, 8:16, :].reshape(bsz * H_B, t // tq, 1, tq)
        yb = _flash(p3, p3, p3, fox_bias, n_heads=H_B, dk=DH_B, q_blk=COL_QB * 8, k_blk=COL_KB * 8,
                    v_blk=COL_VB * 8, scale=DH_B ** -0.5)
        qc, kc, vc = _mla_prep(p2, tab, g_cq[l], g_ckv[l], wq, wkv)
        yc = _flash(qc.reshape(bsz, t, -1), kc.reshape(bsz, t, -1), vc.reshape(bsz, t, -1), None,
                    n_heads=H_C, dk=DK_C, q_blk=0, k_blk=0, v_blk=0, scale=(DH_NOPE + DH_ROPE) ** -0.5)
        hf, hb = _merge(ya.reshape(n, -1), yb.reshape(n, -1), yc.reshape(n, -1), p2, hf,
                        p_a[l].astype(BF16), p_b[l].astype(BF16), p_c[l].astype(BF16), w_out[l].astype(BF16),
                        ln_g[l], ln_b[l], alpha)
    return hf.reshape(bsz, t, D_MODEL)
```

```python
import functools
import math

import jax
import jax.numpy as jnp
from jax import lax
from jax.experimental import pallas as pl
from jax.experimental.pallas import tpu as pltpu

F32 = jnp.float32
BF16 = jnp.bfloat16

D_MODEL = 1024
H_A, DH_A, CONV_K = 4, 256, 4
H_B, DH_B = 8, 128
H_C, DH_NOPE, DH_ROPE, DH_V = 8, 128, 64, 128
Q_RANK, KV_RANK = 384, 256
ROPE_BASE = 10000.0
LN_EPS = 1e-5
RMS_EPS = 1e-6
IN_SPLITS = (1024, 1024, 1024, 4, 4, 1024,
             1024, 1024, 1024, 8, 1024,
             Q_RANK, KV_RANK, DH_ROPE, 1024,
             1024, 1024, 1024)

COL_CA, COL_VA, COL_OA, COL_ZA, COL_QB, COL_KB, COL_VB, COL_ZB, COL_ZC, COL_GA, COL_GB, COL_GC = range(12)
LOWRANK_OFF = 12 * 1024
LOWRANK_W = 768
N_BIG = LOWRANK_OFF + LOWRANK_W
DK_C = 256

MLSTM_CHUNK = 256
FLASH_TQ = 1024
FLASH_TK = 512
FLASH_HEADS_PER_STEP = 4
VMEM_LIMIT = 56 * 1024 * 1024
LOG2E = math.log2(math.e)

NT_DIMS = (((1,), (1,)), ((), ()))
TN_DIMS = (((0,), (0,)), ((), ()))


def _cparams(sem):
    return pltpu.CompilerParams(dimension_semantics=sem, vmem_limit_bytes=VMEM_LIMIT)


def _sigmoid(x):
    return 1.0 / (1.0 + jnp.exp(-x))


def _log_sigmoid(x):
    return jnp.minimum(x, 0.0) - jnp.log1p(jnp.exp(-jnp.abs(x)))


def _split3(x):
    hi = x.astype(BF16)
    r = x - hi.astype(F32)
    mid = r.astype(BF16)
    lo = (r - mid.astype(F32)).astype(BF16)
    return hi, mid, lo


def _ln_kernel(x_ref, g_ref, b_ref, hf_ref, hb_ref):
    x = x_ref[...]
    mu = jnp.mean(x, axis=-1, keepdims=True)
    xc = x - mu
    var = jnp.mean(xc * xc, axis=-1, keepdims=True)
    y = xc * lax.rsqrt(var + LN_EPS) * g_ref[...] + b_ref[...]
    hf_ref[...] = y
    hb_ref[...] = y.astype(BF16)


def _input_ln(x2, g, b):
    n = x2.shape[0]
    tm = min(512, n)
    row = pl.BlockSpec((tm, D_MODEL), lambda i: (i, 0))
    vec = pl.BlockSpec((1, D_MODEL), lambda i: (0, 0))
    return pl.pallas_call(
        _ln_kernel, grid=(n // tm,), in_specs=[row, vec, vec], out_specs=[row, row],
        out_shape=[jax.ShapeDtypeStruct((n, D_MODEL), F32), jax.ShapeDtypeStruct((n, D_MODEL), BF16)],
        compiler_params=_cparams(("parallel",)), name="input_ln",
    )(x2, g.reshape(1, -1), b.reshape(1, -1))


def _rope_kernel(pos_ref, freq_ref, tab_ref):
    ang = pos_ref[...].astype(F32) * freq_ref[...]
    lane = lax.broadcasted_iota(jnp.int32, ang.shape, 1)
    tab_ref[...] = jnp.where(lane < DH_ROPE, jnp.cos(ang), jnp.sin(ang))


def _rope_table(positions):
    n = positions.size
    tm = min(1024, n)
    inv_freq = ROPE_BASE ** (-jnp.arange(0, DH_ROPE, 2, dtype=F32) / DH_ROPE)
    freq = jnp.tile(inv_freq, 4).reshape(1, 128)
    return pl.pallas_call(
        _rope_kernel, grid=(n // tm,),
        in_specs=[pl.BlockSpec((tm, 1), lambda i: (i, 0)), pl.BlockSpec((1, 128), lambda i: (0, 0))],
        out_specs=pl.BlockSpec((tm, 128), lambda i: (i, 0)),
        out_shape=jax.ShapeDtypeStruct((n, 128), F32),
        compiler_params=_cparams(("parallel",)), name="rope_table",
    )(positions.reshape(n, 1), freq)


def _matmul_kernel(x_ref, w_ref, o_ref):
    o_ref[...] = jnp.dot(x_ref[...], w_ref[...], preferred_element_type=F32).astype(o_ref.dtype)


def _in_proj(hb, w_big):
    n = hb.shape[0]
    tm = min(1024, n)
    tn = N_BIG // 6
    return pl.pallas_call(
        _matmul_kernel, grid=(n // tm, N_BIG // tn),
        in_specs=[pl.BlockSpec((tm, D_MODEL), lambda i, j: (i, 0)),
                  pl.BlockSpec((D_MODEL, tn), lambda i, j: (0, j))],
        out_specs=pl.BlockSpec((tm, tn), lambda i, j: (i, j)),
        out_shape=jax.ShapeDtypeStruct((n, N_BIG), BF16),
        compiler_params=_cparams(("parallel", "arbitrary")), name="in_proj",
    )(hb, w_big)


def _gates_kernel(h_ref, wf_ref, wi_ref, wft_ref, wit_ref, bfc_ref, bic_ref, bfr_ref, bir_ref,
                  rows_ref, cols_ref, kb_ref, carry_ref, *, tg, chunk):
    t = pl.program_id(1)

    @pl.when(t == 0)
    def _():
        carry_ref[...] = jnp.zeros_like(carry_ref)

    hb = h_ref[0]
    r_i = lax.broadcasted_iota(jnp.int32, (tg, tg), 0)
    c_i = lax.broadcasted_iota(jnp.int32, (tg, tg), 1)
    same_chunk = (r_i // chunk) == (c_i // chunk)
    ind = lambda mask: jnp.where(mask, 1.0, 0.0).astype(BF16)
    upper_bd = ind((r_i <= c_i) & same_chunk)
    lower = ind(r_i >= c_i)
    lower_bd = ind((r_i >= c_i) & same_chunk)

    gf = lax.dot_general(wft_ref[...], hb, NT_DIMS, preferred_element_type=F32) + bfc_ref[...]
    gi = lax.dot_general(wit_ref[...], hb, NT_DIMS, preferred_element_type=F32) + bic_ref[...]
    y_bd = jnp.zeros((16, tg), F32)
    for part in _split3(_log_sigmoid(gf)):
        y_bd = y_bd + jnp.dot(part, upper_bd, preferred_element_type=F32)
    row = lax.broadcasted_iota(jnp.int32, (16, tg), 0)
    rows_ref[0] = jnp.where(row < 4, y_bd, gi - y_bd)[0:8]

    gfc = jnp.dot(hb, wf_ref[...], preferred_element_type=F32) + bfr_ref[...]
    gic = jnp.dot(hb, wi_ref[...], preferred_element_type=F32) + bir_ref[...]
    yc = jnp.zeros((tg, 128), F32)
    y_full = jnp.zeros((tg, 128), F32)
    for part in _split3(_log_sigmoid(gfc)):
        yc = yc + jnp.dot(lower_bd, part, preferred_element_type=F32)
        y_full = y_full + jnp.dot(lower, part, preferred_element_type=F32)
    col = lax.broadcasted_iota(jnp.int32, (tg, 128), 1)
    cols_ref[0] = jnp.where(col < 4, yc, gic - yc)

    cum = y_full + carry_ref[0:1, :]
    carry_ref[...] = jnp.broadcast_to(cum[tg - 1:tg, :], carry_ref.shape)
    parts = jnp.concatenate(_split3(cum * LOG2E), axis=1)
    sr = lax.broadcasted_iota(jnp.int32, (3 * 128, H_B * 128), 0)
    sc = lax.broadcasted_iota(jnp.int32, (3 * 128, H_B * 128), 1)
    scatter = ind((sr % 128 == 8 + sc // 128) & (sr // 128 == sc % 128))
    kb_ref[0] = jnp.dot(parts, scatter, preferred_element_type=F32).astype(BF16)


def _gates(hb3, gw):
    bsz, t, _ = hb3.shape
    tg = min(512, t)
    full = lambda shape: pl.BlockSpec(shape, lambda b, i: tuple(0 for _ in shape))
    return pl.pallas_call(
        functools.partial(_gates_kernel, tg=tg, chunk=min(MLSTM_CHUNK, t)),
        grid=(bsz, t // tg),
        in_specs=[pl.BlockSpec((1, tg, D_MODEL), lambda b, i: (b, i, 0)),
                  full((D_MODEL, 128)), full((D_MODEL, 128)), full((16, D_MODEL)), full((16, D_MODEL)),
                  full((16, 1)), full((16, 1)), full((1, 128)), full((1, 128))],
        out_specs=[pl.BlockSpec((1, 8, tg), lambda b, i: (b, 0, i)),
                   pl.BlockSpec((1, tg, 128), lambda b, i: (b, i, 0)),
                   pl.BlockSpec((1, tg, H_B * 128), lambda b, i: (b, i, 0))],
        out_shape=[jax.ShapeDtypeStruct((bsz, 8, t), F32), jax.ShapeDtypeStruct((bsz, t, 128), F32),
                   jax.ShapeDtypeStruct((bsz, t, H_B * 128), BF16)],
        scratch_shapes=[pltpu.VMEM((8, 128), F32)],
        compiler_params=_cparams(("parallel", "arbitrary")), name="gates",
    )(hb3, *gw)


def _mlstm_kernel(c_ref, v_ref, o_ref, rows_ref, cols_ref, cw_ref, cb_ref, wq_ref, wk_ref, ng_ref,
                  y_ref, cbuf, c_state, m_state, *, chunk):
    L = chunk
    t = pl.program_id(1)

    @pl.when(t == 0)
    def _():
        cbuf[0:8, :] = jnp.zeros((8, D_MODEL), F32)
        c_state[...] = jnp.zeros_like(c_state)
        m_state[...] = jnp.zeros_like(m_state)

    x = c_ref[0].astype(F32)
    cbuf[8:8 + L, :] = x
    conv = cb_ref[...]
    for j in range(CONV_K):
        conv = conv + cbuf[8 - (CONV_K - 1) + j:8 - (CONV_K - 1) + j + L, :] * cw_ref[j:j + 1, :]
    cbuf[0:8, :] = x[L - 8:L, :]
    ub = (conv * _sigmoid(conv)).astype(BF16)

    nc = L // 128
    lanes = lambda c: slice(c * 128, (c + 1) * 128)
    r_i = lax.broadcasted_iota(jnp.int32, (L, 128), 0)
    c_i = lax.broadcasted_iota(jnp.int32, (L, 128), 1)
    causal = [c_i + c * 128 <= r_i for c in range(nc)]
    ones_blk = jnp.ones((L, 128), BF16)
    for h in range(H_A):
        hs = slice(h * DH_A, (h + 1) * DH_A)
        uh = ub[:, hs]
        q = jnp.dot(uh, wq_ref[h], preferred_element_type=F32)
        k = jnp.dot(uh, wk_ref[h], preferred_element_type=F32) * (DH_A ** -0.5)
        qb = q.astype(BF16)
        kb = k.astype(BF16)
        v_aug = jnp.concatenate([v_ref[0, :, hs], ones_blk], axis=1)
        b_rep = jnp.broadcast_to(cols_ref[0, :, h:h + 1], (L, 128))
        c_rep = jnp.broadcast_to(cols_ref[0, :, H_A + h:H_A + h + 1], (L, 128))
        b_row = rows_ref[0, h:h + 1, :]
        c_row = rows_ref[0, H_A + h:H_A + h + 1, :]
        m_prev = m_state[h, 0:1, :]

        dm = [jnp.where(causal[c], b_rep + c_row[:, lanes(c)], -jnp.inf) for c in range(nc)]
        dmax = dm[0]
        for c in range(1, nc):
            dmax = jnp.maximum(dmax, dm[c])
        inter = b_rep + m_prev
        m_t = jnp.maximum(jnp.max(dmax, axis=1, keepdims=True), inter)
        qk = lax.dot_general(qb, kb, NT_DIMS, preferred_element_type=F32)
        s = jnp.concatenate([(qk[:, lanes(c)] * jnp.exp(dm[c] - m_t)).astype(BF16) for c in range(nc)], axis=1)
        decay = jnp.exp(inter - m_t)
        c_old = c_state[h]
        inter_acc = jnp.dot(qb, c_old.astype(BF16), preferred_element_type=F32)
        acc = jnp.dot(s, v_aug, preferred_element_type=F32) + jnp.tile(decay, (1, DH_A // 128 + 1)) * inter_acc
        den = acc[:, DH_A:]
        inv = 1.0 / jnp.maximum(jnp.abs(den), jnp.exp(-m_t))
        hh = acc[:, :DH_A] * jnp.tile(inv, (1, DH_A // 128))

        g = b_row[:, L - 1:L]
        a_rep = g + c_rep
        m_new = jnp.maximum(g + m_prev, jnp.max(a_rep, axis=0, keepdims=True))
        kw = (k * jnp.tile(jnp.exp(a_rep - m_new), (1, DH_A // 128))).astype(BF16)
        sc = jnp.exp(g + m_prev - m_new)
        c_state[h] = (jnp.tile(sc, (1, DH_A // 128 + 1)) * c_old
                      + lax.dot_general(kw, v_aug, TN_DIMS, preferred_element_type=F32))
        m_state[h] = jnp.broadcast_to(m_new, m_state.shape[1:])

        mu = jnp.mean(hh, axis=1, keepdims=True)
        hc = hh - mu
        var = jnp.mean(hc * hc, axis=1, keepdims=True)
        hn = hc * lax.rsqrt(var + LN_EPS) * ng_ref[:, hs]
        og = _sigmoid(o_ref[0, :, hs].astype(F32))
        y_ref[0, :, hs] = (og * hn).astype(BF16)


def _mlstm(p3, rows, cols, conv_w, conv_b, wq, wk, norm_g):
    bsz, t, _ = p3.shape
    L = min(MLSTM_CHUNK, t)
    colblk = lambda c: pl.BlockSpec((1, L, 1024), lambda b, i, c=c: (b, i, c))
    full = lambda shape: pl.BlockSpec(shape, lambda b, i: tuple(0 for _ in shape))
    return pl.pallas_call(
        functools.partial(_mlstm_kernel, chunk=L),
        grid=(bsz, t // L),
        in_specs=[colblk(COL_CA), colblk(COL_VA), colblk(COL_OA),
                  pl.BlockSpec((1, 8, L), lambda b, i: (b, 0, i)),
                  pl.BlockSpec((1, L, 128), lambda b, i: (b, i, 0)),
                  full((CONV_K, 1024)), full((1, 1024)), full((H_A, DH_A, DH_A)), full((H_A, DH_A, DH_A)),
                  full((1, 1024))],
        out_specs=pl.BlockSpec((1, L, 1024), lambda b, i: (b, i, 0)),
        out_shape=jax.ShapeDtypeStruct((bsz, t, 1024), BF16),
        scratch_shapes=[pltpu.VMEM((8 + L, 1024), F32), pltpu.VMEM((H_A, DH_A, DH_A + 128), F32),
                        pltpu.VMEM((H_A, 8, 128), F32)],
        compiler_params=_cparams(("parallel", "arbitrary")), name="mlstm",
    )(p3, p3, p3, rows, cols, conv_w, conv_b.reshape(1, -1), wq, wk, norm_g.reshape(1, -1))


def _flash_kernel(*refs, scale, tq, tk, dk, heads, has_bias):
    if has_bias:
        q_ref, k_ref, kb_ref, v_ref, o_ref, m_sc, l_sc, acc_sc = refs
    else:
        q_ref, k_ref, v_ref, o_ref, m_sc, l_sc, acc_sc = refs
        kb_ref = None
    i = pl.program_id(2)
    m_sc[...] = jnp.full(m_sc.shape, -jnp.inf, F32)
    l_sc[...] = jnp.zeros_like(l_sc)
    acc_sc[...] = jnp.zeros_like(acc_sc)
    qs = []
    for g in range(heads):
        q = (q_ref[0, :, g * dk:(g + 1) * dk].astype(F32) * (scale * LOG2E)).astype(BF16)
        if has_bias:
            lane = lax.broadcasted_iota(jnp.int32, (tq, 128), 1)
            q = jnp.concatenate([q, jnp.where(lane < 3, -1.0, 0.0).astype(BF16)], axis=1)
        qs.append(q)

    def step(g, key0, nk, r0, nr, mask_off):
        ks = pl.ds(key0, nk)
        rs = slice(r0, r0 + nr)
        kj = k_ref[0, ks, g * dk:(g + 1) * dk]
        if has_bias:
            kj = jnp.concatenate([kj, kb_ref[0, ks, g * 128:(g + 1) * 128]], axis=1)
        vj = v_ref[0, ks, g * 128:(g + 1) * 128]
        s = lax.dot_general(qs[g][rs], kj, NT_DIMS, preferred_element_type=F32)
        if mask_off is not None:
            r_i = r0 + lax.broadcasted_iota(jnp.int32, (nr, nk), 0)
            c_i = mask_off + lax.broadcasted_iota(jnp.int32, (nr, nk), 1)
            s = jnp.where(c_i <= r_i, s, -jnp.inf)
        m_prev = m_sc[g, rs]
        m_new = jnp.maximum(m_prev, jnp.max(s, axis=1, keepdims=True))
        alpha = jnp.exp2(m_prev - m_new)
        l_part = alpha * l_sc[g, rs]
        ps = []
        for c in range(nk // 128):
            p_c = jnp.exp2(s[:, c * 128:(c + 1) * 128] - m_new)
            l_part = l_part + p_c
            ps.append(p_c.astype(BF16))
        p = jnp.concatenate(ps, axis=1)
        acc_sc[g, rs] = alpha * acc_sc[g, rs] + jnp.dot(p, vj, preferred_element_type=F32)
        l_sc[g, rs] = l_part
        m_sc[g, rs] = m_new

    nsub = tq // tk

    def body(j, carry):
        for g in range(heads):
            step(g, pl.multiple_of(j * tk, tk), tk, 0, tq, None)
        return carry

    lax.fori_loop(0, i * nsub, body, 0)
    for d in range(nsub):
        for g in range(heads):
            step(g, pl.multiple_of(i * tq + d * tk, tk), tk, d * tk, tq - d * tk, d * tk)
    for g in range(heads):
        l = jnp.sum(l_sc[g], axis=1, keepdims=True)
        o_ref[0, :, g * 128:(g + 1) * 128] = (acc_sc[g] / l).astype(o_ref.dtype)


def _flash(q_arr, k_arr, kb_arr, v_arr, *, n_heads, dk, q_blk, k_blk, v_blk, scale):
    bsz, t, _ = q_arr.shape
    tq = min(FLASH_TQ, t)
    tk = min(FLASH_TK, t)
    g = FLASH_HEADS_PER_STEP
    hp = n_heads // g
    in_specs = [pl.BlockSpec((1, tq, g * dk), lambda b, h, i: (b, i, q_blk // g + h)),
                pl.BlockSpec((1, t, g * dk), lambda b, h, i: (b, 0, k_blk // g + h))]
    args = [q_arr, k_arr]
    if kb_arr is not None:
        in_specs.append(pl.BlockSpec((1, t, g * 128), lambda b, h, i: (b, 0, h)))
        args.append(kb_arr)
    in_specs.append(pl.BlockSpec((1, t, g * 128), lambda b, h, i: (b, 0, v_blk // g + h)))
    args.append(v_arr)
    return pl.pallas_call(
        functools.partial(_flash_kernel, scale=scale, tq=tq, tk=tk, dk=dk, heads=g, has_bias=kb_arr is not None),
        grid=(bsz, hp, t // tq),
        in_specs=in_specs,
        out_specs=pl.BlockSpec((1, tq, g * 128), lambda b, h, i: (b, i, h)),
        out_shape=jax.ShapeDtypeStruct((bsz, t, n_heads * 128), BF16),
        scratch_shapes=[pltpu.VMEM((g, tq, 128), F32), pltpu.VMEM((g, tq, 128), F32), pltpu.VMEM((g, tq, 128), F32)],
        compiler_params=_cparams(("parallel", "parallel", "arbitrary")),
        name="flash_fox" if kb_arr is not None else "flash_mla",
    )(*args)


def _mla_prep_kernel(c_ref, tab_ref, gq_ref, gkv_ref, wq_ref, wkv_ref, q_out, k_out, v_out):
    c = c_ref[...].astype(F32)
    ckv = c[:, 0:KV_RANK]
    cq = c[:, KV_RANK:KV_RANK + Q_RANK]
    kr = c[:, KV_RANK + Q_RANK:]

    def rms(x, g):
        ms = jnp.mean(x * x, axis=-1, keepdims=True)
        return (x * lax.rsqrt(ms + RMS_EPS) * g).astype(BF16)

    qf = jnp.dot(rms(cq, gq_ref[...]), wq_ref[...], preferred_element_type=F32)
    kvf = jnp.dot(rms(ckv, gkv_ref[...]), wkv_ref[...], preferred_element_type=F32)
    tab = tab_ref[...]

    def rope(xx):
        pr = xx * tab
        return pr + pltpu.roll(pr, DH_ROPE, 1)

    lane = lax.broadcasted_iota(jnp.int32, kr.shape, 1)
    k_rope = jnp.where(lane < DH_ROPE, rope(kr), 0.0).astype(BF16)
    for h in range(H_C):
        lo = h * DK_C
        q_out[:, lo:lo + 128] = qf[:, lo:lo + 128].astype(BF16)
        q_out[:, lo + 128:lo + 256] = rope(qf[:, lo + 128:lo + 256]).astype(BF16)
        k_out[:, lo:lo + 128] = kvf[:, lo:lo + 128].astype(BF16)
        k_out[:, lo + 128:lo + 256] = k_rope
        v_out[:, h * DH_V:(h + 1) * DH_V] = kvf[:, lo + 128:lo + 256].astype(BF16)


def _mla_prep(p2, tab, g_cq, g_ckv, wq, wkv):
    n = p2.shape[0]
    tm = min(512, n)
    full = lambda shape: pl.BlockSpec(shape, lambda i: tuple(0 for _ in shape))
    return pl.pallas_call(
        _mla_prep_kernel, grid=(n // tm,),
        in_specs=[pl.BlockSpec((tm, LOWRANK_W), lambda i: (i, LOWRANK_OFF // LOWRANK_W)),
                  pl.BlockSpec((tm, 128), lambda i: (i, 0)),
                  full((1, Q_RANK)), full((1, KV_RANK)), full((Q_RANK, H_C * DK_C)), full((KV_RANK, H_C * DK_C))],
        out_specs=[pl.BlockSpec((tm, H_C * DK_C), lambda i: (i, 0)),
                   pl.BlockSpec((tm, H_C * DK_C), lambda i: (i, 0)),
                   pl.BlockSpec((tm, H_C * DH_V), lambda i: (i, 0))],
        out_shape=[jax.ShapeDtypeStruct((n, H_C * DK_C), BF16), jax.ShapeDtypeStruct((n, H_C * DK_C), BF16),
                   jax.ShapeDtypeStruct((n, H_C * DH_V), BF16)],
        compiler_params=_cparams(("parallel",)), name="mla_prep",
    )(p2, tab, g_cq.reshape(1, -1), g_ckv.reshape(1, -1), wq, wkv)


def _merge_kernel(ya_ref, yb_ref, yc_ref, za_ref, zb_ref, zc_ref, ga_ref, gb_ref, gc_ref, h_ref,
                  pa_ref, pb_ref, pc_ref, wo_ref, lg_ref, lb_ref, hf_ref, hb_ref, *, alpha):
    def branch(y_ref, z_ref, g_ref, p_ref):
        hz = z_ref[...] * 0.5
        y = y_ref[...] * (hz + hz * jnp.tanh(hz))
        gate = 0.5 + 0.5 * jnp.tanh(g_ref[...] * 0.5)
        return gate.astype(F32) * jnp.dot(y, p_ref[...], preferred_element_type=F32)

    merged = (branch(ya_ref, za_ref, ga_ref, pa_ref) + branch(yb_ref, zb_ref, gb_ref, pb_ref)
              + branch(yc_ref, zc_ref, gc_ref, pc_ref))
    out = jnp.dot(merged.astype(BF16), wo_ref[...], preferred_element_type=F32)
    r = alpha * h_ref[...] + out
    mu = jnp.mean(r, axis=-1, keepdims=True)
    rc = r - mu
    var = jnp.mean(rc * rc, axis=-1, keepdims=True)
    y = rc * lax.rsqrt(var + LN_EPS) * lg_ref[...] + lb_ref[...]
    hf_ref[...] = y
    hb_ref[...] = y.astype(BF16)


def _merge(ya, yb, yc, p2, hf, pa, pb, pc, wo, ln_g, ln_b, alpha):
    n = hf.shape[0]
    tm = min(256, n)
    row = pl.BlockSpec((tm, D_MODEL), lambda i: (i, 0))
    colblk = lambda c: pl.BlockSpec((tm, 1024), lambda i, c=c: (i, c))
    wfull = pl.BlockSpec((1024, D_MODEL), lambda i: (0, 0))
    vec = pl.BlockSpec((1, D_MODEL), lambda i: (0, 0))
    return pl.pallas_call(
        functools.partial(_merge_kernel, alpha=alpha), grid=(n // tm,),
        in_specs=[row, row, row, colblk(COL_ZA), colblk(COL_ZB), colblk(COL_ZC),
                  colblk(COL_GA), colblk(COL_GB), colblk(COL_GC), row,
                  wfull, wfull, wfull, wfull, vec, vec],
        out_specs=[row, row],
        out_shape=[jax.ShapeDtypeStruct((n, D_MODEL), F32), jax.ShapeDtypeStruct((n, D_MODEL), BF16)],
        compiler_params=_cparams(("parallel",)), name="merge",
    )(ya, yb, yc, p2, p2, p2, p2, p2, p2, hf, pa, pb, pc, wo, ln_g.reshape(1, -1), ln_b.reshape(1, -1))


def _rot_cols(w):
    half = w.shape[-1] // 2
    return jnp.concatenate([-w[..., half:], w[..., :half]], axis=-1)


def _layer_weights(w_in, b_ia, b_fa, b_fb, w_uq, w_ukv):
    idx = []
    acc = 0
    for s in IN_SPLITS[:-1]:
        acc += s
        idx.append(acc)
    (w_ca, w_va, w_oa, w_ia, w_fa, w_za, w_qb, w_kb, w_vb, w_fb, w_zb,
     w_cq, w_ckv, w_kr, w_zc, w_ga, w_gb, w_gc) = jnp.split(w_in, idx, axis=-1)
    w_big = jnp.concatenate([w_ca, w_va, w_oa, w_za, w_qb, w_kb, w_vb, w_zb, w_zc, w_ga, w_gb, w_gc,
                             w_ckv, w_cq, w_kr, _rot_cols(w_kr)], axis=-1).astype(BF16)
    zpad = lambda k: jnp.zeros((D_MODEL, k), F32)
    w_f = jnp.concatenate([w_fa, w_fa, w_fb, zpad(112)], axis=-1)
    w_i = jnp.concatenate([zpad(4), w_ia, zpad(120)], axis=-1)
    b_f = jnp.concatenate([b_fa, b_fa, b_fb, jnp.zeros((112,), F32)])
    b_i = jnp.concatenate([jnp.zeros((4,), F32), b_ia, jnp.zeros((120,), F32)])
    gw = (w_f.astype(BF16), w_i.astype(BF16), w_f[:, :16].T.astype(BF16), w_i[:, :16].T.astype(BF16),
          b_f[:16].reshape(16, 1), b_i[:16].reshape(16, 1), b_f.reshape(1, 128), b_i.reshape(1, 128))
    wq3 = w_uq.reshape(Q_RANK, H_C, DH_NOPE + DH_ROPE)
    wq_rope = wq3[..., DH_NOPE:]
    wq = jnp.concatenate([wq3, _rot_cols(wq_rope)], axis=-1).reshape(Q_RANK, H_C * DK_C).astype(BF16)
    return w_big, gw, wq, w_ukv.astype(BF16)


def kernel(x, positions, ln_in_g, ln_in_b, w_in, conv_w, conv_b, w_qa, w_ka, b_ia, b_fa, norm_ga, b_fb,
           g_cq, w_uq, g_ckv, w_ukv, p_a, p_b, p_c, w_out, ln_g, ln_b):
    bsz, t, _ = x.shape
    n = bsz * t
    depth = w_in.shape[0]
    alpha = (2.0 * depth) ** 0.25

    tab = _rope_table(positions)
    hf, hb = _input_ln(x.reshape(n, D_MODEL), ln_in_g, ln_in_b)
    for l in range(depth):
        w_big, gw, wq, wkv = _layer_weights(w_in[l], b_ia[l], b_fa[l], b_fb[l], w_uq[l], w_ukv[l])
        p2 = _in_proj(hb, w_big)
        p3 = p2.reshape(bsz, t, N_BIG)
        rows, cols, kb = _gates(hb.reshape(bsz, t, D_MODEL), gw)
        ya = _mlstm(p3, rows, cols, conv_w[l], conv_b[l], w_qa[l].astype(BF16), w_ka[l].astype(BF16), norm_ga[l])
        yb = _flash(p3, p3, kb, p3, n_heads=H_B, dk=DH_B, q_blk=COL_QB * 8, k_blk=COL_KB * 8,
                    v_blk=COL_VB * 8, scale=DH_B ** -0.5)
        qc, kc, vc = _mla_prep(p2, tab, g_cq[l], g_ckv[l], wq, wkv)
        yc = _flash(qc.reshape(bsz, t, -1), kc.reshape(bsz, t, -1), None, vc.reshape(bsz, t, -1),
                    n_heads=H_C, dk=DK_C, q_blk=0, k_blk=0, v_blk=0, scale=(DH_NOPE + DH_ROPE) ** -0.5)
        hf, hb = _merge(ya.reshape(n, -1), yb.reshape(n, -1), yc.reshape(n, -1), p2, hf,
                        p_a[l].astype(BF16), p_b[l].astype(BF16), p_c[l].astype(BF16), w_out[l].astype(BF16),
                        ln_g[l], ln_b[l], alpha)
    return hf.reshape(bsz, t, D_MODEL)
```

```python
import functools
import math

import jax
import jax.numpy as jnp
from jax import lax
from jax.experimental import pallas as pl
from jax.experimental.pallas import tpu as pltpu

F32 = jnp.float32
BF16 = jnp.bfloat16

D_MODEL = 1024
H_A, DH_A, CONV_K = 4, 256, 4
H_B, DH_B = 8, 128
H_C, DH_NOPE, DH_ROPE, DH_V = 8, 128, 64, 128
Q_RANK, KV_RANK = 384, 256
ROPE_BASE = 10000.0
LN_EPS = 1e-5
RMS_EPS = 1e-6
IN_SPLITS = (1024, 1024, 1024, 4, 4, 1024,
             1024, 1024, 1024, 8, 1024,
             Q_RANK, KV_RANK, DH_ROPE, 1024,
             1024, 1024, 1024)

COL_CA, COL_VA, COL_OA, COL_ZA, COL_QB, COL_KB, COL_VB, COL_ZB, COL_ZC, COL_GA, COL_GB, COL_GC = range(12)
LOWRANK_OFF = 12 * 1024
LOWRANK_W = 768
N_BIG = LOWRANK_OFF + LOWRANK_W
DK_C = 256

MLSTM_CHUNK = 256
FLASH_TQ = 1024
FLASH_TK = 512
FLASH_HEADS_PER_STEP = 4
VMEM_LIMIT = 56 * 1024 * 1024
LOG2E = math.log2(math.e)

NT_DIMS = (((1,), (1,)), ((), ()))
TN_DIMS = (((0,), (0,)), ((), ()))


def _cparams(sem):
    return pltpu.CompilerParams(dimension_semantics=sem, vmem_limit_bytes=VMEM_LIMIT)


def _sigmoid(x):
    return 0.5 + 0.5 * jnp.tanh(0.5 * x)


def _silu(x):
    hx = 0.5 * x
    return hx + hx * jnp.tanh(hx)


def _log_sigmoid(x):
    return jnp.minimum(x, 0.0) - jnp.log1p(jnp.exp(-jnp.abs(x)))


def _split3(x):
    hi = x.astype(BF16)
    r = x - hi.astype(F32)
    mid = r.astype(BF16)
    lo = (r - mid.astype(F32)).astype(BF16)
    return hi, mid, lo


def _ln_kernel(x_ref, g_ref, b_ref, hf_ref, hb_ref):
    x = x_ref[...]
    mu = jnp.mean(x, axis=-1, keepdims=True)
    xc = x - mu
    var = jnp.mean(xc * xc, axis=-1, keepdims=True)
    y = xc * lax.rsqrt(var + LN_EPS) * g_ref[...] + b_ref[...]
    hf_ref[...] = y
    hb_ref[...] = y.astype(BF16)


def _input_ln(x2, g, b):
    n = x2.shape[0]
    tm = min(512, n)
    row = pl.BlockSpec((tm, D_MODEL), lambda i: (i, 0))
    vec = pl.BlockSpec((1, D_MODEL), lambda i: (0, 0))
    return pl.pallas_call(
        _ln_kernel, grid=(n // tm,), in_specs=[row, vec, vec], out_specs=[row, row],
        out_shape=[jax.ShapeDtypeStruct((n, D_MODEL), F32), jax.ShapeDtypeStruct((n, D_MODEL), BF16)],
        compiler_params=_cparams(("parallel",)), name="input_ln",
    )(x2, g.reshape(1, -1), b.reshape(1, -1))


def _rope_kernel(pos_ref, freq_ref, tab_ref):
    ang = pos_ref[...].astype(F32) * freq_ref[...]
    lane = lax.broadcasted_iota(jnp.int32, ang.shape, 1)
    tab_ref[...] = jnp.where(lane < DH_ROPE, jnp.cos(ang), jnp.sin(ang))


def _rope_table(positions):
    n = positions.size
    tm = min(1024, n)
    inv_freq = ROPE_BASE ** (-jnp.arange(0, DH_ROPE, 2, dtype=F32) / DH_ROPE)
    freq = jnp.tile(inv_freq, 4).reshape(1, 128)
    return pl.pallas_call(
        _rope_kernel, grid=(n // tm,),
        in_specs=[pl.BlockSpec((tm, 1), lambda i: (i, 0)), pl.BlockSpec((1, 128), lambda i: (0, 0))],
        out_specs=pl.BlockSpec((tm, 128), lambda i: (i, 0)),
        out_shape=jax.ShapeDtypeStruct((n, 128), F32),
        compiler_params=_cparams(("parallel",)), name="rope_table",
    )(positions.reshape(n, 1), freq)


def _matmul_kernel(x_ref, w_ref, o_ref):
    o_ref[...] = jnp.dot(x_ref[...], w_ref[...], preferred_element_type=F32).astype(o_ref.dtype)


def _in_proj(hb, w_big):
    n = hb.shape[0]
    tm = min(1024, n)
    tn = N_BIG // 6
    return pl.pallas_call(
        _matmul_kernel, grid=(n // tm, N_BIG // tn),
        in_specs=[pl.BlockSpec((tm, D_MODEL), lambda i, j: (i, 0)),
                  pl.BlockSpec((D_MODEL, tn), lambda i, j: (0, j))],
        out_specs=pl.BlockSpec((tm, tn), lambda i, j: (i, j)),
        out_shape=jax.ShapeDtypeStruct((n, N_BIG), BF16),
        compiler_params=_cparams(("parallel", "arbitrary")), name="in_proj",
    )(hb, w_big)


def _gates_kernel(h_ref, wfi_ref, wfit_ref, bfc_ref, bic_ref, bfr_ref, bir_ref,
                  rows_ref, cols_ref, kb_ref, carry_ref, *, tg, chunk):
    t = pl.program_id(1)

    @pl.when(t == 0)
    def _():
        carry_ref[...] = jnp.zeros_like(carry_ref)

    hb = h_ref[0]
    r_i = lax.broadcasted_iota(jnp.int32, (tg, tg), 0)
    c_i = lax.broadcasted_iota(jnp.int32, (tg, tg), 1)
    same_chunk = (r_i // chunk) == (c_i // chunk)
    ind = lambda mask: jnp.where(mask, 1.0, 0.0).astype(BF16)
    upper_bd = ind((r_i <= c_i) & same_chunk)
    lower = ind(r_i >= c_i)
    lower_bd = ind((r_i >= c_i) & same_chunk)

    g_rows = lax.dot_general(wfit_ref[...], hb, NT_DIMS, preferred_element_type=F32)
    gf = g_rows[0:16] + bfc_ref[...]
    gi = g_rows[16:32] + bic_ref[...]
    y3 = jnp.dot(jnp.concatenate(_split3(_log_sigmoid(gf)), axis=0), upper_bd, preferred_element_type=F32)
    y_bd = y3[0:16] + y3[16:32] + y3[32:48]
    row = lax.broadcasted_iota(jnp.int32, (16, tg), 0)
    rows_ref[0] = jnp.where(row < 4, y_bd, gi - y_bd)[0:8]

    g_cols = jnp.dot(hb, wfi_ref[...], preferred_element_type=F32)
    gfc = g_cols[:, 0:128] + bfr_ref[...]
    gic = g_cols[:, 128:256] + bir_ref[...]
    lf_parts = jnp.concatenate(_split3(_log_sigmoid(gfc)), axis=1)
    sum3 = lambda a: a[:, 0:128] + a[:, 128:256] + a[:, 256:384]
    yc = sum3(jnp.dot(lower_bd, lf_parts, preferred_element_type=F32))
    y_full = sum3(jnp.dot(lower, lf_parts, preferred_element_type=F32))
    col = lax.broadcasted_iota(jnp.int32, (tg, 128), 1)
    cols_ref[0] = jnp.where(col < 4, yc, gic - yc)

    cum = y_full + carry_ref[0:1, :]
    carry_ref[...] = jnp.broadcast_to(cum[tg - 1:tg, :], carry_ref.shape)
    parts = jnp.concatenate(_split3(cum * LOG2E), axis=1)
    sr = lax.broadcasted_iota(jnp.int32, (3 * 128, H_B * 128), 0)
    sc = lax.broadcasted_iota(jnp.int32, (3 * 128, H_B * 128), 1)
    scatter = ind((sr % 128 == 8 + sc // 128) & (sr // 128 == sc % 128))
    kb_ref[0] = jnp.dot(parts, scatter, preferred_element_type=F32).astype(BF16)


def _gates(hb3, gw):
    bsz, t, _ = hb3.shape
    tg = min(512, t)
    full = lambda shape: pl.BlockSpec(shape, lambda b, i: tuple(0 for _ in shape))
    return pl.pallas_call(
        functools.partial(_gates_kernel, tg=tg, chunk=min(MLSTM_CHUNK, t)),
        grid=(bsz, t // tg),
        in_specs=[pl.BlockSpec((1, tg, D_MODEL), lambda b, i: (b, i, 0)),
                  full((D_MODEL, 256)), full((32, D_MODEL)),
                  full((16, 1)), full((16, 1)), full((1, 128)), full((1, 128))],
        out_specs=[pl.BlockSpec((1, 8, tg), lambda b, i: (b, 0, i)),
                   pl.BlockSpec((1, tg, 128), lambda b, i: (b, i, 0)),
                   pl.BlockSpec((1, tg, H_B * 128), lambda b, i: (b, i, 0))],
        out_shape=[jax.ShapeDtypeStruct((bsz, 8, t), F32), jax.ShapeDtypeStruct((bsz, t, 128), F32),
                   jax.ShapeDtypeStruct((bsz, t, H_B * 128), BF16)],
        scratch_shapes=[pltpu.VMEM((8, 128), F32)],
        compiler_params=_cparams(("parallel", "arbitrary")), name="gates",
    )(hb3, *gw)


def _mlstm_kernel(c_ref, v_ref, o_ref, rows_ref, cols_ref, cw_ref, cb_ref, wq_ref, wk_ref, ng_ref,
                  y_ref, cbuf, c_state, m_state, *, chunk):
    L = chunk
    t = pl.program_id(1)

    @pl.when(t == 0)
    def _():
        cbuf[0:8, :] = jnp.zeros((8, D_MODEL), F32)
        c_state[...] = jnp.zeros_like(c_state)
        m_state[...] = jnp.zeros_like(m_state)

    x = c_ref[0].astype(F32)
    cbuf[8:8 + L, :] = x
    conv = cb_ref[...]
    for j in range(CONV_K):
        conv = conv + cbuf[8 - (CONV_K - 1) + j:8 - (CONV_K - 1) + j + L, :] * cw_ref[j:j + 1, :]
    cbuf[0:8, :] = x[L - 8:L, :]
    ub = _silu(conv).astype(BF16)

    nc = L // 128
    lanes = lambda c: slice(c * 128, (c + 1) * 128)
    r_i = lax.broadcasted_iota(jnp.int32, (L, 128), 0)
    c_i = lax.broadcasted_iota(jnp.int32, (L, 128), 1)
    causal = [c_i + c * 128 <= r_i for c in range(nc)]
    ones_blk = jnp.ones((L, 128), BF16)
    for h in range(H_A):
        hs = slice(h * DH_A, (h + 1) * DH_A)
        uh = ub[:, hs]
        q = jnp.dot(uh, wq_ref[h], preferred_element_type=F32)
        k = jnp.dot(uh, wk_ref[h], preferred_element_type=F32) * (DH_A ** -0.5)
        qb = q.astype(BF16)
        kb = k.astype(BF16)
        v_aug = jnp.concatenate([v_ref[0, :, hs], ones_blk], axis=1)
        b_rep = jnp.broadcast_to(cols_ref[0, :, h:h + 1], (L, 128))
        c_rep = jnp.broadcast_to(cols_ref[0, :, H_A + h:H_A + h + 1], (L, 128))
        b_row = rows_ref[0, h:h + 1, :]
        c_row = rows_ref[0, H_A + h:H_A + h + 1, :]
        m_prev = m_state[h, 0:1, :]

        dm = [jnp.where(causal[c], b_rep + c_row[:, lanes(c)], -jnp.inf) for c in range(nc)]
        dmax = dm[0]
        for c in range(1, nc):
            dmax = jnp.maximum(dmax, dm[c])
        inter = b_rep + m_prev
        m_t = jnp.maximum(jnp.max(dmax, axis=1, keepdims=True), inter)
        qk = lax.dot_general(qb, kb, NT_DIMS, preferred_element_type=F32)
        s = jnp.concatenate([(qk[:, lanes(c)] * jnp.exp(dm[c] - m_t)).astype(BF16) for c in range(nc)], axis=1)
        decay = jnp.exp(inter - m_t)
        c_old = c_state[h]
        inter_acc = jnp.dot(qb, c_old.astype(BF16), preferred_element_type=F32)
        acc = jnp.dot(s, v_aug, preferred_element_type=F32) + jnp.tile(decay, (1, DH_A // 128 + 1)) * inter_acc
        den = acc[:, DH_A:]
        inv = 1.0 / jnp.maximum(jnp.abs(den), jnp.exp(-m_t))
        hh = acc[:, :DH_A] * jnp.tile(inv, (1, DH_A // 128))

        g = b_row[:, L - 1:L]
        a_rep = g + c_rep
        m_new = jnp.maximum(g + m_prev, jnp.max(a_rep, axis=0, keepdims=True))
        kw = (k * jnp.tile(jnp.exp(a_rep - m_new), (1, DH_A // 128))).astype(BF16)
        sc = jnp.exp(g + m_prev - m_new)
        c_state[h] = (jnp.tile(sc, (1, DH_A // 128 + 1)) * c_old
                      + lax.dot_general(kw, v_aug, TN_DIMS, preferred_element_type=F32))
        m_state[h] = jnp.broadcast_to(m_new, m_state.shape[1:])

        mu = jnp.mean(hh, axis=1, keepdims=True)
        hc = hh - mu
        var = jnp.mean(hc * hc, axis=1, keepdims=True)
        hn = hc * lax.rsqrt(var + LN_EPS) * ng_ref[:, hs]
        og = _sigmoid(o_ref[0, :, hs])
        y_ref[0, :, hs] = (og.astype(F32) * hn).astype(BF16)


def _mlstm(p3, rows, cols, conv_w, conv_b, wq, wk, norm_g):
    bsz, t, _ = p3.shape
    L = min(MLSTM_CHUNK, t)
    colblk = lambda c: pl.BlockSpec((1, L, 1024), lambda b, i, c=c: (b, i, c))
    full = lambda shape: pl.BlockSpec(shape, lambda b, i: tuple(0 for _ in shape))
    return pl.pallas_call(
        functools.partial(_mlstm_kernel, chunk=L),
        grid=(bsz, t // L),
        in_specs=[colblk(COL_CA), colblk(COL_VA), colblk(COL_OA),
                  pl.BlockSpec((1, 8, L), lambda b, i: (b, 0, i)),
                  pl.BlockSpec((1, L, 128), lambda b, i: (b, i, 0)),
                  full((CONV_K, 1024)), full((1, 1024)), full((H_A, DH_A, DH_A)), full((H_A, DH_A, DH_A)),
                  full((1, 1024))],
        out_specs=pl.BlockSpec((1, L, 1024), lambda b, i: (b, i, 0)),
        out_shape=jax.ShapeDtypeStruct((bsz, t, 1024), BF16),
        scratch_shapes=[pltpu.VMEM((8 + L, 1024), F32), pltpu.VMEM((H_A, DH_A, DH_A + 128), F32),
                        pltpu.VMEM((H_A, 8, 128), F32)],
        compiler_params=_cparams(("parallel", "arbitrary")), name="mlstm",
    )(p3, p3, p3, rows, cols, conv_w, conv_b.reshape(1, -1), wq, wk, norm_g.reshape(1, -1))


def _flash_kernel(*refs, scale, tq, tk, dk, heads, has_bias):
    if has_bias:
        q_ref, k_ref, kb_ref, v_ref, o_ref, m_sc, l_sc, acc_sc = refs
    else:
        q_ref, k_ref, v_ref, o_ref, m_sc, l_sc, acc_sc = refs
        kb_ref = None
    i = pl.program_id(2)
    m_sc[...] = jnp.full(m_sc.shape, -jnp.inf, F32)
    l_sc[...] = jnp.zeros_like(l_sc)
    acc_sc[...] = jnp.zeros_like(acc_sc)
    qs = []
    for g in range(heads):
        q = (q_ref[0, :, g * dk:(g + 1) * dk].astype(F32) * (scale * LOG2E)).astype(BF16)
        if has_bias:
            lane = lax.broadcasted_iota(jnp.int32, (tq, 128), 1)
            q = jnp.concatenate([q, jnp.where(lane < 3, -1.0, 0.0).astype(BF16)], axis=1)
        qs.append(q)

    def scores(g, key0, nk, r0, nr, mask_off):
        ks = pl.ds(key0, nk)
        kj = k_ref[0, ks, g * dk:(g + 1) * dk]
        if has_bias:
            kj = jnp.concatenate([kj, kb_ref[0, ks, g * 128:(g + 1) * 128]], axis=1)
        s = lax.dot_general(qs[g][r0:r0 + nr], kj, NT_DIMS, preferred_element_type=F32)
        if mask_off is not None:
            r_i = r0 + lax.broadcasted_iota(jnp.int32, (nr, nk), 0)
            c_i = mask_off + lax.broadcasted_iota(jnp.int32, (nr, nk), 1)
            s = jnp.where(c_i <= r_i, s, -jnp.inf)
        return s

    def update(g, s, key0, nk, r0, nr):
        rs = slice(r0, r0 + nr)
        vj = v_ref[0, pl.ds(key0, nk), g * 128:(g + 1) * 128]
        m_prev = m_sc[g, rs]
        m_new = jnp.maximum(m_prev, jnp.max(s, axis=1, keepdims=True))
        alpha = jnp.exp2(m_prev - m_new)
        l_part = alpha * l_sc[g, rs]
        ps = []
        for c in range(nk // 128):
            p_c = jnp.exp2(s[:, c * 128:(c + 1) * 128] - m_new)
            l_part = l_part + p_c
            ps.append(p_c.astype(BF16))
        p = jnp.concatenate(ps, axis=1)
        acc_sc[g, rs] = alpha * acc_sc[g, rs] + jnp.dot(p, vj, preferred_element_type=F32)
        l_sc[g, rs] = l_part
        m_sc[g, rs] = m_new

    def sweep(key0, nk, r0, nr, mask_off):
        s_cur = scores(0, key0, nk, r0, nr, mask_off)
        for g in range(heads):
            s_next = scores(g + 1, key0, nk, r0, nr, mask_off) if g + 1 < heads else None
            update(g, s_cur, key0, nk, r0, nr)
            s_cur = s_next

    nsub = tq // tk

    def body(j, carry):
        sweep(pl.multiple_of(j * tk, tk), tk, 0, tq, None)
        return carry

    lax.fori_loop(0, i * nsub, body, 0)
    for d in range(nsub):
        sweep(pl.multiple_of(i * tq + d * tk, tk), tk, d * tk, tq - d * tk, d * tk)
    for g in range(heads):
        l = jnp.sum(l_sc[g], axis=1, keepdims=True)
        o_ref[0, :, g * 128:(g + 1) * 128] = (acc_sc[g] / l).astype(o_ref.dtype)


def _flash(q_arr, k_arr, kb_arr, v_arr, *, n_heads, dk, q_blk, k_blk, v_blk, scale):
    bsz, t, _ = q_arr.shape
    tq = min(FLASH_TQ, t)
    tk = min(FLASH_TK, t)
    g = FLASH_HEADS_PER_STEP
    hp = n_heads // g
    in_specs = [pl.BlockSpec((1, tq, g * dk), lambda b, h, i: (b, i, q_blk // g + h)),
                pl.BlockSpec((1, t, g * dk), lambda b, h, i: (b, 0, k_blk // g + h))]
    args = [q_arr, k_arr]
    if kb_arr is not None:
        in_specs.append(pl.BlockSpec((1, t, g * 128), lambda b, h, i: (b, 0, h)))
        args.append(kb_arr)
    in_specs.append(pl.BlockSpec((1, t, g * 128), lambda b, h, i: (b, 0, v_blk // g + h)))
    args.append(v_arr)
    return pl.pallas_call(
        functools.partial(_flash_kernel, scale=scale, tq=tq, tk=tk, dk=dk, heads=g, has_bias=kb_arr is not None),
        grid=(bsz, hp, t // tq),
        in_specs=in_specs,
        out_specs=pl.BlockSpec((1, tq, g * 128), lambda b, h, i: (b, i, h)),
        out_shape=jax.ShapeDtypeStruct((bsz, t, n_heads * 128), BF16),
        scratch_shapes=[pltpu.VMEM((g, tq, 128), F32), pltpu.VMEM((g, tq, 128), F32), pltpu.VMEM((g, tq, 128), F32)],
        compiler_params=_cparams(("parallel", "parallel", "arbitrary")),
        name="flash_fox" if kb_arr is not None else "flash_mla",
    )(*args)


def _mla_prep_kernel(c_ref, tab_ref, gq_ref, gkv_ref, wq_ref, wkv_ref, q_out, k_out, v_out):
    c = c_ref[...].astype(F32)
    ckv = c[:, 0:KV_RANK]
    cq = c[:, KV_RANK:KV_RANK + Q_RANK]
    kr = c[:, KV_RANK + Q_RANK:]

    def rms(x, g):
        ms = jnp.mean(x * x, axis=-1, keepdims=True)
        return (x * lax.rsqrt(ms + RMS_EPS) * g).astype(BF16)

    qf = jnp.dot(rms(cq, gq_ref[...]), wq_ref[...], preferred_element_type=F32)
    kvf = jnp.dot(rms(ckv, gkv_ref[...]), wkv_ref[...], preferred_element_type=F32)
    tab = tab_ref[...]

    def rope(xx):
        pr = xx * tab
        return pr + pltpu.roll(pr, DH_ROPE, 1)

    lane = lax.broadcasted_iota(jnp.int32, kr.shape, 1)
    k_rope = jnp.where(lane < DH_ROPE, rope(kr), 0.0).astype(BF16)
    for h in range(H_C):
        lo = h * DK_C
        q_out[:, lo:lo + 128] = qf[:, lo:lo + 128].astype(BF16)
        q_out[:, lo + 128:lo + 256] = rope(qf[:, lo + 128:lo + 256]).astype(BF16)
        k_out[:, lo:lo + 128] = kvf[:, lo:lo + 128].astype(BF16)
        k_out[:, lo + 128:lo + 256] = k_rope
        v_out[:, h * DH_V:(h + 1) * DH_V] = kvf[:, lo + 128:lo + 256].astype(BF16)


def _mla_prep(p2, tab, g_cq, g_ckv, wq, wkv):
    n = p2.shape[0]
    tm = min(512, n)
    full = lambda shape: pl.BlockSpec(shape, lambda i: tuple(0 for _ in shape))
    return pl.pallas_call(
        _mla_prep_kernel, grid=(n // tm,),
        in_specs=[pl.BlockSpec((tm, LOWRANK_W), lambda i: (i, LOWRANK_OFF // LOWRANK_W)),
                  pl.BlockSpec((tm, 128), lambda i: (i, 0)),
                  full((1, Q_RANK)), full((1, KV_RANK)), full((Q_RANK, H_C * DK_C)), full((KV_RANK, H_C * DK_C))],
        out_specs=[pl.BlockSpec((tm, H_C * DK_C), lambda i: (i, 0)),
                   pl.BlockSpec((tm, H_C * DK_C), lambda i: (i, 0)),
                   pl.BlockSpec((tm, H_C * DH_V), lambda i: (i, 0))],
        out_shape=[jax.ShapeDtypeStruct((n, H_C * DK_C), BF16), jax.ShapeDtypeStruct((n, H_C * DK_C), BF16),
                   jax.ShapeDtypeStruct((n, H_C * DH_V), BF16)],
        compiler_params=_cparams(("parallel",)), name="mla_prep",
    )(p2, tab, g_cq.reshape(1, -1), g_ckv.reshape(1, -1), wq, wkv)


def _merge_kernel(ya_ref, yb_ref, yc_ref, za_ref, zb_ref, zc_ref, ga_ref, gb_ref, gc_ref, h_ref,
                  pa_ref, pb_ref, pc_ref, wo_ref, lg_ref, lb_ref, hf_ref, hb_ref, *, alpha):
    def branch(y_ref, z_ref, g_ref, p_ref):
        hz = z_ref[...] * 0.5
        y = y_ref[...] * (hz + hz * jnp.tanh(hz))
        gate = 0.5 + 0.5 * jnp.tanh(g_ref[...] * 0.5)
        return gate.astype(F32) * jnp.dot(y, p_ref[...], preferred_element_type=F32)

    merged = (branch(ya_ref, za_ref, ga_ref, pa_ref) + branch(yb_ref, zb_ref, gb_ref, pb_ref)
              + branch(yc_ref, zc_ref, gc_ref, pc_ref))
    out = jnp.dot(merged.astype(BF16), wo_ref[...], preferred_element_type=F32)
    r = alpha * h_ref[...] + out
    mu = jnp.mean(r, axis=-1, keepdims=True)
    rc = r - mu
    var = jnp.mean(rc * rc, axis=-1, keepdims=True)
    y = rc * lax.rsqrt(var + LN_EPS) * lg_ref[...] + lb_ref[...]
    hf_ref[...] = y
    hb_ref[...] = y.astype(BF16)


def _merge(ya, yb, yc, p2, hf, pa, pb, pc, wo, ln_g, ln_b, alpha):
    n = hf.shape[0]
    tm = min(512, n)
    row = pl.BlockSpec((tm, D_MODEL), lambda i: (i, 0))
    colblk = lambda c: pl.BlockSpec((tm, 1024), lambda i, c=c: (i, c))
    wfull = pl.BlockSpec((1024, D_MODEL), lambda i: (0, 0), pipeline_mode=pl.Buffered(1))
    vec = pl.BlockSpec((1, D_MODEL), lambda i: (0, 0))
    return pl.pallas_call(
        functools.partial(_merge_kernel, alpha=alpha), grid=(n // tm,),
        in_specs=[row, row, row, colblk(COL_ZA), colblk(COL_ZB), colblk(COL_ZC),
                  colblk(COL_GA), colblk(COL_GB), colblk(COL_GC), row,
                  wfull, wfull, wfull, wfull, vec, vec],
        out_specs=[row, row],
        out_shape=[jax.ShapeDtypeStruct((n, D_MODEL), F32), jax.ShapeDtypeStruct((n, D_MODEL), BF16)],
        compiler_params=_cparams(("parallel",)), name="merge",
    )(ya, yb, yc, p2, p2, p2, p2, p2, p2, hf, pa, pb, pc, wo, ln_g.reshape(1, -1), ln_b.reshape(1, -1))


def _rot_cols(w):
    half = w.shape[-1] // 2
    return jnp.concatenate([-w[..., half:], w[..., :half]], axis=-1)


def _layer_weights(w_in, b_ia, b_fa, b_fb, w_uq, w_ukv):
    idx = []
    acc = 0
    for s in IN_SPLITS[:-1]:
        acc += s
        idx.append(acc)
    (w_ca, w_va, w_oa, w_ia, w_fa, w_za, w_qb, w_kb, w_vb, w_fb, w_zb,
     w_cq, w_ckv, w_kr, w_zc, w_ga, w_gb, w_gc) = jnp.split(w_in, idx, axis=-1)
    w_big = jnp.concatenate([w_ca, w_va, w_oa, w_za, w_qb, w_kb, w_vb, w_zb, w_zc, w_ga, w_gb, w_gc,
                             w_ckv, w_cq, w_kr, _rot_cols(w_kr)], axis=-1).astype(BF16)
    zpad = lambda k: jnp.zeros((D_MODEL, k), F32)
    w_f = jnp.concatenate([w_fa, w_fa, w_fb, zpad(112)], axis=-1)
    w_i = jnp.concatenate([zpad(4), w_ia, zpad(120)], axis=-1)
    b_f = jnp.concatenate([b_fa, b_fa, b_fb, jnp.zeros((112,), F32)])
    b_i = jnp.concatenate([jnp.zeros((4,), F32), b_ia, jnp.zeros((120,), F32)])
    gw = (jnp.concatenate([w_f, w_i], axis=1).astype(BF16),
          jnp.concatenate([w_f[:, :16], w_i[:, :16]], axis=1).T.astype(BF16),
          b_f[:16].reshape(16, 1), b_i[:16].reshape(16, 1), b_f.reshape(1, 128), b_i.reshape(1, 128))
    wq3 = w_uq.reshape(Q_RANK, H_C, DH_NOPE + DH_ROPE)
    wq_rope = wq3[..., DH_NOPE:]
    wq = jnp.concatenate([wq3, _rot_cols(wq_rope)], axis=-1).reshape(Q_RANK, H_C * DK_C).astype(BF16)
    return w_big, gw, wq, w_ukv.astype(BF16)


def kernel(x, positions, ln_in_g, ln_in_b, w_in, conv_w, conv_b, w_qa, w_ka, b_ia, b_fa, norm_ga, b_fb,
           g_cq, w_uq, g_ckv, w_ukv, p_a, p_b, p_c, w_out, ln_g, ln_b):
    bsz, t, _ = x.shape
    n = bsz * t
    depth = w_in.shape[0]
    alpha = (2.0 * depth) ** 0.25

    tab = _rope_table(positions)
    hf, hb = _input_ln(x.reshape(n, D_MODEL), ln_in_g, ln_in_b)
    for l in range(depth):
        w_big, gw, wq, wkv = _layer_weights(w_in[l], b_ia[l], b_fa[l], b_fb[l], w_uq[l], w_ukv[l])
        p2 = _in_proj(hb, w_big)
        p3 = p2.reshape(bsz, t, N_BIG)
        rows, cols, kb = _gates(hb.reshape(bsz, t, D_MODEL), gw)
        ya = _mlstm(p3, rows, cols, conv_w[l], conv_b[l], w_qa[l].astype(BF16), w_ka[l].astype(BF16), norm_ga[l])
        yb = _flash(p3, p3, kb, p3, n_heads=H_B, dk=DH_B, q_blk=COL_QB * 8, k_blk=COL_KB * 8,
                    v_blk=COL_VB * 8, scale=DH_B ** -0.5)
        qc, kc, vc = _mla_prep(p2, tab, g_cq[l], g_ckv[l], wq, wkv)
        yc = _flash(qc.reshape(bsz, t, -1), kc.reshape(bsz, t, -1), None, vc.reshape(bsz, t, -1),
                    n_heads=H_C, dk=DK_C, q_blk=0, k_blk=0, v_blk=0, scale=(DH_NOPE + DH_ROPE) ** -0.5)
        hf, hb = _merge(ya.reshape(n, -1), yb.reshape(n, -1), yc.reshape(n, -1), p2, hf,
                        p_a[l].astype(BF16), p_b[l].astype(BF16), p_c[l].astype(BF16), w_out[l].astype(BF16),
                        ln_g[l], ln_b[l], alpha)
    return hf.reshape(bsz, t, D_MODEL)
```

```python
import functools
import math

import jax
import jax.numpy as jnp
from jax import lax
from jax.experimental import pallas as pl
from jax.experimental.pallas import tpu as pltpu

F32 = jnp.float32
BF16 = jnp.bfloat16

D_MODEL = 1024
H_A, DH_A, CONV_K = 4, 256, 4
H_B, DH_B = 8, 128
H_C, DH_NOPE, DH_ROPE, DH_V = 8, 128, 64, 128
Q_RANK, KV_RANK = 384, 256
ROPE_BASE = 10000.0
LN_EPS = 1e-5
RMS_EPS = 1e-6
IN_SPLITS = (1024, 1024, 1024, 4, 4, 1024,
             1024, 1024, 1024, 8, 1024,
             Q_RANK, KV_RANK, DH_ROPE, 1024,
             1024, 1024, 1024)

COL_CA, COL_VA, COL_OA, COL_ZA, COL_QB, COL_KB, COL_VB, COL_ZB, COL_ZC, COL_GA, COL_GB, COL_GC = range(12)
LOWRANK_OFF = 12 * 1024
LOWRANK_W = 768
N_BIG = LOWRANK_OFF + LOWRANK_W
DK_C = 256

MLSTM_CHUNK = 256
FLASH_TQ = 1024
FLASH_TK = 512
FLASH_HEADS_PER_STEP = 4
VMEM_LIMIT = 56 * 1024 * 1024
LOG2E = math.log2(math.e)

NT_DIMS = (((1,), (1,)), ((), ()))
TN_DIMS = (((0,), (0,)), ((), ()))


def _cparams(sem):
    return pltpu.CompilerParams(dimension_semantics=sem, vmem_limit_bytes=VMEM_LIMIT)


def _sigmoid(x):
    return 0.5 + 0.5 * jnp.tanh(0.5 * x)


def _silu(x):
    hx = 0.5 * x
    return hx + hx * jnp.tanh(hx)


def _log_sigmoid(x):
    return jnp.minimum(x, 0.0) - jnp.log1p(jnp.exp(-jnp.abs(x)))


def _split3(x):
    hi = x.astype(BF16)
    r = x - hi.astype(F32)
    mid = r.astype(BF16)
    lo = (r - mid.astype(F32)).astype(BF16)
    return hi, mid, lo


def _ln_kernel(x_ref, g_ref, b_ref, hf_ref, hb_ref):
    x = x_ref[...]
    mu = jnp.mean(x, axis=-1, keepdims=True)
    xc = x - mu
    var = jnp.mean(xc * xc, axis=-1, keepdims=True)
    y = xc * lax.rsqrt(var + LN_EPS) * g_ref[...] + b_ref[...]
    hf_ref[...] = y
    hb_ref[...] = y.astype(BF16)


def _input_ln(x2, g, b):
    n = x2.shape[0]
    tm = min(512, n)
    row = pl.BlockSpec((tm, D_MODEL), lambda i: (i, 0))
    vec = pl.BlockSpec((1, D_MODEL), lambda i: (0, 0))
    return pl.pallas_call(
        _ln_kernel, grid=(n // tm,), in_specs=[row, vec, vec], out_specs=[row, row],
        out_shape=[jax.ShapeDtypeStruct((n, D_MODEL), F32), jax.ShapeDtypeStruct((n, D_MODEL), BF16)],
        compiler_params=_cparams(("parallel",)), name="input_ln",
    )(x2, g.reshape(1, -1), b.reshape(1, -1))


def _rope_kernel(pos_ref, freq_ref, tab_ref):
    ang = pos_ref[...].astype(F32) * freq_ref[...]
    lane = lax.broadcasted_iota(jnp.int32, ang.shape, 1)
    tab_ref[...] = jnp.where(lane < DH_ROPE, jnp.cos(ang), jnp.sin(ang))


def _rope_table(positions):
    n = positions.size
    tm = min(1024, n)
    inv_freq = ROPE_BASE ** (-jnp.arange(0, DH_ROPE, 2, dtype=F32) / DH_ROPE)
    freq = jnp.tile(inv_freq, 4).reshape(1, 128)
    return pl.pallas_call(
        _rope_kernel, grid=(n // tm,),
        in_specs=[pl.BlockSpec((tm, 1), lambda i: (i, 0)), pl.BlockSpec((1, 128), lambda i: (0, 0))],
        out_specs=pl.BlockSpec((tm, 128), lambda i: (i, 0)),
        out_shape=jax.ShapeDtypeStruct((n, 128), F32),
        compiler_params=_cparams(("parallel",)), name="rope_table",
    )(positions.reshape(n, 1), freq)


def _matmul_kernel(x_ref, w_ref, o_ref):
    o_ref[...] = jnp.dot(x_ref[...], w_ref[...], preferred_element_type=F32).astype(o_ref.dtype)


def _in_proj(hb, w_big):
    n = hb.shape[0]
    tm = min(1024, n)
    tn = N_BIG // 3
    return pl.pallas_call(
        _matmul_kernel, grid=(n // tm, N_BIG // tn),
        in_specs=[pl.BlockSpec((tm, D_MODEL), lambda i, j: (i, 0)),
                  pl.BlockSpec((D_MODEL, tn), lambda i, j: (0, j))],
        out_specs=pl.BlockSpec((tm, tn), lambda i, j: (i, j)),
        out_shape=jax.ShapeDtypeStruct((n, N_BIG), BF16),
        compiler_params=_cparams(("parallel", "arbitrary")), name="in_proj",
    )(hb, w_big)


def _gates_kernel(h_ref, wfi_ref, wfit_ref, bfc_ref, bic_ref, bfr_ref, bir_ref,
                  rows_ref, cols_ref, kb_ref, carry_ref, *, tg, chunk):
    t = pl.program_id(1)

    @pl.when(t == 0)
    def _():
        carry_ref[...] = jnp.zeros_like(carry_ref)

    hb = h_ref[0]
    r_i = lax.broadcasted_iota(jnp.int32, (tg, tg), 0)
    c_i = lax.broadcasted_iota(jnp.int32, (tg, tg), 1)
    same_chunk = (r_i // chunk) == (c_i // chunk)
    ind = lambda mask: jnp.where(mask, 1.0, 0.0).astype(BF16)
    upper_bd = ind((r_i <= c_i) & same_chunk)
    lower = ind(r_i >= c_i)
    lower_bd = ind((r_i >= c_i) & same_chunk)

    g_rows = lax.dot_general(wfit_ref[...], hb, NT_DIMS, preferred_element_type=F32)
    gf = g_rows[0:16] + bfc_ref[...]
    gi = g_rows[16:32] + bic_ref[...]
    y3 = jnp.dot(jnp.concatenate(_split3(_log_sigmoid(gf)), axis=0), upper_bd, preferred_element_type=F32)
    y_bd = y3[0:16] + y3[16:32] + y3[32:48]
    row = lax.broadcasted_iota(jnp.int32, (16, tg), 0)
    rows_ref[0] = jnp.where(row < 4, y_bd, gi - y_bd)[0:8]

    g_cols = jnp.dot(hb, wfi_ref[...], preferred_element_type=F32)
    gfc = g_cols[:, 0:128] + bfr_ref[...]
    gic = g_cols[:, 128:256] + bir_ref[...]
    lf_parts = jnp.concatenate(_split3(_log_sigmoid(gfc)), axis=1)
    sum3 = lambda a: a[:, 0:128] + a[:, 128:256] + a[:, 256:384]
    yc = sum3(jnp.dot(lower_bd, lf_parts, preferred_element_type=F32))
    y_full = sum3(jnp.dot(lower, lf_parts, preferred_element_type=F32))
    col = lax.broadcasted_iota(jnp.int32, (tg, 128), 1)
    cols_ref[0] = jnp.where(col < 4, yc, gic - yc)

    cum = y_full + carry_ref[0:1, :]
    carry_ref[...] = jnp.broadcast_to(cum[tg - 1:tg, :], carry_ref.shape)
    parts = jnp.concatenate(_split3(cum * LOG2E), axis=1)
    sr = lax.broadcasted_iota(jnp.int32, (3 * 128, H_B * 128), 0)
    sc = lax.broadcasted_iota(jnp.int32, (3 * 128, H_B * 128), 1)
    scatter = ind((sr % 128 == 8 + sc // 128) & (sr // 128 == sc % 128))
    kb_ref[0] = jnp.dot(parts, scatter, preferred_element_type=F32).astype(BF16)


def _gates(hb3, gw):
    bsz, t, _ = hb3.shape
    tg = min(512, t)
    full = lambda shape: pl.BlockSpec(shape, lambda b, i: tuple(0 for _ in shape))
    return pl.pallas_call(
        functools.partial(_gates_kernel, tg=tg, chunk=min(MLSTM_CHUNK, t)),
        grid=(bsz, t // tg),
        in_specs=[pl.BlockSpec((1, tg, D_MODEL), lambda b, i: (b, i, 0)),
                  full((D_MODEL, 256)), full((32, D_MODEL)),
                  full((16, 1)), full((16, 1)), full((1, 128)), full((1, 128))],
        out_specs=[pl.BlockSpec((1, 8, tg), lambda b, i: (b, 0, i)),
                   pl.BlockSpec((1, tg, 128), lambda b, i: (b, i, 0)),
                   pl.BlockSpec((1, tg, H_B * 128), lambda b, i: (b, i, 0))],
        out_shape=[jax.ShapeDtypeStruct((bsz, 8, t), F32), jax.ShapeDtypeStruct((bsz, t, 128), F32),
                   jax.ShapeDtypeStruct((bsz, t, H_B * 128), BF16)],
        scratch_shapes=[pltpu.VMEM((8, 128), F32)],
        compiler_params=_cparams(("parallel", "arbitrary")), name="gates",
    )(hb3, *gw)


def _mlstm_kernel(c_ref, v_ref, o_ref, rows_ref, cols_ref, cw_ref, cb_ref, wq_ref, wk_ref, ng_ref,
                  y_ref, cbuf, c_state, m_state, *, chunk):
    L = chunk
    t = pl.program_id(1)

    @pl.when(t == 0)
    def _():
        cbuf[0:8, :] = jnp.zeros((8, D_MODEL), F32)
        c_state[...] = jnp.zeros_like(c_state)
        m_state[...] = jnp.zeros_like(m_state)

    x = c_ref[0].astype(F32)
    cbuf[8:8 + L, :] = x
    conv = cb_ref[...]
    for j in range(CONV_K):
        conv = conv + cbuf[8 - (CONV_K - 1) + j:8 - (CONV_K - 1) + j + L, :] * cw_ref[j:j + 1, :]
    cbuf[0:8, :] = x[L - 8:L, :]
    ub = _silu(conv).astype(BF16)

    nc = L // 128
    lanes = lambda c: slice(c * 128, (c + 1) * 128)
    r_i = lax.broadcasted_iota(jnp.int32, (L, 128), 0)
    c_i = lax.broadcasted_iota(jnp.int32, (L, 128), 1)
    causal = [c_i + c * 128 <= r_i for c in range(nc)]
    ones_blk = jnp.ones((L, 128), BF16)
    for h in range(H_A):
        hs = slice(h * DH_A, (h + 1) * DH_A)
        uh = ub[:, hs]
        q = jnp.dot(uh, wq_ref[h], preferred_element_type=F32)
        k = jnp.dot(uh, wk_ref[h], preferred_element_type=F32) * (DH_A ** -0.5)
        qb = q.astype(BF16)
        kb = k.astype(BF16)
        v_aug = jnp.concatenate([v_ref[0, :, hs], ones_blk], axis=1)
        b_rep = jnp.broadcast_to(cols_ref[0, :, h:h + 1], (L, 128))
        c_rep = jnp.broadcast_to(cols_ref[0, :, H_A + h:H_A + h + 1], (L, 128))
        b_row = rows_ref[0, h:h + 1, :]
        c_row = rows_ref[0, H_A + h:H_A + h + 1, :]
        m_prev = m_state[h, 0:1, :]

        dm = [jnp.where(causal[c], b_rep + c_row[:, lanes(c)], -jnp.inf) for c in range(nc)]
        dmax = dm[0]
        for c in range(1, nc):
            dmax = jnp.maximum(dmax, dm[c])
        inter = b_rep + m_prev
        m_t = jnp.maximum(jnp.max(dmax, axis=1, keepdims=True), inter)
        qk = lax.dot_general(qb, kb, NT_DIMS, preferred_element_type=F32)
        s = jnp.concatenate([(qk[:, lanes(c)] * jnp.exp(dm[c] - m_t)).astype(BF16) for c in range(nc)], axis=1)
        decay = jnp.exp(inter - m_t)
        c_old = c_state[h]
        inter_acc = jnp.dot(qb, c_old.astype(BF16), preferred_element_type=F32)
        acc = jnp.dot(s, v_aug, preferred_element_type=F32) + jnp.tile(decay, (1, DH_A // 128 + 1)) * inter_acc
        den = acc[:, DH_A:]
        inv = 1.0 / jnp.maximum(jnp.abs(den), jnp.exp(-m_t))
        hh = acc[:, :DH_A] * jnp.tile(inv, (1, DH_A // 128))

        g = b_row[:, L - 1:L]
        a_rep = g + c_rep
        m_new = jnp.maximum(g + m_prev, jnp.max(a_rep, axis=0, keepdims=True))
        kw = (k * jnp.tile(jnp.exp(a_rep - m_new), (1, DH_A // 128))).astype(BF16)
        sc = jnp.exp(g + m_prev - m_new)
        c_state[h] = (jnp.tile(sc, (1, DH_A // 128 + 1)) * c_old
                      + lax.dot_general(kw, v_aug, TN_DIMS, preferred_element_type=F32))
        m_state[h] = jnp.broadcast_to(m_new, m_state.shape[1:])

        mu = jnp.mean(hh, axis=1, keepdims=True)
        hc = hh - mu
        var = jnp.mean(hc * hc, axis=1, keepdims=True)
        hn = hc * lax.rsqrt(var + LN_EPS) * ng_ref[:, hs]
        og = _sigmoid(o_ref[0, :, hs])
        y_ref[0, :, hs] = (og.astype(F32) * hn).astype(BF16)


def _mlstm(p3, rows, cols, conv_w, conv_b, wq, wk, norm_g):
    bsz, t, _ = p3.shape
    L = min(MLSTM_CHUNK, t)
    colblk = lambda c: pl.BlockSpec((1, L, 1024), lambda b, i, c=c: (b, i, c))
    full = lambda shape: pl.BlockSpec(shape, lambda b, i: tuple(0 for _ in shape))
    return pl.pallas_call(
        functools.partial(_mlstm_kernel, chunk=L),
        grid=(bsz, t // L),
        in_specs=[colblk(COL_CA), colblk(COL_VA), colblk(COL_OA),
                  pl.BlockSpec((1, 8, L), lambda b, i: (b, 0, i)),
                  pl.BlockSpec((1, L, 128), lambda b, i: (b, i, 0)),
                  full((CONV_K, 1024)), full((1, 1024)), full((H_A, DH_A, DH_A)), full((H_A, DH_A, DH_A)),
                  full((1, 1024))],
        out_specs=pl.BlockSpec((1, L, 1024), lambda b, i: (b, i, 0)),
        out_shape=jax.ShapeDtypeStruct((bsz, t, 1024), BF16),
        scratch_shapes=[pltpu.VMEM((8 + L, 1024), F32), pltpu.VMEM((H_A, DH_A, DH_A + 128), F32),
                        pltpu.VMEM((H_A, 8, 128), F32)],
        compiler_params=_cparams(("parallel", "arbitrary")), name="mlstm",
    )(p3, p3, p3, rows, cols, conv_w, conv_b.reshape(1, -1), wq, wk, norm_g.reshape(1, -1))


def _flash_kernel(*refs, tq, tk, dk, heads, has_bias):
    if has_bias:
        q_ref, k_ref, kb_ref, v_ref, o_ref, m_sc, l_sc, acc_sc = refs
    else:
        q_ref, k_ref, v_ref, o_ref, m_sc, l_sc, acc_sc = refs
        kb_ref = None
    i = pl.program_id(2)
    qs = []
    for g in range(heads):
        q = q_ref[0, :, g * dk:(g + 1) * dk]
        if has_bias:
            lane = lax.broadcasted_iota(jnp.int32, (tq, 128), 1)
            q = jnp.concatenate([q, jnp.where(lane < 3, -1.0, 0.0).astype(BF16)], axis=1)
        qs.append(q)

    def scores(g, key0, nk, r0, nr, mask_off):
        ks = pl.ds(key0, nk)
        kj = k_ref[0, ks, g * dk:(g + 1) * dk]
        if has_bias:
            kj = jnp.concatenate([kj, kb_ref[0, ks, g * 128:(g + 1) * 128]], axis=1)
        s = lax.dot_general(qs[g][r0:r0 + nr], kj, NT_DIMS, preferred_element_type=F32)
        if mask_off is not None:
            r_i = r0 + lax.broadcasted_iota(jnp.int32, (nr, nk), 0)
            c_i = mask_off + lax.broadcasted_iota(jnp.int32, (nr, nk), 1)
            s = jnp.where(c_i <= r_i, s, -jnp.inf)
        return s

    def update(g, s, key0, nk, r0, nr, first):
        rs = slice(r0, r0 + nr)
        vj = v_ref[0, pl.ds(key0, nk), g * 128:(g + 1) * 128]
        s_max = jnp.max(s, axis=1, keepdims=True)
        if first:
            m_new = jnp.broadcast_to(s_max, (nr, 128))
            l_part = jnp.zeros((nr, 128), F32)
        else:
            m_prev = m_sc[g, rs]
            m_new = jnp.maximum(m_prev, s_max)
            alpha = jnp.exp2(m_prev - m_new)
            l_part = alpha * l_sc[g, rs]
        ps = []
        for c in range(nk // 128):
            p_c = jnp.exp2(s[:, c * 128:(c + 1) * 128] - m_new)
            l_part = l_part + p_c
            ps.append(p_c.astype(BF16))
        pv = jnp.dot(jnp.concatenate(ps, axis=1), vj, preferred_element_type=F32)
        acc_sc[g, rs] = pv if first else alpha * acc_sc[g, rs] + pv
        l_sc[g, rs] = l_part
        m_sc[g, rs] = m_new

    def sweep(key0, nk, r0, nr, mask_off, first=False):
        s_cur = scores(0, key0, nk, r0, nr, mask_off)
        for g in range(heads):
            s_next = scores(g + 1, key0, nk, r0, nr, mask_off) if g + 1 < heads else None
            update(g, s_cur, key0, nk, r0, nr, first)
            s_cur = s_next

    nsub = tq // tk

    def body(j, carry):
        sweep(pl.multiple_of(j * tk, tk), tk, 0, tq, None)
        return carry

    for d in range(nsub):
        sweep(pl.multiple_of(i * tq + d * tk, tk), tk, d * tk, tq - d * tk, d * tk, first=(d == 0))
    lax.fori_loop(0, i * nsub, body, 0)
    for g in range(heads):
        l = jnp.sum(l_sc[g], axis=1, keepdims=True)
        o_ref[0, :, g * 128:(g + 1) * 128] = (acc_sc[g] / l).astype(o_ref.dtype)


def _flash(q_arr, k_arr, kb_arr, v_arr, *, n_heads, dk, q_blk, k_blk, v_blk):
    bsz, t, _ = q_arr.shape
    tq = min(FLASH_TQ, t)
    tk = min(FLASH_TK, t)
    g = FLASH_HEADS_PER_STEP
    hp = n_heads // g
    in_specs = [pl.BlockSpec((1, tq, g * dk), lambda b, h, i: (b, i, q_blk // g + h)),
                pl.BlockSpec((1, t, g * dk), lambda b, h, i: (b, 0, k_blk // g + h))]
    args = [q_arr, k_arr]
    if kb_arr is not None:
        in_specs.append(pl.BlockSpec((1, t, g * 128), lambda b, h, i: (b, 0, h)))
        args.append(kb_arr)
    in_specs.append(pl.BlockSpec((1, t, g * 128), lambda b, h, i: (b, 0, v_blk // g + h)))
    args.append(v_arr)
    return pl.pallas_call(
        functools.partial(_flash_kernel, tq=tq, tk=tk, dk=dk, heads=g, has_bias=kb_arr is not None),
        grid=(bsz, hp, t // tq),
        in_specs=in_specs,
        out_specs=pl.BlockSpec((1, tq, g * 128), lambda b, h, i: (b, i, h)),
        out_shape=jax.ShapeDtypeStruct((bsz, t, n_heads * 128), BF16),
        scratch_shapes=[pltpu.VMEM((g, tq, 128), F32), pltpu.VMEM((g, tq, 128), F32), pltpu.VMEM((g, tq, 128), F32)],
        compiler_params=_cparams(("parallel", "parallel", "arbitrary")),
        name="flash_fox" if kb_arr is not None else "flash_mla",
    )(*args)


def _mla_prep_kernel(c_ref, tab_ref, gq_ref, gkv_ref, wq_ref, wkv_ref, q_out, k_out, v_out):
    c = c_ref[...].astype(F32)
    ckv = c[:, 0:KV_RANK]
    cq = c[:, KV_RANK:KV_RANK + Q_RANK]
    kr = c[:, KV_RANK + Q_RANK:]

    def rms(x, g):
        ms = jnp.mean(x * x, axis=-1, keepdims=True)
        return (x * lax.rsqrt(ms + RMS_EPS) * g).astype(BF16)

    qf = jnp.dot(rms(cq, gq_ref[...]), wq_ref[...], preferred_element_type=F32)
    kvf = jnp.dot(rms(ckv, gkv_ref[...]), wkv_ref[...], preferred_element_type=F32)
    tab = tab_ref[...]

    def rope(xx):
        pr = xx * tab
        return pr + pltpu.roll(pr, DH_ROPE, 1)

    lane = lax.broadcasted_iota(jnp.int32, kr.shape, 1)
    k_rope = jnp.where(lane < DH_ROPE, rope(kr), 0.0).astype(BF16)
    for h in range(H_C):
        lo = h * DK_C
        q_out[:, lo:lo + 128] = qf[:, lo:lo + 128].astype(BF16)
        q_out[:, lo + 128:lo + 256] = rope(qf[:, lo + 128:lo + 256]).astype(BF16)
        k_out[:, lo:lo + 128] = kvf[:, lo:lo + 128].astype(BF16)
        k_out[:, lo + 128:lo + 256] = k_rope
        v_out[:, h * DH_V:(h + 1) * DH_V] = kvf[:, lo + 128:lo + 256].astype(BF16)


def _mla_prep(p2, tab, g_cq, g_ckv, wq, wkv):
    n = p2.shape[0]
    tm = min(512, n)
    full = lambda shape: pl.BlockSpec(shape, lambda i: tuple(0 for _ in shape))
    return pl.pallas_call(
        _mla_prep_kernel, grid=(n // tm,),
        in_specs=[pl.BlockSpec((tm, LOWRANK_W), lambda i: (i, LOWRANK_OFF // LOWRANK_W)),
                  pl.BlockSpec((tm, 128), lambda i: (i, 0)),
                  full((1, Q_RANK)), full((1, KV_RANK)), full((Q_RANK, H_C * DK_C)), full((KV_RANK, H_C * DK_C))],
        out_specs=[pl.BlockSpec((tm, H_C * DK_C), lambda i: (i, 0)),
                   pl.BlockSpec((tm, H_C * DK_C), lambda i: (i, 0)),
                   pl.BlockSpec((tm, H_C * DH_V), lambda i: (i, 0))],
        out_shape=[jax.ShapeDtypeStruct((n, H_C * DK_C), BF16), jax.ShapeDtypeStruct((n, H_C * DK_C), BF16),
                   jax.ShapeDtypeStruct((n, H_C * DH_V), BF16)],
        compiler_params=_cparams(("parallel",)), name="mla_prep",
    )(p2, tab, g_cq.reshape(1, -1), g_ckv.reshape(1, -1), wq, wkv)


def _merge_kernel(ya_ref, yb_ref, yc_ref, za_ref, zb_ref, zc_ref, ga_ref, gb_ref, gc_ref, h_ref,
                  pa_ref, pb_ref, pc_ref, wo_ref, lg_ref, lb_ref, hf_ref, hb_ref, *, alpha):
    def branch(y_ref, z_ref, g_ref, p_ref):
        hz = z_ref[...] * 0.5
        y = y_ref[...] * (hz + hz * jnp.tanh(hz))
        gate = 0.5 + 0.5 * jnp.tanh(g_ref[...] * 0.5)
        return gate.astype(F32) * jnp.dot(y, p_ref[...], preferred_element_type=F32)

    merged = (branch(ya_ref, za_ref, ga_ref, pa_ref) + branch(yb_ref, zb_ref, gb_ref, pb_ref)
              + branch(yc_ref, zc_ref, gc_ref, pc_ref))
    out = jnp.dot(merged.astype(BF16), wo_ref[...], preferred_element_type=F32)
    r = alpha * h_ref[...] + out
    mu = jnp.mean(r, axis=-1, keepdims=True)
    rc = r - mu
    var = jnp.mean(rc * rc, axis=-1, keepdims=True)
    y = rc * lax.rsqrt(var + LN_EPS) * lg_ref[...] + lb_ref[...]
    hf_ref[...] = y
    hb_ref[...] = y.astype(BF16)


def _merge(ya, yb, yc, p2, hf, pa, pb, pc, wo, ln_g, ln_b, alpha):
    n = hf.shape[0]
    tm = min(512, n)
    row = pl.BlockSpec((tm, D_MODEL), lambda i: (i, 0))
    colblk = lambda c: pl.BlockSpec((tm, 1024), lambda i, c=c: (i, c))
    wfull = pl.BlockSpec((1024, D_MODEL), lambda i: (0, 0), pipeline_mode=pl.Buffered(1))
    vec = pl.BlockSpec((1, D_MODEL), lambda i: (0, 0))
    return pl.pallas_call(
        functools.partial(_merge_kernel, alpha=alpha), grid=(n // tm,),
        in_specs=[row, row, row, colblk(COL_ZA), colblk(COL_ZB), colblk(COL_ZC),
                  colblk(COL_GA), colblk(COL_GB), colblk(COL_GC), row,
                  wfull, wfull, wfull, wfull, vec, vec],
        out_specs=[row, row],
        out_shape=[jax.ShapeDtypeStruct((n, D_MODEL), F32), jax.ShapeDtypeStruct((n, D_MODEL), BF16)],
        compiler_params=_cparams(("parallel",)), name="merge",
    )(ya, yb, yc, p2, p2, p2, p2, p2, p2, hf, pa, pb, pc, wo, ln_g.reshape(1, -1), ln_b.reshape(1, -1))


def _rot_cols(w):
    half = w.shape[-1] // 2
    return jnp.concatenate([-w[..., half:], w[..., :half]], axis=-1)


def _layer_weights(w_in, b_ia, b_fa, b_fb, w_uq, w_ukv):
    idx = []
    acc = 0
    for s in IN_SPLITS[:-1]:
        acc += s
        idx.append(acc)
    (w_ca, w_va, w_oa, w_ia, w_fa, w_za, w_qb, w_kb, w_vb, w_fb, w_zb,
     w_cq, w_ckv, w_kr, w_zc, w_ga, w_gb, w_gc) = jnp.split(w_in, idx, axis=-1)
    w_qb = w_qb * (DH_B ** -0.5 * LOG2E)
    w_big = jnp.concatenate([w_ca, w_va, w_oa, w_za, w_qb, w_kb, w_vb, w_zb, w_zc, w_ga, w_gb, w_gc,
                             w_ckv, w_cq, w_kr, _rot_cols(w_kr)], axis=-1).astype(BF16)
    zpad = lambda k: jnp.zeros((D_MODEL, k), F32)
    w_f = jnp.concatenate([w_fa, w_fa, w_fb, zpad(112)], axis=-1)
    w_i = jnp.concatenate([zpad(4), w_ia, zpad(120)], axis=-1)
    b_f = jnp.concatenate([b_fa, b_fa, b_fb, jnp.zeros((112,), F32)])
    b_i = jnp.concatenate([jnp.zeros((4,), F32), b_ia, jnp.zeros((120,), F32)])
    gw = (jnp.concatenate([w_f, w_i], axis=1).astype(BF16),
          jnp.concatenate([w_f[:, :16], w_i[:, :16]], axis=1).T.astype(BF16),
          b_f[:16].reshape(16, 1), b_i[:16].reshape(16, 1), b_f.reshape(1, 128), b_i.reshape(1, 128))
    wq3 = w_uq.reshape(Q_RANK, H_C, DH_NOPE + DH_ROPE) * ((DH_NOPE + DH_ROPE) ** -0.5 * LOG2E)
    wq_rope = wq3[..., DH_NOPE:]
    wq = jnp.concatenate([wq3, _rot_cols(wq_rope)], axis=-1).reshape(Q_RANK, H_C * DK_C).astype(BF16)
    return w_big, gw, wq, w_ukv.astype(BF16)


def kernel(x, positions, ln_in_g, ln_in_b, w_in, conv_w, conv_b, w_qa, w_ka, b_ia, b_fa, norm_ga, b_fb,
           g_cq, w_uq, g_ckv, w_ukv, p_a, p_b, p_c, w_out, ln_g, ln_b):
    bsz, t, _ = x.shape
    n = bsz * t
    depth = w_in.shape[0]
    alpha = (2.0 * depth) ** 0.25

    tab = _rope_table(positions)
    hf, hb = _input_ln(x.reshape(n, D_MODEL), ln_in_g, ln_in_b)
    for l in range(depth):
        w_big, gw, wq, wkv = _layer_weights(w_in[l], b_ia[l], b_fa[l], b_fb[l], w_uq[l], w_ukv[l])
        p2 = _in_proj(hb, w_big)
        p3 = p2.reshape(bsz, t, N_BIG)
        rows, cols, kb = _gates(hb.reshape(bsz, t, D_MODEL), gw)
        ya = _mlstm(p3, rows, cols, conv_w[l], conv_b[l], w_qa[l].astype(BF16), w_ka[l].astype(BF16), norm_ga[l])
        yb = _flash(p3, p3, kb, p3, n_heads=H_B, dk=DH_B, q_blk=COL_QB * 8, k_blk=COL_KB * 8, v_blk=COL_VB * 8)
        qc, kc, vc = _mla_prep(p2, tab, g_cq[l], g_ckv[l], wq, wkv)
        yc = _flash(qc.reshape(bsz, t, -1), kc.reshape(bsz, t, -1), None, vc.reshape(bsz, t, -1),
                    n_heads=H_C, dk=DK_C, q_blk=0, k_blk=0, v_blk=0)
        hf, hb = _merge(ya.reshape(n, -1), yb.reshape(n, -1), yc.reshape(n, -1), p2, hf,
                        p_a[l].astype(BF16), p_b[l].astype(BF16), p_c[l].astype(BF16), w_out[l].astype(BF16),
                        ln_g[l], ln_b[l], alpha)
    return hf.reshape(bsz, t, D_MODEL)
```

```python
import functools
import math

import jax
import jax.numpy as jnp
from jax import lax
from jax.experimental import pallas as pl
from jax.experimental.pallas import tpu as pltpu

F32 = jnp.float32
BF16 = jnp.bfloat16

D_MODEL = 1024
H_A, DH_A, CONV_K = 4, 256, 4
H_B, DH_B = 8, 128
H_C, DH_NOPE, DH_ROPE, DH_V = 8, 128, 64, 128
Q_RANK, KV_RANK = 384, 256
ROPE_BASE = 10000.0
LN_EPS = 1e-5
RMS_EPS = 1e-6
IN_SPLITS = (1024, 1024, 1024, 4, 4, 1024,
             1024, 1024, 1024, 8, 1024,
             Q_RANK, KV_RANK, DH_ROPE, 1024,
             1024, 1024, 1024)

COL_CA, COL_VA, COL_OA, COL_ZA, COL_QB, COL_KB, COL_VB, COL_ZB, COL_ZC, COL_GA, COL_GB, COL_GC = range(12)
LOWRANK_OFF = 12 * 1024
LOWRANK_W = 768
N_BIG = LOWRANK_OFF + LOWRANK_W
DK_C = 256

MLSTM_CHUNK = 256
FLASH_TQ = 1024
FLASH_TK = 512
FLASH_HEADS_PER_STEP = 4
MERGE_STRIPS = 2
VMEM_LIMIT = 56 * 1024 * 1024
LOG2E = math.log2(math.e)

NT_DIMS = (((1,), (1,)), ((), ()))
TN_DIMS = (((0,), (0,)), ((), ()))


def _cparams(sem):
    return pltpu.CompilerParams(dimension_semantics=sem, vmem_limit_bytes=VMEM_LIMIT)


def _sigmoid(x):
    return 0.5 + 0.5 * jnp.tanh(0.5 * x)


def _silu(x):
    hx = 0.5 * x
    return hx + hx * jnp.tanh(hx)


def _log_sigmoid(x):
    return jnp.minimum(x, 0.0) - jnp.log1p(jnp.exp(-jnp.abs(x)))


def _split3(x):
    hi = x.astype(BF16)
    r = x - hi.astype(F32)
    mid = r.astype(BF16)
    lo = (r - mid.astype(F32)).astype(BF16)
    return hi, mid, lo


def _ln_kernel(x_ref, g_ref, b_ref, hf_ref, hb_ref):
    x = x_ref[...]
    mu = jnp.mean(x, axis=-1, keepdims=True)
    xc = x - mu
    var = jnp.mean(xc * xc, axis=-1, keepdims=True)
    y = xc * lax.rsqrt(var + LN_EPS) * g_ref[...] + b_ref[...]
    hf_ref[...] = y
    hb_ref[...] = y.astype(BF16)


def _input_ln(x2, g, b):
    n = x2.shape[0]
    tm = min(512, n)
    row = pl.BlockSpec((tm, D_MODEL), lambda i: (i, 0))
    vec = pl.BlockSpec((1, D_MODEL), lambda i: (0, 0))
    return pl.pallas_call(
        _ln_kernel, grid=(n // tm,), in_specs=[row, vec, vec], out_specs=[row, row],
        out_shape=[jax.ShapeDtypeStruct((n, D_MODEL), F32), jax.ShapeDtypeStruct((n, D_MODEL), BF16)],
        compiler_params=_cparams(("parallel",)), name="input_ln",
    )(x2, g.reshape(1, -1), b.reshape(1, -1))


def _rope_kernel(pos_ref, freq_ref, tab_ref):
    ang = pos_ref[...].astype(F32) * freq_ref[...]
    lane = lax.broadcasted_iota(jnp.int32, ang.shape, 1)
    tab_ref[...] = jnp.where(lane < DH_ROPE, jnp.cos(ang), jnp.sin(ang))


def _rope_table(positions):
    n = positions.size
    tm = min(1024, n)
    inv_freq = ROPE_BASE ** (-jnp.arange(0, DH_ROPE, 2, dtype=F32) / DH_ROPE)
    freq = jnp.tile(inv_freq, 4).reshape(1, 128)
    return pl.pallas_call(
        _rope_kernel, grid=(n // tm,),
        in_specs=[pl.BlockSpec((tm, 1), lambda i: (i, 0)), pl.BlockSpec((1, 128), lambda i: (0, 0))],
        out_specs=pl.BlockSpec((tm, 128), lambda i: (i, 0)),
        out_shape=jax.ShapeDtypeStruct((n, 128), F32),
        compiler_params=_cparams(("parallel",)), name="rope_table",
    )(positions.reshape(n, 1), freq)


def _matmul_kernel(x_ref, w_ref, o_ref):
    o_ref[...] = jnp.dot(x_ref[...], w_ref[...], preferred_element_type=F32).astype(o_ref.dtype)


def _in_proj(hb, w_big):
    n = hb.shape[0]
    tm = min(1024, n)
    tn = N_BIG // 3
    return pl.pallas_call(
        _matmul_kernel, grid=(n // tm, N_BIG // tn),
        in_specs=[pl.BlockSpec((tm, D_MODEL), lambda i, j: (i, 0)),
                  pl.BlockSpec((D_MODEL, tn), lambda i, j: (0, j))],
        out_specs=pl.BlockSpec((tm, tn), lambda i, j: (i, j)),
        out_shape=jax.ShapeDtypeStruct((n, N_BIG), BF16),
        compiler_params=_cparams(("parallel", "arbitrary")), name="in_proj",
    )(hb, w_big)


def _gates_kernel(h_ref, wfi_ref, wfit_ref, bfc_ref, bic_ref, bfr_ref, bir_ref,
                  rows_ref, cols_ref, kb_ref, carry_ref, *, tg, chunk):
    t = pl.program_id(1)

    @pl.when(t == 0)
    def _():
        carry_ref[...] = jnp.zeros_like(carry_ref)

    hb = h_ref[0]
    r_i = lax.broadcasted_iota(jnp.int32, (tg, tg), 0)
    c_i = lax.broadcasted_iota(jnp.int32, (tg, tg), 1)
    same_chunk = (r_i // chunk) == (c_i // chunk)
    ind = lambda mask: jnp.where(mask, 1.0, 0.0).astype(BF16)
    upper_bd = ind((r_i <= c_i) & same_chunk)
    lower = ind(r_i >= c_i)
    lower_bd = ind((r_i >= c_i) & same_chunk)

    g_rows = lax.dot_general(wfit_ref[...], hb, NT_DIMS, preferred_element_type=F32)
    gf = g_rows[0:16] + bfc_ref[...]
    gi = g_rows[16:32] + bic_ref[...]
    y3 = jnp.dot(jnp.concatenate(_split3(_log_sigmoid(gf)), axis=0), upper_bd, preferred_element_type=F32)
    y_bd = y3[0:16] + y3[16:32] + y3[32:48]
    row = lax.broadcasted_iota(jnp.int32, (16, tg), 0)
    rows_ref[0] = jnp.where(row < 4, y_bd, gi - y_bd)[0:8]

    g_cols = jnp.dot(hb, wfi_ref[...], preferred_element_type=F32)
    gfc = g_cols[:, 0:128] + bfr_ref[...]
    gic = g_cols[:, 128:256] + bir_ref[...]
    lf_parts = jnp.concatenate(_split3(_log_sigmoid(gfc)), axis=1)
    sum3 = lambda a: a[:, 0:128] + a[:, 128:256] + a[:, 256:384]
    yc = sum3(jnp.dot(lower_bd, lf_parts, preferred_element_type=F32))
    y_full = sum3(jnp.dot(lower, lf_parts, preferred_element_type=F32))
    col = lax.broadcasted_iota(jnp.int32, (tg, 128), 1)
    cols_ref[0] = jnp.where(col < 4, yc, gic - yc)

    cum = y_full + carry_ref[0:1, :]
    carry_ref[...] = jnp.broadcast_to(cum[tg - 1:tg, :], carry_ref.shape)
    parts = jnp.concatenate(_split3(cum * LOG2E), axis=1)
    sr = lax.broadcasted_iota(jnp.int32, (3 * 128, 128), 0)
    sc = lax.broadcasted_iota(jnp.int32, (3 * 128, 128), 1)
    scatter = ind((sr % 128 == 8 + sc // 3) & (sr // 128 == sc % 3) & (sc < 3 * H_B))
    kb_ref[0] = jnp.dot(parts, scatter, preferred_element_type=F32).astype(BF16)


def _gates(hb3, gw):
    bsz, t, _ = hb3.shape
    tg = min(512, t)
    full = lambda shape: pl.BlockSpec(shape, lambda b, i: tuple(0 for _ in shape))
    return pl.pallas_call(
        functools.partial(_gates_kernel, tg=tg, chunk=min(MLSTM_CHUNK, t)),
        grid=(bsz, t // tg),
        in_specs=[pl.BlockSpec((1, tg, D_MODEL), lambda b, i: (b, i, 0)),
                  full((D_MODEL, 256)), full((32, D_MODEL)),
                  full((16, 1)), full((16, 1)), full((1, 128)), full((1, 128))],
        out_specs=[pl.BlockSpec((1, 8, tg), lambda b, i: (b, 0, i)),
                   pl.BlockSpec((1, tg, 128), lambda b, i: (b, i, 0)),
                   pl.BlockSpec((1, tg, 128), lambda b, i: (b, i, 0))],
        out_shape=[jax.ShapeDtypeStruct((bsz, 8, t), F32), jax.ShapeDtypeStruct((bsz, t, 128), F32),
                   jax.ShapeDtypeStruct((bsz, t, 128), BF16)],
        scratch_shapes=[pltpu.VMEM((8, 128), F32)],
        compiler_params=_cparams(("parallel", "arbitrary")), name="gates",
    )(hb3, *gw)


def _mlstm_kernel(c_ref, v_ref, o_ref, rows_ref, cols_ref, cw_ref, cb_ref, wq_ref, wk_ref, ng_ref,
                  y_ref, cbuf, c_state, m_state, *, chunk):
    L = chunk
    t = pl.program_id(1)

    @pl.when(t == 0)
    def _():
        cbuf[0:8, :] = jnp.zeros((8, D_MODEL), F32)
        c_state[...] = jnp.zeros_like(c_state)
        m_state[...] = jnp.zeros_like(m_state)

    x = c_ref[0].astype(F32)
    cbuf[8:8 + L, :] = x
    conv = cb_ref[...]
    for j in range(CONV_K):
        conv = conv + cbuf[8 - (CONV_K - 1) + j:8 - (CONV_K - 1) + j + L, :] * cw_ref[j:j + 1, :]
    cbuf[0:8, :] = x[L - 8:L, :]
    ub = _silu(conv).astype(BF16)

    nc = L // 128
    lanes = lambda c: slice(c * 128, (c + 1) * 128)
    r_i = lax.broadcasted_iota(jnp.int32, (L, 128), 0)
    c_i = lax.broadcasted_iota(jnp.int32, (L, 128), 1)
    causal = [c_i + c * 128 <= r_i for c in range(nc)]
    ones_blk = jnp.ones((L, 128), BF16)
    for h in range(H_A):
        hs = slice(h * DH_A, (h + 1) * DH_A)
        uh = ub[:, hs]
        q = jnp.dot(uh, wq_ref[h], preferred_element_type=F32)
        k = jnp.dot(uh, wk_ref[h], preferred_element_type=F32) * (DH_A ** -0.5)
        qb = q.astype(BF16)
        kb = k.astype(BF16)
        v_aug = jnp.concatenate([v_ref[0, :, hs], ones_blk], axis=1)
        b_rep = jnp.broadcast_to(cols_ref[0, :, h:h + 1], (L, 128))
        c_rep = jnp.broadcast_to(cols_ref[0, :, H_A + h:H_A + h + 1], (L, 128))
        b_row = rows_ref[0, h:h + 1, :]
        c_row = rows_ref[0, H_A + h:H_A + h + 1, :]
        m_prev = m_state[h, 0:1, :]

        dm = [jnp.where(causal[c], b_rep + c_row[:, lanes(c)], -jnp.inf) for c in range(nc)]
        dmax = dm[0]
        for c in range(1, nc):
            dmax = jnp.maximum(dmax, dm[c])
        inter = b_rep + m_prev
        m_t = jnp.maximum(jnp.max(dmax, axis=1, keepdims=True), inter)
        qk = lax.dot_general(qb, kb, NT_DIMS, preferred_element_type=F32)
        s = jnp.concatenate([(qk[:, lanes(c)] * jnp.exp(dm[c] - m_t)).astype(BF16) for c in range(nc)], axis=1)
        decay = jnp.exp(inter - m_t)
        c_old = c_state[h]
        inter_acc = jnp.dot(qb, c_old.astype(BF16), preferred_element_type=F32)
        acc = jnp.dot(s, v_aug, preferred_element_type=F32) + jnp.tile(decay, (1, DH_A // 128 + 1)) * inter_acc
        den = acc[:, DH_A:]
        inv = 1.0 / jnp.maximum(jnp.abs(den), jnp.exp(-m_t))
        hh = acc[:, :DH_A] * jnp.tile(inv, (1, DH_A // 128))

        g = b_row[:, L - 1:L]
        a_rep = g + c_rep
        m_new = jnp.maximum(g + m_prev, jnp.max(a_rep, axis=0, keepdims=True))
        kw = (k * jnp.tile(jnp.exp(a_rep - m_new), (1, DH_A // 128))).astype(BF16)
        sc = jnp.exp(g + m_prev - m_new)
        c_state[h] = (jnp.tile(sc, (1, DH_A // 128 + 1)) * c_old
                      + lax.dot_general(kw, v_aug, TN_DIMS, preferred_element_type=F32))
        m_state[h] = jnp.broadcast_to(m_new, m_state.shape[1:])

        mu = jnp.mean(hh, axis=1, keepdims=True)
        hc = hh - mu
        var = jnp.mean(hc * hc, axis=1, keepdims=True)
        hn = hc * lax.rsqrt(var + LN_EPS) * ng_ref[:, hs]
        og = _sigmoid(o_ref[0, :, hs])
        y_ref[0, :, hs] = (og.astype(F32) * hn).astype(BF16)


def _mlstm(p3, rows, cols, conv_w, conv_b, wq, wk, norm_g):
    bsz, t, _ = p3.shape
    L = min(MLSTM_CHUNK, t)
    colblk = lambda c: pl.BlockSpec((1, L, 1024), lambda b, i, c=c: (b, i, c))
    full = lambda shape: pl.BlockSpec(shape, lambda b, i: tuple(0 for _ in shape))
    return pl.pallas_call(
        functools.partial(_mlstm_kernel, chunk=L),
        grid=(bsz, t // L),
        in_specs=[colblk(COL_CA), colblk(COL_VA), colblk(COL_OA),
                  pl.BlockSpec((1, 8, L), lambda b, i: (b, 0, i)),
                  pl.BlockSpec((1, L, 128), lambda b, i: (b, i, 0)),
                  full((CONV_K, 1024)), full((1, 1024)), full((H_A, DH_A, DH_A)), full((H_A, DH_A, DH_A)),
                  full((1, 1024))],
        out_specs=pl.BlockSpec((1, L, 1024), lambda b, i: (b, i, 0)),
        out_shape=jax.ShapeDtypeStruct((bsz, t, 1024), BF16),
        scratch_shapes=[pltpu.VMEM((8 + L, 1024), F32), pltpu.VMEM((H_A, DH_A, DH_A + 128), F32),
                        pltpu.VMEM((H_A, 8, 128), F32)],
        compiler_params=_cparams(("parallel", "arbitrary")), name="mlstm",
    )(p3, p3, p3, rows, cols, conv_w, conv_b.reshape(1, -1), wq, wk, norm_g.reshape(1, -1))


def _flash_kernel(*refs, tq, tk, dk, heads, has_bias):
    if has_bias:
        q_ref, k_ref, kb_ref, v_ref, o_ref, m_sc, l_sc, acc_sc = refs
    else:
        q_ref, k_ref, v_ref, o_ref, m_sc, l_sc, acc_sc = refs
        kb_ref = None
    i = pl.program_id(2)
    qs = []
    for g in range(heads):
        q = q_ref[0, :, g * dk:(g + 1) * dk]
        if has_bias:
            lane = lax.broadcasted_iota(jnp.int32, (tq, 128), 1) - 3 * (pl.program_id(1) * heads + g)
            q = jnp.concatenate([q, jnp.where((lane >= 0) & (lane < 3), -1.0, 0.0).astype(BF16)], axis=1)
        qs.append(q)

    def scores(g, key0, nk, r0, nr, mask_off):
        ks = pl.ds(key0, nk)
        kj = k_ref[0, ks, g * dk:(g + 1) * dk]
        if has_bias:
            kj = jnp.concatenate([kj, kb_ref[0, ks, :]], axis=1)
        s = lax.dot_general(qs[g][r0:r0 + nr], kj, NT_DIMS, preferred_element_type=F32)
        if mask_off is not None:
            r_i = r0 + lax.broadcasted_iota(jnp.int32, (nr, nk), 0)
            c_i = mask_off + lax.broadcasted_iota(jnp.int32, (nr, nk), 1)
            s = jnp.where(c_i <= r_i, s, -jnp.inf)
        return s

    def update(g, s, key0, nk, r0, nr, first):
        rs = slice(r0, r0 + nr)
        vj = v_ref[0, pl.ds(key0, nk), g * 128:(g + 1) * 128]
        s_max = jnp.max(s, axis=1, keepdims=True)
        if first:
            m_new = jnp.broadcast_to(s_max, (nr, 128))
            l_part = jnp.zeros((nr, 128), F32)
        else:
            m_prev = m_sc[g, rs]
            m_new = jnp.maximum(m_prev, s_max)
            alpha = jnp.exp2(m_prev - m_new)
            l_part = alpha * l_sc[g, rs]
        ps = []
        for c in range(nk // 128):
            p_c = jnp.exp2(s[:, c * 128:(c + 1) * 128] - m_new)
            l_part = l_part + p_c
            ps.append(p_c.astype(BF16))
        pv = jnp.dot(jnp.concatenate(ps, axis=1), vj, preferred_element_type=F32)
        acc_sc[g, rs] = pv if first else alpha * acc_sc[g, rs] + pv
        l_sc[g, rs] = l_part
        m_sc[g, rs] = m_new

    def finish(g):
        l = jnp.sum(l_sc[g], axis=1, keepdims=True)
        o_ref[0, :, g * 128:(g + 1) * 128] = (acc_sc[g] / l).astype(o_ref.dtype)

    def sweep(key0, nk, r0, nr, mask_off, first=False, last=False):
        s_cur = scores(0, key0, nk, r0, nr, mask_off)
        for g in range(heads):
            s_next = scores(g + 1, key0, nk, r0, nr, mask_off) if g + 1 < heads else None
            update(g, s_cur, key0, nk, r0, nr, first)
            if last:
                finish(g)
            s_cur = s_next

    nsub = tq // tk

    def body(j, carry):
        sweep(pl.multiple_of(j * tk, tk), tk, 0, tq, None)
        return carry

    def diag(d, **kw):
        sweep(pl.multiple_of(i * tq + d * tk, tk), tk, d * tk, tq - d * tk, d * tk, **kw)

    diag(0, first=True)
    lax.fori_loop(0, i * nsub, body, 0)
    for d in range(1, nsub):
        diag(d, last=(d == nsub - 1))
    if nsub == 1:
        for g in range(heads):
            finish(g)


def _flash(q_arr, k_arr, kb_arr, v_arr, *, n_heads, dk, q_blk, k_blk, v_blk):
    bsz, t, _ = q_arr.shape
    tq = min(FLASH_TQ, t)
    tk = min(FLASH_TK, t)
    g = FLASH_HEADS_PER_STEP
    hp = n_heads // g
    in_specs = [pl.BlockSpec((1, tq, g * dk), lambda b, h, i: (b, i, q_blk // g + h)),
                pl.BlockSpec((1, t, g * dk), lambda b, h, i: (b, 0, k_blk // g + h))]
    args = [q_arr, k_arr]
    if kb_arr is not None:
        in_specs.append(pl.BlockSpec((1, t, 128), lambda b, h, i: (b, 0, 0)))
        args.append(kb_arr)
    in_specs.append(pl.BlockSpec((1, t, g * 128), lambda b, h, i: (b, 0, v_blk // g + h)))
    args.append(v_arr)
    return pl.pallas_call(
        functools.partial(_flash_kernel, tq=tq, tk=tk, dk=dk, heads=g, has_bias=kb_arr is not None),
        grid=(bsz, hp, t // tq),
        in_specs=in_specs,
        out_specs=pl.BlockSpec((1, tq, g * 128), lambda b, h, i: (b, i, h)),
        out_shape=jax.ShapeDtypeStruct((bsz, t, n_heads * 128), BF16),
        scratch_shapes=[pltpu.VMEM((g, tq, 128), F32), pltpu.VMEM((g, tq, 128), F32), pltpu.VMEM((g, tq, 128), F32)],
        compiler_params=_cparams(("parallel", "parallel", "arbitrary")),
        name="flash_fox" if kb_arr is not None else "flash_mla",
    )(*args)


def _mla_prep_kernel(c_ref, tab_ref, gq_ref, gkv_ref, wq_ref, wkv_ref, q_out, k_out, v_out):
    c = c_ref[...].astype(F32)
    ckv = c[:, 0:KV_RANK]
    cq = c[:, KV_RANK:KV_RANK + Q_RANK]
    kr = c[:, KV_RANK + Q_RANK:]

    def rms(x, g):
        ms = jnp.mean(x * x, axis=-1, keepdims=True)
        return (x * lax.rsqrt(ms + RMS_EPS) * g).astype(BF16)

    qf = jnp.dot(rms(cq, gq_ref[...]), wq_ref[...], preferred_element_type=F32)
    kvf = jnp.dot(rms(ckv, gkv_ref[...]), wkv_ref[...], preferred_element_type=F32)
    tab = tab_ref[...]

    def rope(xx):
        pr = xx * tab
        return pr + pltpu.roll(pr, DH_ROPE, 1)

    lane = lax.broadcasted_iota(jnp.int32, kr.shape, 1)
    k_rope = jnp.where(lane < DH_ROPE, rope(kr), 0.0).astype(BF16)
    for h in range(H_C):
        lo = h * DK_C
        q_out[:, lo:lo + 128] = qf[:, lo:lo + 128].astype(BF16)
        q_out[:, lo + 128:lo + 256] = rope(qf[:, lo + 128:lo + 256]).astype(BF16)
        k_out[:, lo:lo + 128] = kvf[:, lo:lo + 128].astype(BF16)
        k_out[:, lo + 128:lo + 256] = k_rope
        v_out[:, h * DH_V:(h + 1) * DH_V] = kvf[:, lo + 128:lo + 256].astype(BF16)


def _mla_prep(p2, tab, g_cq, g_ckv, wq, wkv):
    n = p2.shape[0]
    tm = min(512, n)
    full = lambda shape: pl.BlockSpec(shape, lambda i: tuple(0 for _ in shape))
    return pl.pallas_call(
        _mla_prep_kernel, grid=(n // tm,),
        in_specs=[pl.BlockSpec((tm, LOWRANK_W), lambda i: (i, LOWRANK_OFF // LOWRANK_W)),
                  pl.BlockSpec((tm, 128), lambda i: (i, 0)),
                  full((1, Q_RANK)), full((1, KV_RANK)), full((Q_RANK, H_C * DK_C)), full((KV_RANK, H_C * DK_C))],
        out_specs=[pl.BlockSpec((tm, H_C * DK_C), lambda i: (i, 0)),
                   pl.BlockSpec((tm, H_C * DK_C), lambda i: (i, 0)),
                   pl.BlockSpec((tm, H_C * DH_V), lambda i: (i, 0))],
        out_shape=[jax.ShapeDtypeStruct((n, H_C * DK_C), BF16), jax.ShapeDtypeStruct((n, H_C * DK_C), BF16),
                   jax.ShapeDtypeStruct((n, H_C * DH_V), BF16)],
        compiler_params=_cparams(("parallel",)), name="mla_prep",
    )(p2, tab, g_cq.reshape(1, -1), g_ckv.reshape(1, -1), wq, wkv)


def _merge_kernel(ya_ref, yb_ref, yc_ref, za_ref, zb_ref, zc_ref, ga_ref, gb_ref, gc_ref, h_ref,
                  pa_ref, pb_ref, pc_ref, wo_ref, lg_ref, lb_ref, hf_ref, hb_ref, *, alpha):
    def front(rs):
        def branch(y_ref, z_ref, g_ref, p_ref):
            hz = z_ref[rs, :] * 0.5
            y = y_ref[rs, :] * (hz + hz * jnp.tanh(hz))
            gate = 0.5 + 0.5 * jnp.tanh(g_ref[rs, :] * 0.5)
            return gate.astype(F32) * jnp.dot(y, p_ref[...], preferred_element_type=F32)

        return (branch(ya_ref, za_ref, ga_ref, pa_ref) + branch(yb_ref, zb_ref, gb_ref, pb_ref)
                + branch(yc_ref, zc_ref, gc_ref, pc_ref)).astype(BF16)

    def back(rs, merged):
        out = jnp.dot(merged, wo_ref[...], preferred_element_type=F32)
        r = alpha * h_ref[rs, :] + out
        mu = jnp.mean(r, axis=-1, keepdims=True)
        rc = r - mu
        var = jnp.mean(rc * rc, axis=-1, keepdims=True)
        y = rc * lax.rsqrt(var + LN_EPS) * lg_ref[...] + lb_ref[...]
        hf_ref[rs, :] = y
        hb_ref[rs, :] = y.astype(BF16)

    tm = h_ref.shape[0]
    strips = [slice(r, r + tm // MERGE_STRIPS) for r in range(0, tm, tm // MERGE_STRIPS)]
    cur = front(strips[0])
    for n, rs in enumerate(strips):
        nxt = front(strips[n + 1]) if n + 1 < len(strips) else None
        back(rs, cur)
        cur = nxt


def _merge(ya, yb, yc, p2, hf, pa, pb, pc, wo, ln_g, ln_b, alpha):
    n = hf.shape[0]
    tm = min(512, n)
    row = pl.BlockSpec((tm, D_MODEL), lambda i: (i, 0))
    colblk = lambda c: pl.BlockSpec((tm, 1024), lambda i, c=c: (i, c))
    wfull = pl.BlockSpec((1024, D_MODEL), lambda i: (0, 0), pipeline_mode=pl.Buffered(1))
    vec = pl.BlockSpec((1, D_MODEL), lambda i: (0, 0))
    return pl.pallas_call(
        functools.partial(_merge_kernel, alpha=alpha), grid=(n // tm,),
        in_specs=[row, row, row, colblk(COL_ZA), colblk(COL_ZB), colblk(COL_ZC),
                  colblk(COL_GA), colblk(COL_GB), colblk(COL_GC), row,
                  wfull, wfull, wfull, wfull, vec, vec],
        out_specs=[row, row],
        out_shape=[jax.ShapeDtypeStruct((n, D_MODEL), F32), jax.ShapeDtypeStruct((n, D_MODEL), BF16)],
        compiler_params=_cparams(("parallel",)), name="merge",
    )(ya, yb, yc, p2, p2, p2, p2, p2, p2, hf, pa, pb, pc, wo, ln_g.reshape(1, -1), ln_b.reshape(1, -1))


def _rot_cols(w):
    half = w.shape[-1] // 2
    return jnp.concatenate([-w[..., half:], w[..., :half]], axis=-1)


def _layer_weights(w_in, b_ia, b_fa, b_fb, w_uq, w_ukv):
    idx = []
    acc = 0
    for s in IN_SPLITS[:-1]:
        acc += s
        idx.append(acc)
    (w_ca, w_va, w_oa, w_ia, w_fa, w_za, w_qb, w_kb, w_vb, w_fb, w_zb,
     w_cq, w_ckv, w_kr, w_zc, w_ga, w_gb, w_gc) = jnp.split(w_in, idx, axis=-1)
    w_qb = w_qb * (DH_B ** -0.5 * LOG2E)
    w_big = jnp.concatenate([w_ca, w_va, w_oa, w_za, w_qb, w_kb, w_vb, w_zb, w_zc, w_ga, w_gb, w_gc,
                             w_ckv, w_cq, w_kr, _rot_cols(w_kr)], axis=-1).astype(BF16)
    zpad = lambda k: jnp.zeros((D_MODEL, k), F32)
    w_f = jnp.concatenate([w_fa, w_fa, w_fb, zpad(112)], axis=-1)
    w_i = jnp.concatenate([zpad(4), w_ia, zpad(120)], axis=-1)
    b_f = jnp.concatenate([b_fa, b_fa, b_fb, jnp.zeros((112,), F32)])
    b_i = jnp.concatenate([jnp.zeros((4,), F32), b_ia, jnp.zeros((120,), F32)])
    gw = (jnp.concatenate([w_f, w_i], axis=1).astype(BF16),
          jnp.concatenate([w_f[:, :16], w_i[:, :16]], axis=1).T.astype(BF16),
          b_f[:16].reshape(16, 1), b_i[:16].reshape(16, 1), b_f.reshape(1, 128), b_i.reshape(1, 128))
    wq3 = w_uq.reshape(Q_RANK, H_C, DH_NOPE + DH_ROPE) * ((DH_NOPE + DH_ROPE) ** -0.5 * LOG2E)
    wq_rope = wq3[..., DH_NOPE:]
    wq = jnp.concatenate([wq3, _rot_cols(wq_rope)], axis=-1).reshape(Q_RANK, H_C * DK_C).astype(BF16)
    return w_big, gw, wq, w_ukv.astype(BF16)


def kernel(x, positions, ln_in_g, ln_in_b, w_in, conv_w, conv_b, w_qa, w_ka, b_ia, b_fa, norm_ga, b_fb,
           g_cq, w_uq, g_ckv, w_ukv, p_a, p_b, p_c, w_out, ln_g, ln_b):
    bsz, t, _ = x.shape
    n = bsz * t
    depth = w_in.shape[0]
    alpha = (2.0 * depth) ** 0.25

    tab = _rope_table(positions)
    hf, hb = _input_ln(x.reshape(n, D_MODEL), ln_in_g, ln_in_b)
    for l in range(depth):
        w_big, gw, wq, wkv = _layer_weights(w_in[l], b_ia[l], b_fa[l], b_fb[l], w_uq[l], w_ukv[l])
        p2 = _in_proj(hb, w_big)
        p3 = p2.reshape(bsz, t, N_BIG)
        rows, cols, kb = _gates(hb.reshape(bsz, t, D_MODEL), gw)
        ya = _mlstm(p3, rows, cols, conv_w[l], conv_b[l], w_qa[l].astype(BF16), w_ka[l].astype(BF16), norm_ga[l])
        yb = _flash(p3, p3, kb, p3, n_heads=H_B, dk=DH_B, q_blk=COL_QB * 8, k_blk=COL_KB * 8, v_blk=COL_VB * 8)
        qc, kc, vc = _mla_prep(p2, tab, g_cq[l], g_ckv[l], wq, wkv)
        yc = _flash(qc.reshape(bsz, t, -1), kc.reshape(bsz, t, -1), None, vc.reshape(bsz, t, -1),
                    n_heads=H_C, dk=DK_C, q_blk=0, k_blk=0, v_blk=0)
        hf, hb = _merge(ya.reshape(n, -1), yb.reshape(n, -1), yc.reshape(n, -1), p2, hf,
                        p_a[l].astype(BF16), p_b[l].astype(BF16), p_c[l].astype(BF16), w_out[l].astype(BF16),
                        ln_g[l], ln_b[l], alpha)
    return hf.reshape(bsz, t, D_MODEL)
```

```python
import functools
import math

import jax
import jax.numpy as jnp
from jax import lax
from jax.experimental import pallas as pl
from jax.experimental.pallas import tpu as pltpu

F32 = jnp.float32
BF16 = jnp.bfloat16

D_MODEL = 1024
H_A, DH_A, CONV_K = 4, 256, 4
H_B, DH_B = 8, 128
H_C, DH_NOPE, DH_ROPE, DH_V = 8, 128, 64, 128
Q_RANK, KV_RANK = 384, 256
ROPE_BASE = 10000.0
LN_EPS = 1e-5
RMS_EPS = 1e-6
IN_SPLITS = (1024, 1024, 1024, 4, 4, 1024,
             1024, 1024, 1024, 8, 1024,
             Q_RANK, KV_RANK, DH_ROPE, 1024,
             1024, 1024, 1024)

COL_CA, COL_VA, COL_OA, COL_ZA, COL_QB, COL_KB, COL_VB, COL_ZB, COL_ZC, COL_GA, COL_GB, COL_GC = range(12)
LOWRANK_OFF = 12 * 1024
LOWRANK_W = 768
N_BIG = LOWRANK_OFF + LOWRANK_W
DK_C = 256

MLSTM_CHUNK = 256
MLSTM_CHUNKS_PER_STEP = 4
FLASH_TQ = 1024
FLASH_TK = 512
FLASH_HEADS_PER_STEP = 4
MERGE_STRIPS = 2
VMEM_LIMIT = 56 * 1024 * 1024
LOG2E = math.log2(math.e)

NT_DIMS = (((1,), (1,)), ((), ()))
TN_DIMS = (((0,), (0,)), ((), ()))


def _cparams(sem):
    return pltpu.CompilerParams(dimension_semantics=sem, vmem_limit_bytes=VMEM_LIMIT)


def _sigmoid(x):
    return 0.5 + 0.5 * jnp.tanh(0.5 * x)


def _silu(x):
    hx = 0.5 * x
    return hx + hx * jnp.tanh(hx)


def _log_sigmoid(x):
    return jnp.minimum(x, 0.0) - jnp.log1p(jnp.exp(-jnp.abs(x)))


def _split3(x):
    hi = x.astype(BF16)
    r = x - hi.astype(F32)
    mid = r.astype(BF16)
    lo = (r - mid.astype(F32)).astype(BF16)
    return hi, mid, lo


def _ln_kernel(x_ref, g_ref, b_ref, hf_ref, hb_ref):
    x = x_ref[...]
    mu = jnp.mean(x, axis=-1, keepdims=True)
    xc = x - mu
    var = jnp.mean(xc * xc, axis=-1, keepdims=True)
    y = xc * lax.rsqrt(var + LN_EPS) * g_ref[...] + b_ref[...]
    hf_ref[...] = y
    hb_ref[...] = y.astype(BF16)


def _input_ln(x2, g, b):
    n = x2.shape[0]
    tm = min(512, n)
    row = pl.BlockSpec((tm, D_MODEL), lambda i: (i, 0))
    vec = pl.BlockSpec((1, D_MODEL), lambda i: (0, 0))
    return pl.pallas_call(
        _ln_kernel, grid=(n // tm,), in_specs=[row, vec, vec], out_specs=[row, row],
        out_shape=[jax.ShapeDtypeStruct((n, D_MODEL), F32), jax.ShapeDtypeStruct((n, D_MODEL), BF16)],
        compiler_params=_cparams(("parallel",)), name="input_ln",
    )(x2, g.reshape(1, -1), b.reshape(1, -1))


def _rope_kernel(pos_ref, freq_ref, tab_ref):
    ang = pos_ref[...].astype(F32) * freq_ref[...]
    lane = lax.broadcasted_iota(jnp.int32, ang.shape, 1)
    tab_ref[...] = jnp.where(lane < DH_ROPE, jnp.cos(ang), jnp.sin(ang))


def _rope_table(positions):
    n = positions.size
    tm = min(1024, n)
    inv_freq = ROPE_BASE ** (-jnp.arange(0, DH_ROPE, 2, dtype=F32) / DH_ROPE)
    freq = jnp.tile(inv_freq, 4).reshape(1, 128)
    return pl.pallas_call(
        _rope_kernel, grid=(n // tm,),
        in_specs=[pl.BlockSpec((tm, 1), lambda i: (i, 0)), pl.BlockSpec((1, 128), lambda i: (0, 0))],
        out_specs=pl.BlockSpec((tm, 128), lambda i: (i, 0)),
        out_shape=jax.ShapeDtypeStruct((n, 128), F32),
        compiler_params=_cparams(("parallel",)), name="rope_table",
    )(positions.reshape(n, 1), freq)


def _matmul_kernel(x_ref, w_ref, o_ref):
    o_ref[...] = jnp.dot(x_ref[...], w_ref[...], preferred_element_type=F32).astype(o_ref.dtype)


def _in_proj(hb, w_big):
    n = hb.shape[0]
    tm = min(1024, n)
    tn = N_BIG // 3
    return pl.pallas_call(
        _matmul_kernel, grid=(n // tm, N_BIG // tn),
        in_specs=[pl.BlockSpec((tm, D_MODEL), lambda i, j: (i, 0)),
                  pl.BlockSpec((D_MODEL, tn), lambda i, j: (0, j))],
        out_specs=pl.BlockSpec((tm, tn), lambda i, j: (i, j)),
        out_shape=jax.ShapeDtypeStruct((n, N_BIG), BF16),
        compiler_params=_cparams(("parallel", "arbitrary")), name="in_proj",
    )(hb, w_big)


def _gates_kernel(h_ref, wfi_ref, wfit_ref, bfc_ref, bic_ref, bfr_ref, bir_ref,
                  rows_ref, cols_ref, kb_ref, carry_ref, *, tg, chunk):
    t = pl.program_id(1)

    @pl.when(t == 0)
    def _():
        carry_ref[...] = jnp.zeros_like(carry_ref)

    hb = h_ref[0]
    r_i = lax.broadcasted_iota(jnp.int32, (tg, tg), 0)
    c_i = lax.broadcasted_iota(jnp.int32, (tg, tg), 1)
    same_chunk = (r_i // chunk) == (c_i // chunk)
    ind = lambda mask: jnp.where(mask, 1.0, 0.0).astype(BF16)
    upper_bd = ind((r_i <= c_i) & same_chunk)
    lower = ind(r_i >= c_i)
    lower_bd = ind((r_i >= c_i) & same_chunk)

    g_rows = lax.dot_general(wfit_ref[...], hb, NT_DIMS, preferred_element_type=F32)
    gf = g_rows[0:16] + bfc_ref[...]
    gi = g_rows[16:32] + bic_ref[...]
    y3 = jnp.dot(jnp.concatenate(_split3(_log_sigmoid(gf)), axis=0), upper_bd, preferred_element_type=F32)
    y_bd = y3[0:16] + y3[16:32] + y3[32:48]
    row = lax.broadcasted_iota(jnp.int32, (16, tg), 0)
    rows_ref[0] = jnp.where(row < 4, y_bd, gi - y_bd)[0:8]

    g_cols = jnp.dot(hb, wfi_ref[...], preferred_element_type=F32)
    gfc = g_cols[:, 0:128] + bfr_ref[...]
    gic = g_cols[:, 128:256] + bir_ref[...]
    lf_parts = jnp.concatenate(_split3(_log_sigmoid(gfc)), axis=1)
    sum3 = lambda a: a[:, 0:128] + a[:, 128:256] + a[:, 256:384]
    yc = sum3(jnp.dot(lower_bd, lf_parts, preferred_element_type=F32))
    y_full = sum3(jnp.dot(lower, lf_parts, preferred_element_type=F32))
    col = lax.broadcasted_iota(jnp.int32, (tg, 128), 1)
    cols_ref[0] = jnp.where(col < 4, yc, gic - yc)

    cum = y_full + carry_ref[0:1, :]
    carry_ref[...] = jnp.broadcast_to(cum[tg - 1:tg, :], carry_ref.shape)
    parts = jnp.concatenate(_split3(cum * LOG2E), axis=1)
    sr = lax.broadcasted_iota(jnp.int32, (3 * 128, 128), 0)
    sc = lax.broadcasted_iota(jnp.int32, (3 * 128, 128), 1)
    scatter = ind((sr % 128 == 8 + sc // 3) & (sr // 128 == sc % 3) & (sc < 3 * H_B))
    kb_ref[0] = jnp.dot(parts, scatter, preferred_element_type=F32).astype(BF16)


def _gates(hb3, gw):
    bsz, t, _ = hb3.shape
    tg = min(512, t)
    full = lambda shape: pl.BlockSpec(shape, lambda b, i: tuple(0 for _ in shape))
    return pl.pallas_call(
        functools.partial(_gates_kernel, tg=tg, chunk=min(MLSTM_CHUNK, t)),
        grid=(bsz, t // tg),
        in_specs=[pl.BlockSpec((1, tg, D_MODEL), lambda b, i: (b, i, 0)),
                  full((D_MODEL, 256)), full((32, D_MODEL)),
                  full((16, 1)), full((16, 1)), full((1, 128)), full((1, 128))],
        out_specs=[pl.BlockSpec((1, 8, tg), lambda b, i: (b, 0, i)),
                   pl.BlockSpec((1, tg, 128), lambda b, i: (b, i, 0)),
                   pl.BlockSpec((1, tg, 128), lambda b, i: (b, i, 0))],
        out_shape=[jax.ShapeDtypeStruct((bsz, 8, t), F32), jax.ShapeDtypeStruct((bsz, t, 128), F32),
                   jax.ShapeDtypeStruct((bsz, t, 128), BF16)],
        scratch_shapes=[pltpu.VMEM((8, 128), F32)],
        compiler_params=_cparams(("parallel", "arbitrary")), name="gates",
    )(hb3, *gw)


def _mlstm_kernel(c_ref, v_ref, o_ref, rows_ref, cols_ref, cw_ref, cb_ref, wq_ref, wk_ref, ng_ref,
                  y_ref, cbuf, c_state, m_state, *, chunk, chunks):
    L = chunk
    t = pl.program_id(1)

    @pl.when(t == 0)
    def _():
        cbuf[0:8, :] = jnp.zeros((8, D_MODEL), F32)
        c_state[...] = jnp.zeros_like(c_state)
        m_state[...] = jnp.zeros_like(m_state)

    nc = L // 128
    lanes = lambda c: slice(c * 128, (c + 1) * 128)
    r_i = lax.broadcasted_iota(jnp.int32, (L, 128), 0)
    c_i = lax.broadcasted_iota(jnp.int32, (L, 128), 1)
    causal = [c_i + c * 128 <= r_i for c in range(nc)]
    ones_blk = jnp.ones((L, 128), BF16)
    for sub in range(chunks):
        rows_c = slice(sub * L, (sub + 1) * L)
        x = c_ref[0, rows_c].astype(F32)
        cbuf[8:8 + L, :] = x
        conv = cb_ref[...]
        for j in range(CONV_K):
            conv = conv + cbuf[8 - (CONV_K - 1) + j:8 - (CONV_K - 1) + j + L, :] * cw_ref[j:j + 1, :]
        cbuf[0:8, :] = x[L - 8:L, :]
        ub = _silu(conv).astype(BF16)

        for h in range(H_A):
            hs = slice(h * DH_A, (h + 1) * DH_A)
            uh = ub[:, hs]
            q = jnp.dot(uh, wq_ref[h], preferred_element_type=F32)
            k = jnp.dot(uh, wk_ref[h], preferred_element_type=F32) * (DH_A ** -0.5)
            qb = q.astype(BF16)
            kb = k.astype(BF16)
            v_aug = jnp.concatenate([v_ref[0, rows_c, hs], ones_blk], axis=1)
            b_rep = jnp.broadcast_to(cols_ref[0, rows_c, h:h + 1], (L, 128))
            c_rep = jnp.broadcast_to(cols_ref[0, rows_c, H_A + h:H_A + h + 1], (L, 128))
            b_row = rows_ref[0, h:h + 1, rows_c]
            c_row = rows_ref[0, H_A + h:H_A + h + 1, rows_c]
            m_prev = m_state[h, 0:1, :]

            dm = [jnp.where(causal[c], b_rep + c_row[:, lanes(c)], -jnp.inf) for c in range(nc)]
            dmax = dm[0]
            for c in range(1, nc):
                dmax = jnp.maximum(dmax, dm[c])
            inter = b_rep + m_prev
            m_t = jnp.maximum(jnp.max(dmax, axis=1, keepdims=True), inter)
            qk = lax.dot_general(qb, kb, NT_DIMS, preferred_element_type=F32)
            s = jnp.concatenate([(qk[:, lanes(c)] * jnp.exp(dm[c] - m_t)).astype(BF16) for c in range(nc)], axis=1)
            decay = jnp.exp(inter - m_t)
            c_old = c_state[h]
            inter_acc = jnp.dot(qb, c_old.astype(BF16), preferred_element_type=F32)
            acc = jnp.dot(s, v_aug, preferred_element_type=F32) + jnp.tile(decay, (1, DH_A // 128 + 1)) * inter_acc
            den = acc[:, DH_A:]
            inv = 1.0 / jnp.maximum(jnp.abs(den), jnp.exp(-m_t))
            hh = acc[:, :DH_A] * jnp.tile(inv, (1, DH_A // 128))

            g = b_row[:, L - 1:L]
            a_rep = g + c_rep
            m_new = jnp.maximum(g + m_prev, jnp.max(a_rep, axis=0, keepdims=True))
            kw = (k * jnp.tile(jnp.exp(a_rep - m_new), (1, DH_A // 128))).astype(BF16)
            sc = jnp.exp(g + m_prev - m_new)
            c_state[h] = (jnp.tile(sc, (1, DH_A // 128 + 1)) * c_old
                          + lax.dot_general(kw, v_aug, TN_DIMS, preferred_element_type=F32))
            m_state[h] = jnp.broadcast_to(m_new, m_state.shape[1:])

            mu = jnp.mean(hh, axis=1, keepdims=True)
            hc = hh - mu
            var = jnp.mean(hc * hc, axis=1, keepdims=True)
            hn = hc * lax.rsqrt(var + LN_EPS) * ng_ref[:, hs]
            og = _sigmoid(o_ref[0, rows_c, hs])
            y_ref[0, rows_c, hs] = (og.astype(F32) * hn).astype(BF16)


def _mlstm(p3, rows, cols, conv_w, conv_b, wq, wk, norm_g):
    bsz, t, _ = p3.shape
    L = min(MLSTM_CHUNK, t)
    chunks = min(MLSTM_CHUNKS_PER_STEP, t // L)
    tt = L * chunks
    colblk = lambda c: pl.BlockSpec((1, tt, 1024), lambda b, i, c=c: (b, i, c))
    full = lambda shape: pl.BlockSpec(shape, lambda b, i: tuple(0 for _ in shape))
    return pl.pallas_call(
        functools.partial(_mlstm_kernel, chunk=L, chunks=chunks),
        grid=(bsz, t // tt),
        in_specs=[colblk(COL_CA), colblk(COL_VA), colblk(COL_OA),
                  pl.BlockSpec((1, 8, tt), lambda b, i: (b, 0, i)),
                  pl.BlockSpec((1, tt, 128), lambda b, i: (b, i, 0)),
                  full((CONV_K, 1024)), full((1, 1024)), full((H_A, DH_A, DH_A)), full((H_A, DH_A, DH_A)),
                  full((1, 1024))],
        out_specs=pl.BlockSpec((1, tt, 1024), lambda b, i: (b, i, 0)),
        out_shape=jax.ShapeDtypeStruct((bsz, t, 1024), BF16),
        scratch_shapes=[pltpu.VMEM((8 + L, 1024), F32), pltpu.VMEM((H_A, DH_A, DH_A + 128), F32),
                        pltpu.VMEM((H_A, 8, 128), F32)],
        compiler_params=_cparams(("parallel", "arbitrary")), name="mlstm",
    )(p3, p3, p3, rows, cols, conv_w, conv_b.reshape(1, -1), wq, wk, norm_g.reshape(1, -1))


def _flash_kernel(*refs, tq, tk, dk, heads, has_bias):
    if has_bias:
        q_ref, k_ref, kb_ref, v_ref, o_ref, m_sc, l_sc, acc_sc = refs
    else:
        q_ref, k_ref, v_ref, o_ref, m_sc, l_sc, acc_sc = refs
        kb_ref = None
    i = pl.program_id(2)
    qs = []
    for g in range(heads):
        q = q_ref[0, :, g * dk:(g + 1) * dk]
        if has_bias:
            lane = lax.broadcasted_iota(jnp.int32, (tq, 128), 1) - 3 * (pl.program_id(1) * heads + g)
            q = jnp.concatenate([q, jnp.where((lane >= 0) & (lane < 3), -1.0, 0.0).astype(BF16)], axis=1)
        qs.append(q)

    def scores(g, key0, nk, r0, nr, mask_off):
        ks = pl.ds(key0, nk)
        kj = k_ref[0, ks, g * dk:(g + 1) * dk]
        if has_bias:
            kj = jnp.concatenate([kj, kb_ref[0, ks, :]], axis=1)
        s = lax.dot_general(qs[g][r0:r0 + nr], kj, NT_DIMS, preferred_element_type=F32)
        if mask_off is not None:
            r_i = r0 + lax.broadcasted_iota(jnp.int32, (nr, nk), 0)
            c_i = mask_off + lax.broadcasted_iota(jnp.int32, (nr, nk), 1)
            s = jnp.where(c_i <= r_i, s, -jnp.inf)
        return s

    def update(g, s, key0, nk, r0, nr, first):
        rs = slice(r0, r0 + nr)
        vj = v_ref[0, pl.ds(key0, nk), g * 128:(g + 1) * 128]
        s_max = jnp.max(s, axis=1, keepdims=True)
        if first:
            m_new = jnp.broadcast_to(s_max, (nr, 128))
            l_part = jnp.zeros((nr, 128), F32)
        else:
            m_prev = m_sc[g, rs]
            m_new = jnp.maximum(m_prev, s_max)
            alpha = jnp.exp2(m_prev - m_new)
            l_part = alpha * l_sc[g, rs]
        ps = []
        for c in range(nk // 128):
            p_c = jnp.exp2(s[:, c * 128:(c + 1) * 128] - m_new)
            l_part = l_part + p_c
            ps.append(p_c.astype(BF16))
        pv = jnp.dot(jnp.concatenate(ps, axis=1), vj, preferred_element_type=F32)
        acc_sc[g, rs] = pv if first else alpha * acc_sc[g, rs] + pv
        l_sc[g, rs] = l_part
        m_sc[g, rs] = m_new

    def finish(g):
        l = jnp.sum(l_sc[g], axis=1, keepdims=True)
        o_ref[0, :, g * 128:(g + 1) * 128] = (acc_sc[g] / l).astype(o_ref.dtype)

    def sweep(key0, nk, r0, nr, mask_off, first=False, last=False):
        s_cur = scores(0, key0, nk, r0, nr, mask_off)
        for g in range(heads):
            s_next = scores(g + 1, key0, nk, r0, nr, mask_off) if g + 1 < heads else None
            update(g, s_cur, key0, nk, r0, nr, first)
            if last:
                finish(g)
            s_cur = s_next

    nsub = tq // tk

    def body(j, carry):
        sweep(pl.multiple_of(j * tk, tk), tk, 0, tq, None)
        return carry

    def diag(d, **kw):
        sweep(pl.multiple_of(i * tq + d * tk, tk), tk, d * tk, tq - d * tk, d * tk, **kw)

    diag(0, first=True)
    lax.fori_loop(0, i * nsub, body, 0)
    for d in range(1, nsub):
        diag(d, last=(d == nsub - 1))
    if nsub == 1:
        for g in range(heads):
            finish(g)


def _flash(q_arr, k_arr, kb_arr, v_arr, *, n_heads, dk, q_blk, k_blk, v_blk):
    bsz, t, _ = q_arr.shape
    tq = min(FLASH_TQ, t)
    tk = min(FLASH_TK, t)
    g = FLASH_HEADS_PER_STEP
    hp = n_heads // g
    in_specs = [pl.BlockSpec((1, tq, g * dk), lambda b, h, i: (b, i, q_blk // g + h)),
                pl.BlockSpec((1, t, g * dk), lambda b, h, i: (b, 0, k_blk // g + h))]
    args = [q_arr, k_arr]
    if kb_arr is not None:
        in_specs.append(pl.BlockSpec((1, t, 128), lambda b, h, i: (b, 0, 0)))
        args.append(kb_arr)
    in_specs.append(pl.BlockSpec((1, t, g * 128), lambda b, h, i: (b, 0, v_blk // g + h)))
    args.append(v_arr)
    return pl.pallas_call(
        functools.partial(_flash_kernel, tq=tq, tk=tk, dk=dk, heads=g, has_bias=kb_arr is not None),
        grid=(bsz, hp, t // tq),
        in_specs=in_specs,
        out_specs=pl.BlockSpec((1, tq, g * 128), lambda b, h, i: (b, i, h)),
        out_shape=jax.ShapeDtypeStruct((bsz, t, n_heads * 128), BF16),
        scratch_shapes=[pltpu.VMEM((g, tq, 128), F32), pltpu.VMEM((g, tq, 128), F32), pltpu.VMEM((g, tq, 128), F32)],
        compiler_params=_cparams(("parallel", "parallel", "arbitrary")),
        name="flash_fox" if kb_arr is not None else "flash_mla",
    )(*args)


def _mla_prep_kernel(c_ref, tab_ref, gq_ref, gkv_ref, wq_ref, wkv_ref, q_out, k_out, v_out):
    c = c_ref[...].astype(F32)
    ckv = c[:, 0:KV_RANK]
    cq = c[:, KV_RANK:KV_RANK + Q_RANK]
    kr = c[:, KV_RANK + Q_RANK:]

    def rms(x, g):
        ms = jnp.mean(x * x, axis=-1, keepdims=True)
        return (x * lax.rsqrt(ms + RMS_EPS) * g).astype(BF16)

    qf = jnp.dot(rms(cq, gq_ref[...]), wq_ref[...], preferred_element_type=F32)
    kvf = jnp.dot(rms(ckv, gkv_ref[...]), wkv_ref[...], preferred_element_type=F32)
    tab = tab_ref[...]

    def rope(xx):
        pr = xx * tab
        return pr + pltpu.roll(pr, DH_ROPE, 1)

    lane = lax.broadcasted_iota(jnp.int32, kr.shape, 1)
    k_rope = jnp.where(lane < DH_ROPE, rope(kr), 0.0).astype(BF16)
    for h in range(H_C):
        lo = h * DK_C
        q_out[:, lo:lo + 128] = qf[:, lo:lo + 128].astype(BF16)
        q_out[:, lo + 128:lo + 256] = rope(qf[:, lo + 128:lo + 256]).astype(BF16)
        k_out[:, lo:lo + 128] = kvf[:, lo:lo + 128].astype(BF16)
        k_out[:, lo + 128:lo + 256] = k_rope
        v_out[:, h * DH_V:(h + 1) * DH_V] = kvf[:, lo + 128:lo + 256].astype(BF16)


def _mla_prep(p2, tab, g_cq, g_ckv, wq, wkv):
    n = p2.shape[0]
    tm = min(512, n)
    full = lambda shape: pl.BlockSpec(shape, lambda i: tuple(0 for _ in shape))
    return pl.pallas_call(
        _mla_prep_kernel, grid=(n // tm,),
        in_specs=[pl.BlockSpec((tm, LOWRANK_W), lambda i: (i, LOWRANK_OFF // LOWRANK_W)),
                  pl.BlockSpec((tm, 128), lambda i: (i, 0)),
                  full((1, Q_RANK)), full((1, KV_RANK)), full((Q_RANK, H_C * DK_C)), full((KV_RANK, H_C * DK_C))],
        out_specs=[pl.BlockSpec((tm, H_C * DK_C), lambda i: (i, 0)),
                   pl.BlockSpec((tm, H_C * DK_C), lambda i: (i, 0)),
                   pl.BlockSpec((tm, H_C * DH_V), lambda i: (i, 0))],
        out_shape=[jax.ShapeDtypeStruct((n, H_C * DK_C), BF16), jax.ShapeDtypeStruct((n, H_C * DK_C), BF16),
                   jax.ShapeDtypeStruct((n, H_C * DH_V), BF16)],
        compiler_params=_cparams(("parallel",)), name="mla_prep",
    )(p2, tab, g_cq.reshape(1, -1), g_ckv.reshape(1, -1), wq, wkv)


def _merge_kernel(ya_ref, yb_ref, yc_ref, za_ref, zb_ref, zc_ref, ga_ref, gb_ref, gc_ref, h_ref,
                  pa_ref, pb_ref, pc_ref, wo_ref, lg_ref, lb_ref, hf_ref, hb_ref, *, alpha):
    def front(rs):
        def branch(y_ref, z_ref, g_ref, p_ref):
            hz = z_ref[rs, :] * 0.5
            y = y_ref[rs, :] * (hz + hz * jnp.tanh(hz))
            gate = 0.5 + 0.5 * jnp.tanh(g_ref[rs, :] * 0.5)
            return gate.astype(F32) * jnp.dot(y, p_ref[...], preferred_element_type=F32)

        return (branch(ya_ref, za_ref, ga_ref, pa_ref) + branch(yb_ref, zb_ref, gb_ref, pb_ref)
                + branch(yc_ref, zc_ref, gc_ref, pc_ref)).astype(BF16)

    def back(rs, merged):
        out = jnp.dot(merged, wo_ref[...], preferred_element_type=F32)
        r = alpha * h_ref[rs, :] + out
        mu = jnp.mean(r, axis=-1, keepdims=True)
        rc = r - mu
        var = jnp.mean(rc * rc, axis=-1, keepdims=True)
        y = rc * lax.rsqrt(var + LN_EPS) * lg_ref[...] + lb_ref[...]
        hf_ref[rs, :] = y
        hb_ref[rs, :] = y.astype(BF16)

    tm = h_ref.shape[0]
    strips = [slice(r, r + tm // MERGE_STRIPS) for r in range(0, tm, tm // MERGE_STRIPS)]
    cur = front(strips[0])
    for n, rs in enumerate(strips):
        nxt = front(strips[n + 1]) if n + 1 < len(strips) else None
        back(rs, cur)
        cur = nxt


def _merge(ya, yb, yc, p2, hf, pa, pb, pc, wo, ln_g, ln_b, alpha):
    n = hf.shape[0]
    tm = min(512, n)
    row = pl.BlockSpec((tm, D_MODEL), lambda i: (i, 0))
    colblk = lambda c: pl.BlockSpec((tm, 1024), lambda i, c=c: (i, c))
    wfull = pl.BlockSpec((1024, D_MODEL), lambda i: (0, 0), pipeline_mode=pl.Buffered(1))
    vec = pl.BlockSpec((1, D_MODEL), lambda i: (0, 0))
    return pl.pallas_call(
        functools.partial(_merge_kernel, alpha=alpha), grid=(n // tm,),
        in_specs=[row, row, row, colblk(COL_ZA), colblk(COL_ZB), colblk(COL_ZC),
                  colblk(COL_GA), colblk(COL_GB), colblk(COL_GC), row,
                  wfull, wfull, wfull, wfull, vec, vec],
        out_specs=[row, row],
        out_shape=[jax.ShapeDtypeStruct((n, D_MODEL), F32), jax.ShapeDtypeStruct((n, D_MODEL), BF16)],
        compiler_params=_cparams(("parallel",)), name="merge",
    )(ya, yb, yc, p2, p2, p2, p2, p2, p2, hf, pa, pb, pc, wo, ln_g.reshape(1, -1), ln_b.reshape(1, -1))


def _rot_cols(w):
    half = w.shape[-1] // 2
    return jnp.concatenate([-w[..., half:], w[..., :half]], axis=-1)


def _layer_weights(w_in, b_ia, b_fa, b_fb, w_uq, w_ukv):
    idx = []
    acc = 0
    for s in IN_SPLITS[:-1]:
        acc += s
        idx.append(acc)
    (w_ca, w_va, w_oa, w_ia, w_fa, w_za, w_qb, w_kb, w_vb, w_fb, w_zb,
     w_cq, w_ckv, w_kr, w_zc, w_ga, w_gb, w_gc) = jnp.split(w_in, idx, axis=-1)
    w_qb = w_qb * (DH_B ** -0.5 * LOG2E)
    w_big = jnp.concatenate([w_ca, w_va, w_oa, w_za, w_qb, w_kb, w_vb, w_zb, w_zc, w_ga, w_gb, w_gc,
                             w_ckv, w_cq, w_kr, _rot_cols(w_kr)], axis=-1).astype(BF16)
    zpad = lambda k: jnp.zeros((D_MODEL, k), F32)
    w_f = jnp.concatenate([w_fa, w_fa, w_fb, zpad(112)], axis=-1)
    w_i = jnp.concatenate([zpad(4), w_ia, zpad(120)], axis=-1)
    b_f = jnp.concatenate([b_fa, b_fa, b_fb, jnp.zeros((112,), F32)])
    b_i = jnp.concatenate([jnp.zeros((4,), F32), b_ia, jnp.zeros((120,), F32)])
    gw = (jnp.concatenate([w_f, w_i], axis=1).astype(BF16),
          jnp.concatenate([w_f[:, :16], w_i[:, :16]], axis=1).T.astype(BF16),
          b_f[:16].reshape(16, 1), b_i[:16].reshape(16, 1), b_f.reshape(1, 128), b_i.reshape(1, 128))
    wq3 = w_uq.reshape(Q_RANK, H_C, DH_NOPE + DH_ROPE) * ((DH_NOPE + DH_ROPE) ** -0.5 * LOG2E)
    wq_rope = wq3[..., DH_NOPE:]
    wq = jnp.concatenate([wq3, _rot_cols(wq_rope)], axis=-1).reshape(Q_RANK, H_C * DK_C).astype(BF16)
    return w_big, gw, wq, w_ukv.astype(BF16)


def kernel(x, positions, ln_in_g, ln_in_b, w_in, conv_w, conv_b, w_qa, w_ka, b_ia, b_fa, norm_ga, b_fb,
           g_cq, w_uq, g_ckv, w_ukv, p_a, p_b, p_c, w_out, ln_g, ln_b):
    bsz, t, _ = x.shape
    n = bsz * t
    depth = w_in.shape[0]
    alpha = (2.0 * depth) ** 0.25

    tab = _rope_table(positions)
    hf, hb = _input_ln(x.reshape(n, D_MODEL), ln_in_g, ln_in_b)
    for l in range(depth):
        w_big, gw, wq, wkv = _layer_weights(w_in[l], b_ia[l], b_fa[l], b_fb[l], w_uq[l], w_ukv[l])
        p2 = _in_proj(hb, w_big)
        p3 = p2.reshape(bsz, t, N_BIG)
        rows, cols, kb = _gates(hb.reshape(bsz, t, D_MODEL), gw)
        ya = _mlstm(p3, rows, cols, conv_w[l], conv_b[l], w_qa[l].astype(BF16), w_ka[l].astype(BF16), norm_ga[l])
        yb = _flash(p3, p3, kb, p3, n_heads=H_B, dk=DH_B, q_blk=COL_QB * 8, k_blk=COL_KB * 8, v_blk=COL_VB * 8)
        qc, kc, vc = _mla_prep(p2, tab, g_cq[l], g_ckv[l], wq, wkv)
        yc = _flash(qc.reshape(bsz, t, -1), kc.reshape(bsz, t, -1), None, vc.reshape(bsz, t, -1),
                    n_heads=H_C, dk=DK_C, q_blk=0, k_blk=0, v_blk=0)
        hf, hb = _merge(ya.reshape(n, -1), yb.reshape(n, -1), yc.reshape(n, -1), p2, hf,
                        p_a[l].astype(BF16), p_b[l].astype(BF16), p_c[l].astype(BF16), w_out[l].astype(BF16),
                        ln_g[l], ln_b[l], alpha)
    return hf.reshape(bsz, t, D_MODEL)
```

```python
import functools
import math

import jax
import jax.numpy as jnp
from jax import lax
from jax.experimental import pallas as pl
from jax.experimental.pallas import tpu as pltpu

F32 = jnp.float32
BF16 = jnp.bfloat16

D_MODEL = 1024
H_A, DH_A, CONV_K = 4, 256, 4
H_B, DH_B = 8, 128
H_C, DH_NOPE, DH_ROPE, DH_V = 8, 128, 64, 128
Q_RANK, KV_RANK = 384, 256
ROPE_BASE = 10000.0
LN_EPS = 1e-5
RMS_EPS = 1e-6
IN_SPLITS = (1024, 1024, 1024, 4, 4, 1024,
             1024, 1024, 1024, 8, 1024,
             Q_RANK, KV_RANK, DH_ROPE, 1024,
             1024, 1024, 1024)

COL_CA, COL_VA, COL_OA, COL_ZA, COL_QB, COL_KB, COL_VB, COL_ZB, COL_ZC, COL_GA, COL_GB, COL_GC = range(12)
LOWRANK_OFF = 12 * 1024
LOWRANK_W = 768
N_BIG = LOWRANK_OFF + LOWRANK_W
DK_C = 256

MLSTM_CHUNK = 256
GATES_SEQS_PER_STEP = 4
FLASH_TQ = 1024
FLASH_TK = 512
FLASH_HEADS_PER_STEP = 4
MERGE_STRIPS = 2
VMEM_LIMIT = 56 * 1024 * 1024
LOG2E = math.log2(math.e)

NT_DIMS = (((1,), (1,)), ((), ()))
TN_DIMS = (((0,), (0,)), ((), ()))


def _cparams(sem):
    return pltpu.CompilerParams(dimension_semantics=sem, vmem_limit_bytes=VMEM_LIMIT)


def _sigmoid(x):
    return 0.5 + 0.5 * jnp.tanh(0.5 * x)


def _silu(x):
    hx = 0.5 * x
    return hx + hx * jnp.tanh(hx)


def _log_sigmoid(x):
    return jnp.minimum(x, 0.0) - jnp.log1p(jnp.exp(-jnp.abs(x)))


def _split3(x):
    hi = x.astype(BF16)
    r = x - hi.astype(F32)
    mid = r.astype(BF16)
    lo = (r - mid.astype(F32)).astype(BF16)
    return hi, mid, lo


def _ln_kernel(x_ref, g_ref, b_ref, hf_ref, hb_ref):
    x = x_ref[...]
    mu = jnp.mean(x, axis=-1, keepdims=True)
    xc = x - mu
    var = jnp.mean(xc * xc, axis=-1, keepdims=True)
    y = xc * lax.rsqrt(var + LN_EPS) * g_ref[...] + b_ref[...]
    hf_ref[...] = y
    hb_ref[...] = y.astype(BF16)


def _input_ln(x2, g, b):
    n = x2.shape[0]
    tm = min(512, n)
    row = pl.BlockSpec((tm, D_MODEL), lambda i: (i, 0))
    vec = pl.BlockSpec((1, D_MODEL), lambda i: (0, 0))
    return pl.pallas_call(
        _ln_kernel, grid=(n // tm,), in_specs=[row, vec, vec], out_specs=[row, row],
        out_shape=[jax.ShapeDtypeStruct((n, D_MODEL), F32), jax.ShapeDtypeStruct((n, D_MODEL), BF16)],
        compiler_params=_cparams(("parallel",)), name="input_ln",
    )(x2, g.reshape(1, -1), b.reshape(1, -1))


def _rope_kernel(pos_ref, freq_ref, tab_ref):
    ang = pos_ref[...].astype(F32) * freq_ref[...]
    lane = lax.broadcasted_iota(jnp.int32, ang.shape, 1)
    tab_ref[...] = jnp.where(lane < DH_ROPE, jnp.cos(ang), jnp.sin(ang))


def _rope_table(positions):
    n = positions.size
    tm = min(1024, n)
    inv_freq = ROPE_BASE ** (-jnp.arange(0, DH_ROPE, 2, dtype=F32) / DH_ROPE)
    freq = jnp.tile(inv_freq, 4).reshape(1, 128)
    return pl.pallas_call(
        _rope_kernel, grid=(n // tm,),
        in_specs=[pl.BlockSpec((tm, 1), lambda i: (i, 0)), pl.BlockSpec((1, 128), lambda i: (0, 0))],
        out_specs=pl.BlockSpec((tm, 128), lambda i: (i, 0)),
        out_shape=jax.ShapeDtypeStruct((n, 128), F32),
        compiler_params=_cparams(("parallel",)), name="rope_table",
    )(positions.reshape(n, 1), freq)


def _matmul_kernel(x_ref, w_ref, o_ref):
    o_ref[...] = jnp.dot(x_ref[...], w_ref[...], preferred_element_type=F32).astype(o_ref.dtype)


def _in_proj(hb, w_big):
    n = hb.shape[0]
    tm = min(1024, n)
    tn = N_BIG // 3
    return pl.pallas_call(
        _matmul_kernel, grid=(n // tm, N_BIG // tn),
        in_specs=[pl.BlockSpec((tm, D_MODEL), lambda i, j: (i, 0)),
                  pl.BlockSpec((D_MODEL, tn), lambda i, j: (0, j))],
        out_specs=pl.BlockSpec((tm, tn), lambda i, j: (i, j)),
        out_shape=jax.ShapeDtypeStruct((n, N_BIG), BF16),
        compiler_params=_cparams(("parallel", "arbitrary")), name="in_proj",
    )(hb, w_big)


def _gates_kernel(h_ref, wfi_ref, wfit_ref, bfc_ref, bic_ref, bfr_ref, bir_ref,
                  rows_ref, cols_ref, kb_ref, carry_ref, *, tg, chunk, nb):
    t = pl.program_id(1)

    @pl.when(t == 0)
    def _():
        carry_ref[...] = jnp.zeros_like(carry_ref)

    seq_rows = lambda a, i: a[i * tg:(i + 1) * tg]
    seq_lanes = lambda a, i, w: a[:, i * w:(i + 1) * w]
    hb = h_ref[...].reshape(nb * tg, D_MODEL)
    r_i = lax.broadcasted_iota(jnp.int32, (tg, tg), 0)
    c_i = lax.broadcasted_iota(jnp.int32, (tg, tg), 1)
    same_chunk = (r_i // chunk) == (c_i // chunk)
    ind = lambda mask: jnp.where(mask, 1.0, 0.0).astype(BF16)
    upper_bd = ind((r_i <= c_i) & same_chunk)
    lower = ind(r_i >= c_i)
    lower_bd = ind((r_i >= c_i) & same_chunk)

    g_rows = lax.dot_general(wfit_ref[...], hb, NT_DIMS, preferred_element_type=F32)
    gf = g_rows[0:16] + bfc_ref[...]
    gi = g_rows[16:32] + bic_ref[...]
    lf3 = jnp.concatenate(_split3(_log_sigmoid(gf)), axis=0)
    y3 = jnp.dot(jnp.concatenate([seq_lanes(lf3, i, tg) for i in range(nb)], axis=0), upper_bd,
                 preferred_element_type=F32)
    row = lax.broadcasted_iota(jnp.int32, (16, tg), 0)
    for i in range(nb):
        y_bd = y3[48 * i:48 * i + 16] + y3[48 * i + 16:48 * i + 32] + y3[48 * i + 32:48 * i + 48]
        rows_ref[i] = jnp.where(row < 4, y_bd, seq_lanes(gi, i, tg) - y_bd)[0:8]

    g_cols = jnp.dot(hb, wfi_ref[...], preferred_element_type=F32)
    gfc = g_cols[:, 0:128] + bfr_ref[...]
    gic = g_cols[:, 128:256] + bir_ref[...]
    lfc3 = jnp.concatenate(_split3(_log_sigmoid(gfc)), axis=1)
    lfc3 = jnp.concatenate([seq_rows(lfc3, i) for i in range(nb)], axis=1)
    y_bd_c = jnp.dot(lower_bd, lfc3, preferred_element_type=F32)
    y_full_c = jnp.dot(lower, lfc3, preferred_element_type=F32)
    sum3 = lambda a: a[:, 0:128] + a[:, 128:256] + a[:, 256:384]
    col = lax.broadcasted_iota(jnp.int32, (tg, 128), 1)
    parts = []
    for i in range(nb):
        yc = sum3(seq_lanes(y_bd_c, i, 384))
        cols_ref[i] = jnp.where(col < 4, yc, seq_rows(gic, i) - yc)
        cum = sum3(seq_lanes(y_full_c, i, 384)) + carry_ref[i, 0:1, :]
        carry_ref[i] = jnp.broadcast_to(cum[tg - 1:tg, :], carry_ref.shape[1:])
        parts.append(jnp.concatenate(_split3(cum * LOG2E), axis=1))
    sr = lax.broadcasted_iota(jnp.int32, (3 * 128, 128), 0)
    sc = lax.broadcasted_iota(jnp.int32, (3 * 128, 128), 1)
    scatter = ind((sr % 128 == 8 + sc // 3) & (sr // 128 == sc % 3) & (sc < 3 * H_B))
    kb = jnp.dot(jnp.concatenate(parts, axis=0), scatter, preferred_element_type=F32).astype(BF16)
    for i in range(nb):
        kb_ref[i] = seq_rows(kb, i)


def _gates(hb3, gw):
    bsz, t, _ = hb3.shape
    tg = min(512, t)
    nb = math.gcd(GATES_SEQS_PER_STEP, bsz)
    full = lambda shape: pl.BlockSpec(shape, lambda b, i: tuple(0 for _ in shape))
    return pl.pallas_call(
        functools.partial(_gates_kernel, tg=tg, chunk=min(MLSTM_CHUNK, t), nb=nb),
        grid=(bsz // nb, t // tg),
        in_specs=[pl.BlockSpec((nb, tg, D_MODEL), lambda b, i: (b, i, 0)),
                  full((D_MODEL, 256)), full((32, D_MODEL)),
                  full((16, 1)), full((16, 1)), full((1, 128)), full((1, 128))],
        out_specs=[pl.BlockSpec((nb, 8, tg), lambda b, i: (b, 0, i)),
                   pl.BlockSpec((nb, tg, 128), lambda b, i: (b, i, 0)),
                   pl.BlockSpec((nb, tg, 128), lambda b, i: (b, i, 0))],
        out_shape=[jax.ShapeDtypeStruct((bsz, 8, t), F32), jax.ShapeDtypeStruct((bsz, t, 128), F32),
                   jax.ShapeDtypeStruct((bsz, t, 128), BF16)],
        scratch_shapes=[pltpu.VMEM((nb, 8, 128), F32)],
        compiler_params=_cparams(("parallel", "arbitrary")), name="gates",
    )(hb3, *gw)


def _mlstm_kernel(c_ref, v_ref, o_ref, rows_ref, cols_ref, cw_ref, cb_ref, wq_ref, wk_ref, ng_ref,
                  y_ref, cbuf, c_state, m_state, *, chunk):
    L = chunk
    t = pl.program_id(1)

    @pl.when(t == 0)
    def _():
        cbuf[0:8, :] = jnp.zeros((8, D_MODEL), F32)
        c_state[...] = jnp.zeros_like(c_state)
        m_state[...] = jnp.zeros_like(m_state)

    x = c_ref[0].astype(F32)
    cbuf[8:8 + L, :] = x
    conv = cb_ref[...]
    for j in range(CONV_K):
        conv = conv + cbuf[8 - (CONV_K - 1) + j:8 - (CONV_K - 1) + j + L, :] * cw_ref[j:j + 1, :]
    cbuf[0:8, :] = x[L - 8:L, :]
    ub = _silu(conv).astype(BF16)

    nc = L // 128
    lanes = lambda c: slice(c * 128, (c + 1) * 128)
    r_i = lax.broadcasted_iota(jnp.int32, (L, 128), 0)
    c_i = lax.broadcasted_iota(jnp.int32, (L, 128), 1)
    causal = [c_i + c * 128 <= r_i for c in range(nc)]
    ones_blk = jnp.ones((L, 128), BF16)
    for h in range(H_A):
        hs = slice(h * DH_A, (h + 1) * DH_A)
        uh = ub[:, hs]
        q = jnp.dot(uh, wq_ref[h], preferred_element_type=F32)
        k = jnp.dot(uh, wk_ref[h], preferred_element_type=F32) * (DH_A ** -0.5)
        qb = q.astype(BF16)
        kb = k.astype(BF16)
        v_aug = jnp.concatenate([v_ref[0, :, hs], ones_blk], axis=1)
        b_rep = jnp.broadcast_to(cols_ref[0, :, h:h + 1], (L, 128))
        c_rep = jnp.broadcast_to(cols_ref[0, :, H_A + h:H_A + h + 1], (L, 128))
        b_row = rows_ref[0, h:h + 1, :]
        c_row = rows_ref[0, H_A + h:H_A + h + 1, :]
        m_prev = m_state[h, 0:1, :]

        dm = [jnp.where(causal[c], b_rep + c_row[:, lanes(c)], -jnp.inf) for c in range(nc)]
        dmax = dm[0]
        for c in range(1, nc):
            dmax = jnp.maximum(dmax, dm[c])
        inter = b_rep + m_prev
        m_t = jnp.maximum(jnp.max(dmax, axis=1, keepdims=True), inter)
        qk = lax.dot_general(qb, kb, NT_DIMS, preferred_element_type=F32)
        s = jnp.concatenate([(qk[:, lanes(c)] * jnp.exp(dm[c] - m_t)).astype(BF16) for c in range(nc)], axis=1)
        decay = jnp.exp(inter - m_t)
        c_old = c_state[h]
        inter_acc = jnp.dot(qb, c_old.astype(BF16), preferred_element_type=F32)
        acc = jnp.dot(s, v_aug, preferred_element_type=F32) + jnp.tile(decay, (1, DH_A // 128 + 1)) * inter_acc
        den = acc[:, DH_A:]
        inv = 1.0 / jnp.maximum(jnp.abs(den), jnp.exp(-m_t))
        hh = acc[:, :DH_A] * jnp.tile(inv, (1, DH_A // 128))

        g = b_row[:, L - 1:L]
        a_rep = g + c_rep
        m_new = jnp.maximum(g + m_prev, jnp.max(a_rep, axis=0, keepdims=True))
        kw = (k * jnp.tile(jnp.exp(a_rep - m_new), (1, DH_A // 128))).astype(BF16)
        sc = jnp.exp(g + m_prev - m_new)
        c_state[h] = (jnp.tile(sc, (1, DH_A // 128 + 1)) * c_old
                      + lax.dot_general(kw, v_aug, TN_DIMS, preferred_element_type=F32))
        m_state[h] = jnp.broadcast_to(m_new, m_state.shape[1:])

        mu = jnp.mean(hh, axis=1, keepdims=True)
        hc = hh - mu
        var = jnp.mean(hc * hc, axis=1, keepdims=True)
        hn = hc * lax.rsqrt(var + LN_EPS) * ng_ref[:, hs]
        og = _sigmoid(o_ref[0, :, hs])
        y_ref[0, :, hs] = (og.astype(F32) * hn).astype(BF16)


def _mlstm(p3, rows, cols, conv_w, conv_b, wq, wk, norm_g):
    bsz, t, _ = p3.shape
    L = min(MLSTM_CHUNK, t)
    colblk = lambda c: pl.BlockSpec((1, L, 1024), lambda b, i, c=c: (b, i, c))
    full = lambda shape: pl.BlockSpec(shape, lambda b, i: tuple(0 for _ in shape))
    return pl.pallas_call(
        functools.partial(_mlstm_kernel, chunk=L),
        grid=(bsz, t // L),
        in_specs=[colblk(COL_CA), colblk(COL_VA), colblk(COL_OA),
                  pl.BlockSpec((1, 8, L), lambda b, i: (b, 0, i)),
                  pl.BlockSpec((1, L, 128), lambda b, i: (b, i, 0)),
                  full((CONV_K, 1024)), full((1, 1024)), full((H_A, DH_A, DH_A)), full((H_A, DH_A, DH_A)),
                  full((1, 1024))],
        out_specs=pl.BlockSpec((1, L, 1024), lambda b, i: (b, i, 0)),
        out_shape=jax.ShapeDtypeStruct((bsz, t, 1024), BF16),
        scratch_shapes=[pltpu.VMEM((8 + L, 1024), F32), pltpu.VMEM((H_A, DH_A, DH_A + 128), F32),
                        pltpu.VMEM((H_A, 8, 128), F32)],
        compiler_params=_cparams(("parallel", "arbitrary")), name="mlstm",
    )(p3, p3, p3, rows, cols, conv_w, conv_b.reshape(1, -1), wq, wk, norm_g.reshape(1, -1))


def _flash_kernel(*refs, tq, tk, dk, heads, has_bias):
    if has_bias:
        q_ref, k_ref, kb_ref, v_ref, o_ref, m_sc, l_sc, acc_sc = refs
    else:
        q_ref, k_ref, v_ref, o_ref, m_sc, l_sc, acc_sc = refs
        kb_ref = None
    i = pl.program_id(2)
    qs = []
    for g in range(heads):
        q = q_ref[0, :, g * dk:(g + 1) * dk]
        if has_bias:
            lane = lax.broadcasted_iota(jnp.int32, (tq, 128), 1) - 3 * (pl.program_id(1) * heads + g)
            q = jnp.concatenate([q, jnp.where((lane >= 0) & (lane < 3), -1.0, 0.0).astype(BF16)], axis=1)
        qs.append(q)

    def scores(g, key0, nk, r0, nr, mask_off):
        ks = pl.ds(key0, nk)
        kj = k_ref[0, ks, g * dk:(g + 1) * dk]
        if has_bias:
            kj = jnp.concatenate([kj, kb_ref[0, ks, :]], axis=1)
        s = lax.dot_general(qs[g][r0:r0 + nr], kj, NT_DIMS, preferred_element_type=F32)
        if mask_off is not None:
            r_i = r0 + lax.broadcasted_iota(jnp.int32, (nr, nk), 0)
            c_i = mask_off + lax.broadcasted_iota(jnp.int32, (nr, nk), 1)
            s = jnp.where(c_i <= r_i, s, -jnp.inf)
        return s

    def update(g, s, key0, nk, r0, nr, first):
        rs = slice(r0, r0 + nr)
        vj = v_ref[0, pl.ds(key0, nk), g * 128:(g + 1) * 128]
        s_max = jnp.max(s, axis=1, keepdims=True)
        if first:
            m_new = jnp.broadcast_to(s_max, (nr, 128))
            l_part = jnp.zeros((nr, 128), F32)
        else:
            m_prev = m_sc[g, rs]
            m_new = jnp.maximum(m_prev, s_max)
            alpha = jnp.exp2(m_prev - m_new)
            l_part = alpha * l_sc[g, rs]
        ps = []
        for c in range(nk // 128):
            p_c = jnp.exp2(s[:, c * 128:(c + 1) * 128] - m_new)
            l_part = l_part + p_c
            ps.append(p_c.astype(BF16))
        pv = jnp.dot(jnp.concatenate(ps, axis=1), vj, preferred_element_type=F32)
        acc_sc[g, rs] = pv if first else alpha * acc_sc[g, rs] + pv
        l_sc[g, rs] = l_part
        m_sc[g, rs] = m_new

    def finish(g):
        l = jnp.sum(l_sc[g], axis=1, keepdims=True)
        o_ref[0, :, g * 128:(g + 1) * 128] = (acc_sc[g] / l).astype(o_ref.dtype)

    def sweep(key0, nk, r0, nr, mask_off, first=False, last=False):
        s_cur = scores(0, key0, nk, r0, nr, mask_off)
        for g in range(heads):
            s_next = scores(g + 1, key0, nk, r0, nr, mask_off) if g + 1 < heads else None
            update(g, s_cur, key0, nk, r0, nr, first)
            if last:
                finish(g)
            s_cur = s_next

    nsub = tq // tk

    def body(j, carry):
        sweep(pl.multiple_of(j * tk, tk), tk, 0, tq, None)
        return carry

    def diag(d, **kw):
        sweep(pl.multiple_of(i * tq + d * tk, tk), tk, d * tk, tq - d * tk, d * tk, **kw)

    diag(0, first=True)
    lax.fori_loop(0, i * nsub, body, 0)
    for d in range(1, nsub):
        diag(d, last=(d == nsub - 1))
    if nsub == 1:
        for g in range(heads):
            finish(g)


def _flash(q_arr, k_arr, kb_arr, v_arr, *, n_heads, dk, q_blk, k_blk, v_blk):
    bsz, t, _ = q_arr.shape
    tq = min(FLASH_TQ, t)
    tk = min(FLASH_TK, t)
    g = FLASH_HEADS_PER_STEP
    hp = n_heads // g
    in_specs = [pl.BlockSpec((1, tq, g * dk), lambda b, h, i: (b, i, q_blk // g + h)),
                pl.BlockSpec((1, t, g * dk), lambda b, h, i: (b, 0, k_blk // g + h))]
    args = [q_arr, k_arr]
    if kb_arr is not None:
        in_specs.append(pl.BlockSpec((1, t, 128), lambda b, h, i: (b, 0, 0)))
        args.append(kb_arr)
    in_specs.append(pl.BlockSpec((1, t, g * 128), lambda b, h, i: (b, 0, v_blk // g + h)))
    args.append(v_arr)
    return pl.pallas_call(
        functools.partial(_flash_kernel, tq=tq, tk=tk, dk=dk, heads=g, has_bias=kb_arr is not None),
        grid=(bsz, hp, t // tq),
        in_specs=in_specs,
        out_specs=pl.BlockSpec((1, tq, g * 128), lambda b, h, i: (b, i, h)),
        out_shape=jax.ShapeDtypeStruct((bsz, t, n_heads * 128), BF16),
        scratch_shapes=[pltpu.VMEM((g, tq, 128), F32), pltpu.VMEM((g, tq, 128), F32), pltpu.VMEM((g, tq, 128), F32)],
        compiler_params=_cparams(("parallel", "parallel", "arbitrary")),
        name="flash_fox" if kb_arr is not None else "flash_mla",
    )(*args)


def _mla_prep_kernel(c_ref, tab_ref, gq_ref, gkv_ref, wq_ref, wkv_ref, q_out, k_out, v_out):
    c = c_ref[...].astype(F32)
    ckv = c[:, 0:KV_RANK]
    cq = c[:, KV_RANK:KV_RANK + Q_RANK]
    kr = c[:, KV_RANK + Q_RANK:]

    def rms(x, g):
        ms = jnp.mean(x * x, axis=-1, keepdims=True)
        return (x * lax.rsqrt(ms + RMS_EPS) * g).astype(BF16)

    qf = jnp.dot(rms(cq, gq_ref[...]), wq_ref[...], preferred_element_type=F32)
    kvf = jnp.dot(rms(ckv, gkv_ref[...]), wkv_ref[...], preferred_element_type=F32)
    tab = tab_ref[...]

    def rope(xx):
        pr = xx * tab
        return pr + pltpu.roll(pr, DH_ROPE, 1)

    lane = lax.broadcasted_iota(jnp.int32, kr.shape, 1)
    k_rope = jnp.where(lane < DH_ROPE, rope(kr), 0.0).astype(BF16)
    for h in range(H_C):
        lo = h * DK_C
        q_out[:, lo:lo + 128] = qf[:, lo:lo + 128].astype(BF16)
        q_out[:, lo + 128:lo + 256] = rope(qf[:, lo + 128:lo + 256]).astype(BF16)
        k_out[:, lo:lo + 128] = kvf[:, lo:lo + 128].astype(BF16)
        k_out[:, lo + 128:lo + 256] = k_rope
        v_out[:, h * DH_V:(h + 1) * DH_V] = kvf[:, lo + 128:lo + 256].astype(BF16)


def _mla_prep(p2, tab, g_cq, g_ckv, wq, wkv):
    n = p2.shape[0]
    tm = min(512, n)
    full = lambda shape: pl.BlockSpec(shape, lambda i: tuple(0 for _ in shape))
    return pl.pallas_call(
        _mla_prep_kernel, grid=(n // tm,),
        in_specs=[pl.BlockSpec((tm, LOWRANK_W), lambda i: (i, LOWRANK_OFF // LOWRANK_W)),
                  pl.BlockSpec((tm, 128), lambda i: (i, 0)),
                  full((1, Q_RANK)), full((1, KV_RANK)), full((Q_RANK, H_C * DK_C)), full((KV_RANK, H_C * DK_C))],
        out_specs=[pl.BlockSpec((tm, H_C * DK_C), lambda i: (i, 0)),
                   pl.BlockSpec((tm, H_C * DK_C), lambda i: (i, 0)),
                   pl.BlockSpec((tm, H_C * DH_V), lambda i: (i, 0))],
        out_shape=[jax.ShapeDtypeStruct((n, H_C * DK_C), BF16), jax.ShapeDtypeStruct((n, H_C * DK_C), BF16),
                   jax.ShapeDtypeStruct((n, H_C * DH_V), BF16)],
        compiler_params=_cparams(("parallel",)), name="mla_prep",
    )(p2, tab, g_cq.reshape(1, -1), g_ckv.reshape(1, -1), wq, wkv)


def _merge_kernel(ya_ref, yb_ref, yc_ref, za_ref, zb_ref, zc_ref, ga_ref, gb_ref, gc_ref, h_ref,
                  pa_ref, pb_ref, pc_ref, wo_ref, lg_ref, lb_ref, hf_ref, hb_ref, *, alpha):
    def front(rs):
        def branch(y_ref, z_ref, g_ref, p_ref):
            hz = z_ref[rs, :] * 0.5
            y = y_ref[rs, :] * (hz + hz * jnp.tanh(hz))
            gate = 0.5 + 0.5 * jnp.tanh(g_ref[rs, :] * 0.5)
            return gate.astype(F32) * jnp.dot(y, p_ref[...], preferred_element_type=F32)

        return (branch(ya_ref, za_ref, ga_ref, pa_ref) + branch(yb_ref, zb_ref, gb_ref, pb_ref)
                + branch(yc_ref, zc_ref, gc_ref, pc_ref)).astype(BF16)

    def back(rs, merged):
        out = jnp.dot(merged, wo_ref[...], preferred_element_type=F32)
        r = alpha * h_ref[rs, :] + out
        mu = jnp.mean(r, axis=-1, keepdims=True)
        rc = r - mu
        var = jnp.mean(rc * rc, axis=-1, keepdims=True)
        y = rc * lax.rsqrt(var + LN_EPS) * lg_ref[...] + lb_ref[...]
        hf_ref[rs, :] = y
        hb_ref[rs, :] = y.astype(BF16)

    tm = h_ref.shape[0]
    strips = [slice(r, r + tm // MERGE_STRIPS) for r in range(0, tm, tm // MERGE_STRIPS)]
    cur = front(strips[0])
    for n, rs in enumerate(strips):
        nxt = front(strips[n + 1]) if n + 1 < len(strips) else None
        back(rs, cur)
        cur = nxt


def _merge(ya, yb, yc, p2, hf, pa, pb, pc, wo, ln_g, ln_b, alpha):
    n = hf.shape[0]
    tm = min(512, n)
    row = pl.BlockSpec((tm, D_MODEL), lambda i: (i, 0))
    colblk = lambda c: pl.BlockSpec((tm, 1024), lambda i, c=c: (i, c))
    wfull = pl.BlockSpec((1024, D_MODEL), lambda i: (0, 0), pipeline_mode=pl.Buffered(1))
    vec = pl.BlockSpec((1, D_MODEL), lambda i: (0, 0))
    return pl.pallas_call(
        functools.partial(_merge_kernel, alpha=alpha), grid=(n // tm,),
        in_specs=[row, row, row, colblk(COL_ZA), colblk(COL_ZB), colblk(COL_ZC),
                  colblk(COL_GA), colblk(COL_GB), colblk(COL_GC), row,
                  wfull, wfull, wfull, wfull, vec, vec],
        out_specs=[row, row],
        out_shape=[jax.ShapeDtypeStruct((n, D_MODEL), F32), jax.ShapeDtypeStruct((n, D_MODEL), BF16)],
        compiler_params=_cparams(("parallel",)), name="merge",
    )(ya, yb, yc, p2, p2, p2, p2, p2, p2, hf, pa, pb, pc, wo, ln_g.reshape(1, -1), ln_b.reshape(1, -1))


def _rot_cols(w):
    half = w.shape[-1] // 2
    return jnp.concatenate([-w[..., half:], w[..., :half]], axis=-1)


def _layer_weights(w_in, b_ia, b_fa, b_fb, w_uq, w_ukv):
    idx = []
    acc = 0
    for s in IN_SPLITS[:-1]:
        acc += s
        idx.append(acc)
    (w_ca, w_va, w_oa, w_ia, w_fa, w_za, w_qb, w_kb, w_vb, w_fb, w_zb,
     w_cq, w_ckv, w_kr, w_zc, w_ga, w_gb, w_gc) = jnp.split(w_in, idx, axis=-1)
    w_qb = w_qb * (DH_B ** -0.5 * LOG2E)
    w_big = jnp.concatenate([w_ca, w_va, w_oa, w_za, w_qb, w_kb, w_vb, w_zb, w_zc, w_ga, w_gb, w_gc,
                             w_ckv, w_cq, w_kr, _rot_cols(w_kr)], axis=-1).astype(BF16)
    zpad = lambda k: jnp.zeros((D_MODEL, k), F32)
    w_f = jnp.concatenate([w_fa, w_fa, w_fb, zpad(112)], axis=-1)
    w_i = jnp.concatenate([zpad(4), w_ia, zpad(120)], axis=-1)
    b_f = jnp.concatenate([b_fa, b_fa, b_fb, jnp.zeros((112,), F32)])
    b_i = jnp.concatenate([jnp.zeros((4,), F32), b_ia, jnp.zeros((120,), F32)])
    gw = (jnp.concatenate([w_f, w_i], axis=1).astype(BF16),
          jnp.concatenate([w_f[:, :16], w_i[:, :16]], axis=1).T.astype(BF16),
          b_f[:16].reshape(16, 1), b_i[:16].reshape(16, 1), b_f.reshape(1, 128), b_i.reshape(1, 128))
    wq3 = w_uq.reshape(Q_RANK, H_C, DH_NOPE + DH_ROPE) * ((DH_NOPE + DH_ROPE) ** -0.5 * LOG2E)
    wq_rope = wq3[..., DH_NOPE:]
    wq = jnp.concatenate([wq3, _rot_cols(wq_rope)], axis=-1).reshape(Q_RANK, H_C * DK_C).astype(BF16)
    return w_big, gw, wq, w_ukv.astype(BF16)


def kernel(x, positions, ln_in_g, ln_in_b, w_in, conv_w, conv_b, w_qa, w_ka, b_ia, b_fa, norm_ga, b_fb,
           g_cq, w_uq, g_ckv, w_ukv, p_a, p_b, p_c, w_out, ln_g, ln_b):
    bsz, t, _ = x.shape
    n = bsz * t
    depth = w_in.shape[0]
    alpha = (2.0 * depth) ** 0.25

    tab = _rope_table(positions)
    hf, hb = _input_ln(x.reshape(n, D_MODEL), ln_in_g, ln_in_b)
    for l in range(depth):
        w_big, gw, wq, wkv = _layer_weights(w_in[l], b_ia[l], b_fa[l], b_fb[l], w_uq[l], w_ukv[l])
        p2 = _in_proj(hb, w_big)
        p3 = p2.reshape(bsz, t, N_BIG)
        rows, cols, kb = _gates(hb.reshape(bsz, t, D_MODEL), gw)
        ya = _mlstm(p3, rows, cols, conv_w[l], conv_b[l], w_qa[l].astype(BF16), w_ka[l].astype(BF16), norm_ga[l])
        yb = _flash(p3, p3, kb, p3, n_heads=H_B, dk=DH_B, q_blk=COL_QB * 8, k_blk=COL_KB * 8, v_blk=COL_VB * 8)
        qc, kc, vc = _mla_prep(p2, tab, g_cq[l], g_ckv[l], wq, wkv)
        yc = _flash(qc.reshape(bsz, t, -1), kc.reshape(bsz, t, -1), None, vc.reshape(bsz, t, -1),
                    n_heads=H_C, dk=DK_C, q_blk=0, k_blk=0, v_blk=0)
        hf, hb = _merge(ya.reshape(n, -1), yb.reshape(n, -1), yc.reshape(n, -1), p2, hf,
                        p_a[l].astype(BF16), p_b[l].astype(BF16), p_c[l].astype(BF16), w_out[l].astype(BF16),
                        ln_g[l], ln_b[l], alpha)
    return hf.reshape(bsz, t, D_MODEL)
```

```python
import functools
import math

import jax
import jax.numpy as jnp
from jax import lax
from jax.experimental import pallas as pl
from jax.experimental.pallas import tpu as pltpu

F32 = jnp.float32
BF16 = jnp.bfloat16

D_MODEL = 1024
H_A, DH_A, CONV_K = 4, 256, 4
H_B, DH_B = 8, 128
H_C, DH_NOPE, DH_ROPE, DH_V = 8, 128, 64, 128
Q_RANK, KV_RANK = 384, 256
ROPE_BASE = 10000.0
LN_EPS = 1e-5
RMS_EPS = 1e-6
IN_SPLITS = (1024, 1024, 1024, 4, 4, 1024,
             1024, 1024, 1024, 8, 1024,
             Q_RANK, KV_RANK, DH_ROPE, 1024,
             1024, 1024, 1024)

COL_CA, COL_VA, COL_OA, COL_ZA, COL_QB, COL_KB, COL_VB, COL_ZB, COL_ZC, COL_GA, COL_GB, COL_GC = range(12)
LOWRANK_OFF = 12 * 1024
LOWRANK_W = 768
N_BIG = LOWRANK_OFF + LOWRANK_W
DK_C = 256

MLSTM_CHUNK = 256
GATES_SEQS_PER_STEP = 4
FLASH_TQ = 1024
FLASH_TK = 512
FLASH_HEADS_PER_STEP = 4
MERGE_STRIPS = 2
VMEM_LIMIT = 56 * 1024 * 1024
LOG2E = math.log2(math.e)

NT_DIMS = (((1,), (1,)), ((), ()))
TN_DIMS = (((0,), (0,)), ((), ()))


def _cparams(sem):
    return pltpu.CompilerParams(dimension_semantics=sem, vmem_limit_bytes=VMEM_LIMIT)


def _sigmoid(x):
    return 0.5 + 0.5 * jnp.tanh(0.5 * x)


def _silu(x):
    hx = 0.5 * x
    return hx + hx * jnp.tanh(hx)


def _log_sigmoid(x):
    return jnp.minimum(x, 0.0) - jnp.log1p(jnp.exp(-jnp.abs(x)))


def _split3(x):
    hi = x.astype(BF16)
    r = x - hi.astype(F32)
    mid = r.astype(BF16)
    lo = (r - mid.astype(F32)).astype(BF16)
    return hi, mid, lo


def _ln_kernel(x_ref, g_ref, b_ref, hf_ref, hb_ref):
    x = x_ref[...]
    mu = jnp.mean(x, axis=-1, keepdims=True)
    xc = x - mu
    var = jnp.mean(xc * xc, axis=-1, keepdims=True)
    y = xc * lax.rsqrt(var + LN_EPS) * g_ref[...] + b_ref[...]
    hf_ref[...] = y
    hb_ref[...] = y.astype(BF16)


def _input_ln(x2, g, b):
    n = x2.shape[0]
    tm = min(512, n)
    row = pl.BlockSpec((tm, D_MODEL), lambda i: (i, 0))
    vec = pl.BlockSpec((1, D_MODEL), lambda i: (0, 0))
    return pl.pallas_call(
        _ln_kernel, grid=(n // tm,), in_specs=[row, vec, vec], out_specs=[row, row],
        out_shape=[jax.ShapeDtypeStruct((n, D_MODEL), F32), jax.ShapeDtypeStruct((n, D_MODEL), BF16)],
        compiler_params=_cparams(("parallel",)), name="input_ln",
    )(x2, g.reshape(1, -1), b.reshape(1, -1))


def _rope_kernel(pos_ref, freq_ref, tab_ref):
    ang = pos_ref[...].astype(F32) * freq_ref[...]
    lane = lax.broadcasted_iota(jnp.int32, ang.shape, 1)
    tab_ref[...] = jnp.where(lane < DH_ROPE, jnp.cos(ang), jnp.sin(ang))


def _rope_table(positions):
    n = positions.size
    tm = min(1024, n)
    inv_freq = ROPE_BASE ** (-jnp.arange(0, DH_ROPE, 2, dtype=F32) / DH_ROPE)
    freq = jnp.tile(inv_freq, 4).reshape(1, 128)
    return pl.pallas_call(
        _rope_kernel, grid=(n // tm,),
        in_specs=[pl.BlockSpec((tm, 1), lambda i: (i, 0)), pl.BlockSpec((1, 128), lambda i: (0, 0))],
        out_specs=pl.BlockSpec((tm, 128), lambda i: (i, 0)),
        out_shape=jax.ShapeDtypeStruct((n, 128), F32),
        compiler_params=_cparams(("parallel",)), name="rope_table",
    )(positions.reshape(n, 1), freq)


def _matmul_kernel(x_ref, w_ref, o_ref):
    o_ref[...] = jnp.dot(x_ref[...], w_ref[...], preferred_element_type=F32).astype(o_ref.dtype)


def _in_proj(hb, w_big):
    n = hb.shape[0]
    tm = min(1024, n)
    tn = N_BIG // 3
    return pl.pallas_call(
        _matmul_kernel, grid=(n // tm, N_BIG // tn),
        in_specs=[pl.BlockSpec((tm, D_MODEL), lambda i, j: (i, 0)),
                  pl.BlockSpec((D_MODEL, tn), lambda i, j: (0, j))],
        out_specs=pl.BlockSpec((tm, tn), lambda i, j: (i, j)),
        out_shape=jax.ShapeDtypeStruct((n, N_BIG), BF16),
        compiler_params=_cparams(("parallel", "arbitrary")), name="in_proj",
    )(hb, w_big)


def _gates_kernel(h_ref, wfi_ref, wfit_ref, bfc_ref, bic_ref, bfr_ref, bir_ref,
                  rows_ref, cols_ref, kb_ref, carry_ref, *, tg, chunk, nb):
    t = pl.program_id(1)

    @pl.when(t == 0)
    def _():
        carry_ref[...] = jnp.zeros_like(carry_ref)

    seq_rows = lambda a, i: a[i * tg:(i + 1) * tg]
    seq_lanes = lambda a, i, w: a[:, i * w:(i + 1) * w]
    hb = h_ref[...].reshape(nb * tg, D_MODEL)
    r_i = lax.broadcasted_iota(jnp.int32, (tg, tg), 0)
    c_i = lax.broadcasted_iota(jnp.int32, (tg, tg), 1)
    same_chunk = (r_i // chunk) == (c_i // chunk)
    ind = lambda mask: jnp.where(mask, 1.0, 0.0).astype(BF16)
    upper_bd = ind((r_i <= c_i) & same_chunk)
    lower = ind(r_i >= c_i)
    lower_bd = ind((r_i >= c_i) & same_chunk)

    g_rows = lax.dot_general(wfit_ref[...], hb, NT_DIMS, preferred_element_type=F32)
    gf = g_rows[0:16] + bfc_ref[...]
    gi = g_rows[16:32] + bic_ref[...]
    lf3 = jnp.concatenate(_split3(_log_sigmoid(gf)), axis=0)
    y3 = jnp.dot(jnp.concatenate([seq_lanes(lf3, i, tg) for i in range(nb)], axis=0), upper_bd,
                 preferred_element_type=F32)
    row = lax.broadcasted_iota(jnp.int32, (16, tg), 0)
    for i in range(nb):
        y_bd = y3[48 * i:48 * i + 16] + y3[48 * i + 16:48 * i + 32] + y3[48 * i + 32:48 * i + 48]
        rows_ref[i] = jnp.where(row < 4, y_bd, seq_lanes(gi, i, tg) - y_bd)[0:8]

    g_cols = jnp.dot(hb, wfi_ref[...], preferred_element_type=F32)
    gfc = g_cols[:, 0:128] + bfr_ref[...]
    gic = g_cols[:, 128:256] + bir_ref[...]
    lfc3 = jnp.concatenate(_split3(_log_sigmoid(gfc)), axis=1)
    lfc3 = jnp.concatenate([seq_rows(lfc3, i) for i in range(nb)], axis=1)
    y_bd_c = jnp.dot(lower_bd, lfc3, preferred_element_type=F32)
    y_full_c = jnp.dot(lower, lfc3, preferred_element_type=F32)
    sum3 = lambda a: a[:, 0:128] + a[:, 128:256] + a[:, 256:384]
    col = lax.broadcasted_iota(jnp.int32, (tg, 128), 1)
    parts = []
    for i in range(nb):
        yc = sum3(seq_lanes(y_bd_c, i, 384))
        cols_ref[i] = jnp.where(col < 4, yc, seq_rows(gic, i) - yc)
        cum = sum3(seq_lanes(y_full_c, i, 384)) + carry_ref[i, 0:1, :]
        carry_ref[i] = jnp.broadcast_to(cum[tg - 1:tg, :], carry_ref.shape[1:])
        parts.append(jnp.concatenate(_split3(cum * LOG2E), axis=1))
    sr = lax.broadcasted_iota(jnp.int32, (3 * 128, 128), 0)
    sc = lax.broadcasted_iota(jnp.int32, (3 * 128, 128), 1)
    scatter = ind((sr % 128 == 8 + sc // 3) & (sr // 128 == sc % 3) & (sc < 3 * H_B))
    kb = jnp.dot(jnp.concatenate(parts, axis=0), scatter, preferred_element_type=F32).astype(BF16)
    for i in range(nb):
        kb_ref[i] = seq_rows(kb, i)


def _gates(hb3, gw):
    bsz, t, _ = hb3.shape
    tg = min(512, t)
    nb = math.gcd(GATES_SEQS_PER_STEP, bsz)
    full = lambda shape: pl.BlockSpec(shape, lambda b, i: tuple(0 for _ in shape))
    return pl.pallas_call(
        functools.partial(_gates_kernel, tg=tg, chunk=min(MLSTM_CHUNK, t), nb=nb),
        grid=(bsz // nb, t // tg),
        in_specs=[pl.BlockSpec((nb, tg, D_MODEL), lambda b, i: (b, i, 0)),
                  full((D_MODEL, 256)), full((32, D_MODEL)),
                  full((16, 1)), full((16, 1)), full((1, 128)), full((1, 128))],
        out_specs=[pl.BlockSpec((nb, 8, tg), lambda b, i: (b, 0, i)),
                   pl.BlockSpec((nb, tg, 128), lambda b, i: (b, i, 0)),
                   pl.BlockSpec((nb, tg, 128), lambda b, i: (b, i, 0))],
        out_shape=[jax.ShapeDtypeStruct((bsz, 8, t), F32), jax.ShapeDtypeStruct((bsz, t, 128), F32),
                   jax.ShapeDtypeStruct((bsz, t, 128), BF16)],
        scratch_shapes=[pltpu.VMEM((nb, 8, 128), F32)],
        compiler_params=_cparams(("parallel", "arbitrary")), name="gates",
    )(hb3, *gw)


def _mlstm_kernel(c_ref, v_ref, o_ref, rows_ref, cols_ref, cw_ref, cb_ref, wq_ref, wk_ref, ng_ref,
                  y_ref, cbuf, c_state, m_state, *, chunk):
    L = chunk
    t = pl.program_id(1)

    @pl.when(t == 0)
    def _():
        cbuf[0:8, :] = jnp.zeros((8, D_MODEL), F32)
        c_state[...] = jnp.zeros_like(c_state)
        m_state[...] = jnp.zeros_like(m_state)

    x = c_ref[0].astype(F32)
    cbuf[8:8 + L, :] = x
    conv = cb_ref[...]
    for j in range(CONV_K):
        conv = conv + cbuf[8 - (CONV_K - 1) + j:8 - (CONV_K - 1) + j + L, :] * cw_ref[j:j + 1, :]
    cbuf[0:8, :] = x[L - 8:L, :]
    ub = _silu(conv).astype(BF16)

    nc = L // 128
    lanes = lambda c: slice(c * 128, (c + 1) * 128)
    r_i = lax.broadcasted_iota(jnp.int32, (L, 128), 0)
    c_i = lax.broadcasted_iota(jnp.int32, (L, 128), 1)
    causal = [c_i + c * 128 <= r_i for c in range(nc)]
    ones_blk = jnp.ones((L, 128), BF16)
    for h in range(H_A):
        hs = slice(h * DH_A, (h + 1) * DH_A)
        uh = ub[:, hs]
        q = jnp.dot(uh, wq_ref[h], preferred_element_type=F32)
        k = jnp.dot(uh, wk_ref[h], preferred_element_type=F32) * (DH_A ** -0.5)
        qb = q.astype(BF16)
        kb = k.astype(BF16)
        v_aug = jnp.concatenate([v_ref[0, :, hs], ones_blk], axis=1)
        b_rep = jnp.broadcast_to(cols_ref[0, :, h:h + 1], (L, 128))
        c_rep = jnp.broadcast_to(cols_ref[0, :, H_A + h:H_A + h + 1], (L, 128))
        b_row = rows_ref[0, h:h + 1, :]
        c_row = rows_ref[0, H_A + h:H_A + h + 1, :]
        m_prev = m_state[h, 0:1, :]

        dm = [jnp.where(causal[c], b_rep + c_row[:, lanes(c)], -jnp.inf) for c in range(nc)]
        dmax = dm[0]
        for c in range(1, nc):
            dmax = jnp.maximum(dmax, dm[c])
        inter = b_rep + m_prev
        m_t = jnp.maximum(jnp.max(dmax, axis=1, keepdims=True), inter)
        qk = lax.dot_general(qb, kb, NT_DIMS, preferred_element_type=F32)
        s = jnp.concatenate([(qk[:, lanes(c)] * jnp.exp(dm[c] - m_t)).astype(BF16) for c in range(nc)], axis=1)
        decay = jnp.exp(inter - m_t)
        c_old = c_state[h]
        inter_acc = jnp.dot(qb, c_old.astype(BF16), preferred_element_type=F32)
        acc = jnp.dot(s, v_aug, preferred_element_type=F32) + jnp.tile(decay, (1, DH_A // 128 + 1)) * inter_acc
        den = acc[:, DH_A:]
        inv = 1.0 / jnp.maximum(jnp.abs(den), jnp.exp(-m_t))
        hh = acc[:, :DH_A] * jnp.tile(inv, (1, DH_A // 128))

        g = b_row[:, L - 1:L]
        a_rep = g + c_rep
        m_new = jnp.maximum(g + m_prev, jnp.max(a_rep, axis=0, keepdims=True))
        kw = (k * jnp.tile(jnp.exp(a_rep - m_new), (1, DH_A // 128))).astype(BF16)
        sc = jnp.exp(g + m_prev - m_new)
        c_state[h] = (jnp.tile(sc, (1, DH_A // 128 + 1)) * c_old
                      + lax.dot_general(kw, v_aug, TN_DIMS, preferred_element_type=F32))
        m_state[h] = jnp.broadcast_to(m_new, m_state.shape[1:])

        mu = jnp.mean(hh, axis=1, keepdims=True)
        hc = hh - mu
        var = jnp.mean(hc * hc, axis=1, keepdims=True)
        hn = hc * lax.rsqrt(var + LN_EPS) * ng_ref[:, hs]
        og = _sigmoid(o_ref[0, :, hs])
        y_ref[0, :, hs] = (og.astype(F32) * hn).astype(BF16)


def _mlstm(p3, rows, cols, conv_w, conv_b, wq, wk, norm_g):
    bsz, t, _ = p3.shape
    L = min(MLSTM_CHUNK, t)
    colblk = lambda c: pl.BlockSpec((1, L, 1024), lambda b, i, c=c: (b, i, c))
    full = lambda shape: pl.BlockSpec(shape, lambda b, i: tuple(0 for _ in shape))
    return pl.pallas_call(
        functools.partial(_mlstm_kernel, chunk=L),
        grid=(bsz, t // L),
        in_specs=[colblk(COL_CA), colblk(COL_VA), colblk(COL_OA),
                  pl.BlockSpec((1, 8, L), lambda b, i: (b, 0, i)),
                  pl.BlockSpec((1, L, 128), lambda b, i: (b, i, 0)),
                  full((CONV_K, 1024)), full((1, 1024)), full((H_A, DH_A, DH_A)), full((H_A, DH_A, DH_A)),
                  full((1, 1024))],
        out_specs=pl.BlockSpec((1, L, 1024), lambda b, i: (b, i, 0)),
        out_shape=jax.ShapeDtypeStruct((bsz, t, 1024), BF16),
        scratch_shapes=[pltpu.VMEM((8 + L, 1024), F32), pltpu.VMEM((H_A, DH_A, DH_A + 128), F32),
                        pltpu.VMEM((H_A, 8, 128), F32)],
        compiler_params=_cparams(("parallel", "arbitrary")), name="mlstm",
    )(p3, p3, p3, rows, cols, conv_w, conv_b.reshape(1, -1), wq, wk, norm_g.reshape(1, -1))


def _flash_kernel(*refs, tq, tk, dk, heads, has_bias):
    if has_bias:
        q_ref, k_ref, kb_ref, v_ref, o_ref, m_sc, l_sc, acc_sc = refs
    else:
        q_ref, k_ref, v_ref, o_ref, m_sc, l_sc, acc_sc = refs
        kb_ref = None
    i = pl.program_id(2)
    qs = []
    for g in range(heads):
        q = q_ref[0, :, g * dk:(g + 1) * dk]
        if has_bias:
            lane = lax.broadcasted_iota(jnp.int32, (tq, 128), 1) - 3 * (pl.program_id(1) * heads + g)
            q = jnp.concatenate([q, jnp.where((lane >= 0) & (lane < 3), -1.0, 0.0).astype(BF16)], axis=1)
        qs.append(q)

    def scores(g, key0, nk, r0, nr, mask_off):
        ks = pl.ds(key0, nk)
        kj = k_ref[0, ks, g * dk:(g + 1) * dk]
        if has_bias:
            kj = jnp.concatenate([kj, kb_ref[0, ks, :]], axis=1)
        s = lax.dot_general(qs[g][r0:r0 + nr], kj, NT_DIMS, preferred_element_type=F32)
        if mask_off is not None:
            r_i = r0 + lax.broadcasted_iota(jnp.int32, (nr, nk), 0)
            c_i = mask_off + lax.broadcasted_iota(jnp.int32, (nr, nk), 1)
            s = jnp.where(c_i <= r_i, s, -jnp.inf)
        return s

    def update(g, s, key0, nk, r0, nr, first):
        rs = slice(r0, r0 + nr)
        vj = v_ref[0, pl.ds(key0, nk), g * 128:(g + 1) * 128]
        s_max = jnp.max(s, axis=1, keepdims=True)
        if first:
            m_new = jnp.broadcast_to(s_max, (nr, 128))
            l_part = jnp.zeros((nr, 128), F32)
        else:
            m_prev = m_sc[g, rs]
            m_new = jnp.maximum(m_prev, s_max)
            alpha = jnp.exp2(m_prev - m_new)
            l_part = alpha * l_sc[g, rs]
        ps = []
        for c in range(nk // 128):
            p_c = jnp.exp2(s[:, c * 128:(c + 1) * 128] - m_new)
            l_part = l_part + p_c
            ps.append(p_c.astype(BF16))
        pv = jnp.dot(jnp.concatenate(ps, axis=1), vj, preferred_element_type=F32)
        acc_sc[g, rs] = pv if first else alpha * acc_sc[g, rs] + pv
        l_sc[g, rs] = l_part
        m_sc[g, rs] = m_new

    def finish(g):
        l = jnp.sum(l_sc[g], axis=1, keepdims=True)
        o_ref[0, :, g * 128:(g + 1) * 128] = (acc_sc[g] / l).astype(o_ref.dtype)

    def sweep(key0, nk, r0, nr, mask_off, first=False, last=False):
        s_cur = scores(0, key0, nk, r0, nr, mask_off)
        for g in range(heads):
            s_next = scores(g + 1, key0, nk, r0, nr, mask_off) if g + 1 < heads else None
            update(g, s_cur, key0, nk, r0, nr, first)
            if last:
                finish(g)
            s_cur = s_next

    nsub = tq // tk

    def body(j, carry):
        sweep(pl.multiple_of(j * tk, tk), tk, 0, tq, None)
        return carry

    def diag(d, **kw):
        sweep(pl.multiple_of(i * tq + d * tk, tk), tk, d * tk, tq - d * tk, d * tk, **kw)

    diag(0, first=True)
    lax.fori_loop(0, i * nsub, body, 0)
    for d in range(1, nsub):
        diag(d, last=(d == nsub - 1))
    if nsub == 1:
        for g in range(heads):
            finish(g)


def _flash(q_arr, k_arr, kb_arr, v_arr, *, n_heads, dk, q_blk, k_blk, v_blk):
    bsz, t, _ = q_arr.shape
    tq = min(FLASH_TQ, t)
    tk = min(FLASH_TK, t)
    g = FLASH_HEADS_PER_STEP
    hp = n_heads // g
    in_specs = [pl.BlockSpec((1, tq, g * dk), lambda b, h, i: (b, i, q_blk // g + h)),
                pl.BlockSpec((1, t, g * dk), lambda b, h, i: (b, 0, k_blk // g + h))]
    args = [q_arr, k_arr]
    if kb_arr is not None:
        in_specs.append(pl.BlockSpec((1, t, 128), lambda b, h, i: (b, 0, 0)))
        args.append(kb_arr)
    in_specs.append(pl.BlockSpec((1, t, g * 128), lambda b, h, i: (b, 0, v_blk // g + h)))
    args.append(v_arr)
    return pl.pallas_call(
        functools.partial(_flash_kernel, tq=tq, tk=tk, dk=dk, heads=g, has_bias=kb_arr is not None),
        grid=(bsz, hp, t // tq),
        in_specs=in_specs,
        out_specs=pl.BlockSpec((1, tq, g * 128), lambda b, h, i: (b, i, h)),
        out_shape=jax.ShapeDtypeStruct((bsz, t, n_heads * 128), BF16),
        scratch_shapes=[pltpu.VMEM((g, tq, 128), F32), pltpu.VMEM((g, tq, 128), F32), pltpu.VMEM((g, tq, 128), F32)],
        compiler_params=_cparams(("parallel", "parallel", "arbitrary")),
        name="flash_fox" if kb_arr is not None else "flash_mla",
    )(*args)


def _mla_prep_kernel(c_ref, tab_ref, gq_ref, gkv_ref, wq_ref, wkv_ref, q_out, k_out, v_out):
    c = c_ref[...].astype(F32)
    ckv = c[:, 0:KV_RANK]
    cq = c[:, KV_RANK:KV_RANK + Q_RANK]
    kr = c[:, KV_RANK + Q_RANK:]

    def rms(x, g):
        ms = jnp.mean(x * x, axis=-1, keepdims=True)
        return (x * lax.rsqrt(ms + RMS_EPS) * g).astype(BF16)

    qf = jnp.dot(rms(cq, gq_ref[...]), wq_ref[...], preferred_element_type=F32)
    kvf = jnp.dot(rms(ckv, gkv_ref[...]), wkv_ref[...], preferred_element_type=F32)
    tab = tab_ref[...]

    def rope(xx):
        pr = xx * tab
        return pr + pltpu.roll(pr, DH_ROPE, 1)

    lane = lax.broadcasted_iota(jnp.int32, kr.shape, 1)
    k_rope = jnp.where(lane < DH_ROPE, rope(kr), 0.0).astype(BF16)
    for h in range(H_C):
        lo = h * DK_C
        q_out[:, lo:lo + 128] = qf[:, lo:lo + 128].astype(BF16)
        q_out[:, lo + 128:lo + 256] = rope(qf[:, lo + 128:lo + 256]).astype(BF16)
        k_out[:, lo:lo + 128] = kvf[:, lo:lo + 128].astype(BF16)
        k_out[:, lo + 128:lo + 256] = k_rope
        v_out[:, h * DH_V:(h + 1) * DH_V] = kvf[:, lo + 128:lo + 256].astype(BF16)


def _mla_prep(p2, tab, g_cq, g_ckv, wq, wkv):
    n = p2.shape[0]
    tm = min(512, n)
    full = lambda shape: pl.BlockSpec(shape, lambda i: tuple(0 for _ in shape))
    return pl.pallas_call(
        _mla_prep_kernel, grid=(n // tm,),
        in_specs=[pl.BlockSpec((tm, LOWRANK_W), lambda i: (i, LOWRANK_OFF // LOWRANK_W)),
                  pl.BlockSpec((tm, 128), lambda i: (i, 0)),
                  full((1, Q_RANK)), full((1, KV_RANK)), full((Q_RANK, H_C * DK_C)), full((KV_RANK, H_C * DK_C))],
        out_specs=[pl.BlockSpec((tm, H_C * DK_C), lambda i: (i, 0)),
                   pl.BlockSpec((tm, H_C * DK_C), lambda i: (i, 0)),
                   pl.BlockSpec((tm, H_C * DH_V), lambda i: (i, 0))],
        out_shape=[jax.ShapeDtypeStruct((n, H_C * DK_C), BF16), jax.ShapeDtypeStruct((n, H_C * DK_C), BF16),
                   jax.ShapeDtypeStruct((n, H_C * DH_V), BF16)],
        compiler_params=_cparams(("parallel",)), name="mla_prep",
    )(p2, tab, g_cq.reshape(1, -1), g_ckv.reshape(1, -1), wq, wkv)


def _merge_kernel(ya_ref, yb_ref, yc_ref, za_ref, zb_ref, zc_ref, ga_ref, gb_ref, gc_ref, h_ref,
                  pa_ref, pb_ref, pc_ref, wo_ref, lg_ref, lb_ref, hf_ref, hb_ref, *, alpha):
    def front(rs):
        def branch(y_ref, z_ref, g_ref, p_ref):
            hz = z_ref[rs, :] * 0.5
            y = y_ref[rs, :] * (hz + hz * jnp.tanh(hz))
            gate = 0.5 + 0.5 * jnp.tanh(g_ref[rs, :] * 0.5)
            return gate.astype(F32) * jnp.dot(y, p_ref[...], preferred_element_type=F32)

        return (branch(ya_ref, za_ref, ga_ref, pa_ref) + branch(yb_ref, zb_ref, gb_ref, pb_ref)
                + branch(yc_ref, zc_ref, gc_ref, pc_ref)).astype(BF16)

    def back(rs, merged):
        out = jnp.dot(merged, wo_ref[...], preferred_element_type=F32)
        r = alpha * h_ref[rs, :] + out
        mu = jnp.mean(r, axis=-1, keepdims=True)
        rc = r - mu
        var = jnp.mean(rc * rc, axis=-1, keepdims=True)
        y = rc * lax.rsqrt(var + LN_EPS) * lg_ref[...] + lb_ref[...]
        hf_ref[rs, :] = y
        hb_ref[rs, :] = y.astype(BF16)

    tm = h_ref.shape[0]
    strips = [slice(r, r + tm // MERGE_STRIPS) for r in range(0, tm, tm // MERGE_STRIPS)]
    cur = front(strips[0])
    for n, rs in enumerate(strips):
        nxt = front(strips[n + 1]) if n + 1 < len(strips) else None
        back(rs, cur)
        cur = nxt


def _merge(ya, yb, yc, p2, hf, pa, pb, pc, wo, ln_g, ln_b, alpha):
    n = hf.shape[0]
    tm = min(512, n)
    row = pl.BlockSpec((tm, D_MODEL), lambda i: (i, 0))
    colblk = lambda c: pl.BlockSpec((tm, 1024), lambda i, c=c: (i, c))
    wfull = pl.BlockSpec((1024, D_MODEL), lambda i: (0, 0), pipeline_mode=pl.Buffered(1))
    vec = pl.BlockSpec((1, D_MODEL), lambda i: (0, 0))
    return pl.pallas_call(
        functools.partial(_merge_kernel, alpha=alpha), grid=(n // tm,),
        in_specs=[row, row, row, colblk(COL_ZA), colblk(COL_ZB), colblk(COL_ZC),
                  colblk(COL_GA), colblk(COL_GB), colblk(COL_GC), row,
                  wfull, wfull, wfull, wfull, vec, vec],
        out_specs=[row, row],
        out_shape=[jax.ShapeDtypeStruct((n, D_MODEL), F32), jax.ShapeDtypeStruct((n, D_MODEL), BF16)],
        compiler_params=_cparams(("parallel",)), name="merge",
    )(ya, yb, yc, p2, p2, p2, p2, p2, p2, hf, pa, pb, pc, wo, ln_g.reshape(1, -1), ln_b.reshape(1, -1))


def _rot_cols(w):
    half = w.shape[-1] // 2
    return jnp.concatenate([-w[..., half:], w[..., :half]], axis=-1)


def _layer_weights(w_in, b_ia, b_fa, b_fb, w_uq, w_ukv):
    depth = w_in.shape[0]
    idx = []
    acc = 0
    for s in IN_SPLITS[:-1]:
        acc += s
        idx.append(acc)
    (w_ca, w_va, w_oa, w_ia, w_fa, w_za, w_qb, w_kb, w_vb, w_fb, w_zb,
     w_cq, w_ckv, w_kr, w_zc, w_ga, w_gb, w_gc) = jnp.split(w_in, idx, axis=-1)
    w_qb = w_qb * (DH_B ** -0.5 * LOG2E)
    w_big = jnp.concatenate([w_ca, w_va, w_oa, w_za, w_qb, w_kb, w_vb, w_zb, w_zc, w_ga, w_gb, w_gc,
                             w_ckv, w_cq, w_kr, _rot_cols(w_kr)], axis=-1).astype(BF16)
    zpad = lambda k: jnp.zeros((depth, D_MODEL, k), F32)
    w_f = jnp.concatenate([w_fa, w_fa, w_fb, zpad(112)], axis=-1)
    w_i = jnp.concatenate([zpad(4), w_ia, zpad(120)], axis=-1)
    b_f = jnp.concatenate([b_fa, b_fa, b_fb, jnp.zeros((depth, 112), F32)], axis=-1)
    b_i = jnp.concatenate([jnp.zeros((depth, 4), F32), b_ia, jnp.zeros((depth, 120), F32)], axis=-1)
    gw = (jnp.concatenate([w_f, w_i], axis=-1).astype(BF16),
          jnp.swapaxes(jnp.concatenate([w_f[..., :16], w_i[..., :16]], axis=-1), -1, -2).astype(BF16),
          b_f[:, :16, None], b_i[:, :16, None], b_f[:, None, :], b_i[:, None, :])
    wq3 = w_uq.reshape(depth, Q_RANK, H_C, DH_NOPE + DH_ROPE) * ((DH_NOPE + DH_ROPE) ** -0.5 * LOG2E)
    wq_rope = wq3[..., DH_NOPE:]
    wq = jnp.concatenate([wq3, _rot_cols(wq_rope)], axis=-1).reshape(depth, Q_RANK, H_C * DK_C).astype(BF16)
    return w_big, gw, wq, w_ukv.astype(BF16)


def kernel(x, positions, ln_in_g, ln_in_b, w_in, conv_w, conv_b, w_qa, w_ka, b_ia, b_fa, norm_ga, b_fb,
           g_cq, w_uq, g_ckv, w_ukv, p_a, p_b, p_c, w_out, ln_g, ln_b):
    bsz, t, _ = x.shape
    n = bsz * t
    depth = w_in.shape[0]
    alpha = (2.0 * depth) ** 0.25

    w_big, gw, wq, wkv = _layer_weights(w_in, b_ia, b_fa, b_fb, w_uq, w_ukv)
    wqa, wka, pa, pb, pc, wo = (w.astype(BF16) for w in (w_qa, w_ka, p_a, p_b, p_c, w_out))
    tab = _rope_table(positions)
    hf, hb = _input_ln(x.reshape(n, D_MODEL), ln_in_g, ln_in_b)
    for l in range(depth):
        p2 = _in_proj(hb, w_big[l])
        p3 = p2.reshape(bsz, t, N_BIG)
        rows, cols, kb = _gates(hb.reshape(bsz, t, D_MODEL), tuple(w[l] for w in gw))
        ya = _mlstm(p3, rows, cols, conv_w[l], conv_b[l], wqa[l], wka[l], norm_ga[l])
        yb = _flash(p3, p3, kb, p3, n_heads=H_B, dk=DH_B, q_blk=COL_QB * 8, k_blk=COL_KB * 8, v_blk=COL_VB * 8)
        qc, kc, vc = _mla_prep(p2, tab, g_cq[l], g_ckv[l], wq[l], wkv[l])
        yc = _flash(qc.reshape(bsz, t, -1), kc.reshape(bsz, t, -1), None, vc.reshape(bsz, t, -1),
                    n_heads=H_C, dk=DK_C, q_blk=0, k_blk=0, v_blk=0)
        hf, hb = _merge(ya.reshape(n, -1), yb.reshape(n, -1), yc.reshape(n, -1), p2, hf,
                        pa[l], pb[l], pc[l], wo[l], ln_g[l], ln_b[l], alpha)
    return hf.reshape(bsz, t, D_MODEL)
```

```python
import functools
import math

import jax
import jax.numpy as jnp
from jax import lax
from jax.experimental import pallas as pl
from jax.experimental.pallas import tpu as pltpu

F32 = jnp.float32
BF16 = jnp.bfloat16

D_MODEL = 1024
H_A, DH_A, CONV_K = 4, 256, 4
H_B, DH_B = 8, 128
H_C, DH_NOPE, DH_ROPE, DH_V = 8, 128, 64, 128
Q_RANK, KV_RANK = 384, 256
ROPE_BASE = 10000.0
LN_EPS = 1e-5
RMS_EPS = 1e-6
IN_SPLITS = (1024, 1024, 1024, 4, 4, 1024,
             1024, 1024, 1024, 8, 1024,
             Q_RANK, KV_RANK, DH_ROPE, 1024,
             1024, 1024, 1024)

COL_CA, COL_VA, COL_OA, COL_ZA, COL_QB, COL_KB, COL_VB, COL_ZB, COL_ZC, COL_GA, COL_GB, COL_GC = range(12)
LOWRANK_OFF = 12 * 1024
LOWRANK_W = 768
N_BIG = LOWRANK_OFF + LOWRANK_W
DK_C = 256

MLSTM_CHUNK = 256
GATES_SEQS_PER_STEP = 4
FLASH_TQ = 1024
FLASH_TK = 512
FLASH_HEADS_PER_STEP = 4
MERGE_STRIPS = 2
VMEM_LIMIT = 56 * 1024 * 1024
LOG2E = math.log2(math.e)

NT_DIMS = (((1,), (1,)), ((), ()))
TN_DIMS = (((0,), (0,)), ((), ()))


def _cparams(sem):
    return pltpu.CompilerParams(dimension_semantics=sem, vmem_limit_bytes=VMEM_LIMIT)


def _sigmoid(x):
    return 0.5 + 0.5 * jnp.tanh(0.5 * x)


def _silu(x):
    hx = 0.5 * x
    return hx + hx * jnp.tanh(hx)


def _log_sigmoid(x):
    return jnp.minimum(x, 0.0) - jnp.log1p(jnp.exp(-jnp.abs(x)))


def _split3(x):
    hi = x.astype(BF16)
    r = x - hi.astype(F32)
    mid = r.astype(BF16)
    lo = (r - mid.astype(F32)).astype(BF16)
    return hi, mid, lo


def _ln_kernel(x_ref, g_ref, b_ref, hf_ref, hb_ref):
    x = x_ref[...]
    mu = jnp.mean(x, axis=-1, keepdims=True)
    xc = x - mu
    var = jnp.mean(xc * xc, axis=-1, keepdims=True)
    y = xc * lax.rsqrt(var + LN_EPS) * g_ref[...] + b_ref[...]
    hf_ref[...] = y
    hb_ref[...] = y.astype(BF16)


def _input_ln(x2, g, b):
    n = x2.shape[0]
    tm = min(512, n)
    row = pl.BlockSpec((tm, D_MODEL), lambda i: (i, 0))
    vec = pl.BlockSpec((1, D_MODEL), lambda i: (0, 0))
    return pl.pallas_call(
        _ln_kernel, grid=(n // tm,), in_specs=[row, vec, vec], out_specs=[row, row],
        out_shape=[jax.ShapeDtypeStruct((n, D_MODEL), F32), jax.ShapeDtypeStruct((n, D_MODEL), BF16)],
        compiler_params=_cparams(("parallel",)), name="input_ln",
    )(x2, g.reshape(1, -1), b.reshape(1, -1))


def _rope_kernel(pos_ref, freq_ref, tab_ref):
    ang = pos_ref[...].astype(F32) * freq_ref[...]
    lane = lax.broadcasted_iota(jnp.int32, ang.shape, 1)
    tab_ref[...] = jnp.where(lane < DH_ROPE, jnp.cos(ang), jnp.sin(ang))


def _rope_table(positions):
    n = positions.size
    tm = min(1024, n)
    inv_freq = ROPE_BASE ** (-jnp.arange(0, DH_ROPE, 2, dtype=F32) / DH_ROPE)
    freq = jnp.tile(inv_freq, 4).reshape(1, 128)
    return pl.pallas_call(
        _rope_kernel, grid=(n // tm,),
        in_specs=[pl.BlockSpec((tm, 1), lambda i: (i, 0)), pl.BlockSpec((1, 128), lambda i: (0, 0))],
        out_specs=pl.BlockSpec((tm, 128), lambda i: (i, 0)),
        out_shape=jax.ShapeDtypeStruct((n, 128), F32),
        compiler_params=_cparams(("parallel",)), name="rope_table",
    )(positions.reshape(n, 1), freq)


def _matmul_kernel(x_ref, w_ref, o_ref):
    o_ref[...] = jnp.dot(x_ref[...], w_ref[...], preferred_element_type=F32).astype(o_ref.dtype)


def _in_proj(hb, w_big):
    n = hb.shape[0]
    tm = min(1024, n)
    tn = N_BIG // 3
    return pl.pallas_call(
        _matmul_kernel, grid=(n // tm, N_BIG // tn),
        in_specs=[pl.BlockSpec((tm, D_MODEL), lambda i, j: (i, 0)),
                  pl.BlockSpec((D_MODEL, tn), lambda i, j: (0, j))],
        out_specs=pl.BlockSpec((tm, tn), lambda i, j: (i, j)),
        out_shape=jax.ShapeDtypeStruct((n, N_BIG), BF16),
        compiler_params=_cparams(("parallel", "arbitrary")), name="in_proj",
    )(hb, w_big)


def _gates_kernel(h_ref, wfi_ref, wfit_ref, bfc_ref, bic_ref, bfr_ref, bir_ref,
                  rows_ref, cols_ref, kb_ref, carry_ref, *, tg, chunk, nb):
    t = pl.program_id(1)

    @pl.when(t == 0)
    def _():
        carry_ref[...] = jnp.zeros_like(carry_ref)

    seq_rows = lambda a, i: a[i * tg:(i + 1) * tg]
    seq_lanes = lambda a, i, w: a[:, i * w:(i + 1) * w]
    hb = h_ref[...].reshape(nb * tg, D_MODEL)
    r_i = lax.broadcasted_iota(jnp.int32, (tg, tg), 0)
    c_i = lax.broadcasted_iota(jnp.int32, (tg, tg), 1)
    same_chunk = (r_i // chunk) == (c_i // chunk)
    ind = lambda mask: jnp.where(mask, 1.0, 0.0).astype(BF16)
    upper_bd = ind((r_i <= c_i) & same_chunk)
    lower = ind(r_i >= c_i)
    lower_bd = ind((r_i >= c_i) & same_chunk)

    g_rows = lax.dot_general(wfit_ref[...], hb, NT_DIMS, preferred_element_type=F32)
    gf = g_rows[0:16] + bfc_ref[...]
    gi = g_rows[16:32] + bic_ref[...]
    lf3 = jnp.concatenate(_split3(_log_sigmoid(gf)), axis=0)
    y3 = jnp.dot(jnp.concatenate([seq_lanes(lf3, i, tg) for i in range(nb)], axis=0), upper_bd,
                 preferred_element_type=F32)
    row = lax.broadcasted_iota(jnp.int32, (16, tg), 0)
    for i in range(nb):
        y_bd = y3[48 * i:48 * i + 16] + y3[48 * i + 16:48 * i + 32] + y3[48 * i + 32:48 * i + 48]
        rows_ref[i] = jnp.where(row < 4, y_bd, seq_lanes(gi, i, tg) - y_bd)[0:8]

    g_cols = jnp.dot(hb, wfi_ref[...], preferred_element_type=F32)
    gfc = g_cols[:, 0:128] + bfr_ref[...]
    gic = g_cols[:, 128:256] + bir_ref[...]
    lfc3 = jnp.concatenate(_split3(_log_sigmoid(gfc)), axis=1)
    lfc3 = jnp.concatenate([seq_rows(lfc3, i) for i in range(nb)], axis=1)
    y_bd_c = jnp.dot(lower_bd, lfc3, preferred_element_type=F32)
    y_full_c = jnp.dot(lower, lfc3, preferred_element_type=F32)
    sum3 = lambda a: a[:, 0:128] + a[:, 128:256] + a[:, 256:384]
    col = lax.broadcasted_iota(jnp.int32, (tg, 128), 1)
    parts = []
    for i in range(nb):
        yc = sum3(seq_lanes(y_bd_c, i, 384))
        cols_ref[i] = jnp.where(col < 4, yc, seq_rows(gic, i) - yc)
        cum = sum3(seq_lanes(y_full_c, i, 384)) + carry_ref[i, 0:1, :]
        carry_ref[i] = jnp.broadcast_to(cum[tg - 1:tg, :], carry_ref.shape[1:])
        parts.append(jnp.concatenate(_split3(cum * LOG2E), axis=1))
    sr = lax.broadcasted_iota(jnp.int32, (3 * 128, 128), 0)
    sc = lax.broadcasted_iota(jnp.int32, (3 * 128, 128), 1)
    scatter = ind((sr % 128 == 8 + sc // 3) & (sr // 128 == sc % 3) & (sc < 3 * H_B))
    kb = jnp.dot(jnp.concatenate(parts, axis=0), scatter, preferred_element_type=F32).astype(BF16)
    for i in range(nb):
        kb_ref[i] = seq_rows(kb, i)


def _gates(hb3, gw):
    bsz, t, _ = hb3.shape
    tg = min(512, t)
    nb = math.gcd(GATES_SEQS_PER_STEP, bsz)
    full = lambda shape: pl.BlockSpec(shape, lambda b, i: tuple(0 for _ in shape))
    return pl.pallas_call(
        functools.partial(_gates_kernel, tg=tg, chunk=min(MLSTM_CHUNK, t), nb=nb),
        grid=(bsz // nb, t // tg),
        in_specs=[pl.BlockSpec((nb, tg, D_MODEL), lambda b, i: (b, i, 0)),
                  full((D_MODEL, 256)), full((32, D_MODEL)),
                  full((16, 1)), full((16, 1)), full((1, 128)), full((1, 128))],
        out_specs=[pl.BlockSpec((nb, 8, tg), lambda b, i: (b, 0, i)),
                   pl.BlockSpec((nb, tg, 128), lambda b, i: (b, i, 0)),
                   pl.BlockSpec((nb, tg, 128), lambda b, i: (b, i, 0))],
        out_shape=[jax.ShapeDtypeStruct((bsz, 8, t), F32), jax.ShapeDtypeStruct((bsz, t, 128), F32),
                   jax.ShapeDtypeStruct((bsz, t, 128), BF16)],
        scratch_shapes=[pltpu.VMEM((nb, 8, 128), F32)],
        compiler_params=_cparams(("parallel", "arbitrary")), name="gates",
    )(hb3, *gw)


def _mlstm_kernel(c_ref, v_ref, o_ref, rows_ref, cols_ref, cw_ref, cb_ref, wq_ref, wk_ref, ng_ref,
                  y_ref, cbuf, c_state, m_state, *, chunk):
    L = chunk
    t = pl.program_id(1)

    @pl.when(t == 0)
    def _():
        cbuf[0:8, :] = jnp.zeros((8, D_MODEL), F32)
        c_state[...] = jnp.zeros_like(c_state)
        m_state[...] = jnp.zeros_like(m_state)

    x = c_ref[0].astype(F32)
    cbuf[8:8 + L, :] = x
    conv = cb_ref[...]
    for j in range(CONV_K):
        conv = conv + cbuf[8 - (CONV_K - 1) + j:8 - (CONV_K - 1) + j + L, :] * cw_ref[j:j + 1, :]
    cbuf[0:8, :] = x[L - 8:L, :]
    ub = _silu(conv).astype(BF16)

    nc = L // 128
    lanes = lambda c: slice(c * 128, (c + 1) * 128)
    r_i = lax.broadcasted_iota(jnp.int32, (L, 128), 0)
    c_i = lax.broadcasted_iota(jnp.int32, (L, 128), 1)
    causal = [c_i + c * 128 <= r_i for c in range(nc)]
    ones_blk = jnp.ones((L, 128), BF16)
    for h in range(H_A):
        hs = slice(h * DH_A, (h + 1) * DH_A)
        uh = ub[:, hs]
        q = jnp.dot(uh, wq_ref[h], preferred_element_type=F32)
        k = jnp.dot(uh, wk_ref[h], preferred_element_type=F32) * (DH_A ** -0.5)
        qb = q.astype(BF16)
        kb = k.astype(BF16)
        v_aug = jnp.concatenate([v_ref[0, :, hs], ones_blk], axis=1)
        b_rep = jnp.broadcast_to(cols_ref[0, :, h:h + 1], (L, 128))
        c_rep = jnp.broadcast_to(cols_ref[0, :, H_A + h:H_A + h + 1], (L, 128))
        b_row = rows_ref[0, h:h + 1, :]
        c_row = rows_ref[0, H_A + h:H_A + h + 1, :]
        m_prev = m_state[h, 0:1, :]

        dm = [jnp.where(causal[c], b_rep + c_row[:, lanes(c)], -jnp.inf) for c in range(nc)]
        dmax = dm[0]
        for c in range(1, nc):
            dmax = jnp.maximum(dmax, dm[c])
        inter = b_rep + m_prev
        m_t = jnp.maximum(jnp.max(dmax, axis=1, keepdims=True), inter)
        qk = lax.dot_general(qb, kb, NT_DIMS, preferred_element_type=F32)
        s = jnp.concatenate([(qk[:, lanes(c)] * jnp.exp(dm[c] - m_t)).astype(BF16) for c in range(nc)], axis=1)
        decay = jnp.exp(inter - m_t)
        c_old = c_state[h]
        inter_acc = jnp.dot(qb, c_old.astype(BF16), preferred_element_type=F32)
        acc = jnp.dot(s, v_aug, preferred_element_type=F32) + jnp.tile(decay, (1, DH_A // 128 + 1)) * inter_acc
        den = acc[:, DH_A:]
        inv = 1.0 / jnp.maximum(jnp.abs(den), jnp.exp(-m_t))
        hh = acc[:, :DH_A] * jnp.tile(inv, (1, DH_A // 128))

        g = b_row[:, L - 1:L]
        a_rep = g + c_rep
        m_new = jnp.maximum(g + m_prev, jnp.max(a_rep, axis=0, keepdims=True))
        kw = (k * jnp.tile(jnp.exp(a_rep - m_new), (1, DH_A // 128))).astype(BF16)
        sc = jnp.exp(g + m_prev - m_new)
        c_state[h] = (jnp.tile(sc, (1, DH_A // 128 + 1)) * c_old
                      + lax.dot_general(kw, v_aug, TN_DIMS, preferred_element_type=F32))
        m_state[h] = jnp.broadcast_to(m_new, m_state.shape[1:])

        mu = jnp.mean(hh, axis=1, keepdims=True)
        hc = hh - mu
        var = jnp.mean(hc * hc, axis=1, keepdims=True)
        hn = hc * lax.rsqrt(var + LN_EPS) * ng_ref[:, hs]
        og = _sigmoid(o_ref[0, :, hs])
        y_ref[0, :, hs] = (og.astype(F32) * hn).astype(BF16)


def _mlstm(p3, rows, cols, conv_w, conv_b, wq, wk, norm_g):
    bsz, t, _ = p3.shape
    L = min(MLSTM_CHUNK, t)
    colblk = lambda c: pl.BlockSpec((1, L, 1024), lambda b, i, c=c: (b, i, c))
    full = lambda shape: pl.BlockSpec(shape, lambda b, i: tuple(0 for _ in shape))
    return pl.pallas_call(
        functools.partial(_mlstm_kernel, chunk=L),
        grid=(bsz, t // L),
        in_specs=[colblk(COL_CA), colblk(COL_VA), colblk(COL_OA),
                  pl.BlockSpec((1, 8, L), lambda b, i: (b, 0, i)),
                  pl.BlockSpec((1, L, 128), lambda b, i: (b, i, 0)),
                  full((CONV_K, 1024)), full((1, 1024)), full((H_A, DH_A, DH_A)), full((H_A, DH_A, DH_A)),
                  full((1, 1024))],
        out_specs=pl.BlockSpec((1, L, 1024), lambda b, i: (b, i, 0)),
        out_shape=jax.ShapeDtypeStruct((bsz, t, 1024), BF16),
        scratch_shapes=[pltpu.VMEM((8 + L, 1024), F32), pltpu.VMEM((H_A, DH_A, DH_A + 128), F32),
                        pltpu.VMEM((H_A, 8, 128), F32)],
        compiler_params=_cparams(("parallel", "arbitrary")), name="mlstm",
    )(p3, p3, p3, rows, cols, conv_w, conv_b.reshape(1, -1), wq, wk, norm_g.reshape(1, -1))


def _flash_kernel(*refs, tq, tk, dk, heads, has_bias):
    if has_bias:
        q_ref, k_ref, kb_ref, v_ref, o_ref, m_sc, l_sc, acc_sc = refs
    else:
        q_ref, k_ref, v_ref, o_ref, m_sc, l_sc, acc_sc = refs
        kb_ref = None
    i = pl.program_id(2)
    qs = []
    for g in range(heads):
        q = q_ref[0, :, g * dk:(g + 1) * dk]
        if has_bias:
            lane = lax.broadcasted_iota(jnp.int32, (tq, 128), 1) - 3 * (pl.program_id(1) * heads + g)
            q = jnp.concatenate([q, jnp.where((lane >= 0) & (lane < 3), -1.0, 0.0).astype(BF16)], axis=1)
        qs.append(q)

    def scores(g, key0, nk, r0, nr, mask_off):
        ks = pl.ds(key0, nk)
        kj = k_ref[0, ks, g * dk:(g + 1) * dk]
        if has_bias:
            kj = jnp.concatenate([kj, kb_ref[0, ks, :]], axis=1)
        s = lax.dot_general(qs[g][r0:r0 + nr], kj, NT_DIMS, preferred_element_type=F32)
        if mask_off is not None:
            r_i = r0 + lax.broadcasted_iota(jnp.int32, (nr, nk), 0)
            c_i = mask_off + lax.broadcasted_iota(jnp.int32, (nr, nk), 1)
            s = jnp.where(c_i <= r_i, s, -jnp.inf)
        return s

    def update(g, s, key0, nk, r0, nr, first):
        rs = slice(r0, r0 + nr)
        vj = v_ref[0, pl.ds(key0, nk), g * 128:(g + 1) * 128]
        s_max = jnp.max(s, axis=1, keepdims=True)
        if first:
            m_new = jnp.broadcast_to(s_max, (nr, 128))
            l_part = jnp.zeros((nr, 128), F32)
        else:
            m_prev = m_sc[g, rs]
            m_new = jnp.maximum(m_prev, s_max)
            alpha = jnp.exp2(m_prev - m_new)
            l_part = alpha * l_sc[g, rs]
        ps = []
        for c in range(nk // 128):
            p_c = jnp.exp2(s[:, c * 128:(c + 1) * 128] - m_new)
            l_part = l_part + p_c
            ps.append(p_c.astype(BF16))
        pv = jnp.dot(jnp.concatenate(ps, axis=1), vj, preferred_element_type=F32)
        acc_sc[g, rs] = pv if first else alpha * acc_sc[g, rs] + pv
        l_sc[g, rs] = l_part
        m_sc[g, rs] = m_new

    def finish(g):
        l = jnp.sum(l_sc[g], axis=1, keepdims=True)
        o_ref[0, :, g * 128:(g + 1) * 128] = (acc_sc[g] / l).astype(o_ref.dtype)

    def sweep(key0, nk, r0, nr, mask_off, first=False, last=False):
        s_cur = scores(0, key0, nk, r0, nr, mask_off)
        for g in range(heads):
            s_next = scores(g + 1, key0, nk, r0, nr, mask_off) if g + 1 < heads else None
            update(g, s_cur, key0, nk, r0, nr, first)
            if last:
                finish(g)
            s_cur = s_next

    nsub = tq // tk

    def body(j, carry):
        sweep(pl.multiple_of(j * tk, tk), tk, 0, tq, None)
        return carry

    def diag(d, **kw):
        sweep(pl.multiple_of(i * tq + d * tk, tk), tk, d * tk, tq - d * tk, d * tk, **kw)

    diag(0, first=True)
    lax.fori_loop(0, i * nsub, body, 0)
    for d in range(1, nsub):
        diag(d, last=(d == nsub - 1))
    if nsub == 1:
        for g in range(heads):
            finish(g)


def _flash(q_arr, k_arr, kb_arr, v_arr, *, n_heads, dk, q_blk, k_blk, v_blk):
    bsz, t, _ = q_arr.shape
    tq = min(FLASH_TQ, t)
    tk = min(FLASH_TK, t)
    g = FLASH_HEADS_PER_STEP
    hp = n_heads // g
    in_specs = [pl.BlockSpec((1, tq, g * dk), lambda b, h, i: (b, i, q_blk // g + h)),
                pl.BlockSpec((1, t, g * dk), lambda b, h, i: (b, 0, k_blk // g + h))]
    args = [q_arr, k_arr]
    if kb_arr is not None:
        in_specs.append(pl.BlockSpec((1, t, 128), lambda b, h, i: (b, 0, 0)))
        args.append(kb_arr)
    in_specs.append(pl.BlockSpec((1, t, g * 128), lambda b, h, i: (b, 0, v_blk // g + h)))
    args.append(v_arr)
    return pl.pallas_call(
        functools.partial(_flash_kernel, tq=tq, tk=tk, dk=dk, heads=g, has_bias=kb_arr is not None),
        grid=(bsz, hp, t // tq),
        in_specs=in_specs,
        out_specs=pl.BlockSpec((1, tq, g * 128), lambda b, h, i: (b, i, h)),
        out_shape=jax.ShapeDtypeStruct((bsz, t, n_heads * 128), BF16),
        scratch_shapes=[pltpu.VMEM((g, tq, 128), F32), pltpu.VMEM((g, tq, 128), F32), pltpu.VMEM((g, tq, 128), F32)],
        compiler_params=_cparams(("parallel", "parallel", "arbitrary")),
        name="flash_fox" if kb_arr is not None else "flash_mla",
    )(*args)


def _mla_prep_kernel(c_ref, tab_ref, gq_ref, gkv_ref, wq_ref, wkv_ref, q_out, k_out, v_out):
    c = c_ref[...].astype(F32)
    ckv = c[:, 0:KV_RANK]
    cq = c[:, KV_RANK:KV_RANK + Q_RANK]
    kr = c[:, KV_RANK + Q_RANK:]

    def rms(x, g):
        ms = jnp.mean(x * x, axis=-1, keepdims=True)
        return (x * lax.rsqrt(ms + RMS_EPS) * g).astype(BF16)

    qf = jnp.dot(rms(cq, gq_ref[...]), wq_ref[...], preferred_element_type=F32)
    kvf = jnp.dot(rms(ckv, gkv_ref[...]), wkv_ref[...], preferred_element_type=F32)
    tab = tab_ref[...]

    def rope(xx):
        pr = xx * tab
        return pr + pltpu.roll(pr, DH_ROPE, 1)

    lane = lax.broadcasted_iota(jnp.int32, kr.shape, 1)
    k_rope = jnp.where(lane < DH_ROPE, rope(kr), 0.0).astype(BF16)
    for h in range(H_C):
        lo = h * DK_C
        q_out[:, lo:lo + 128] = qf[:, lo:lo + 128].astype(BF16)
        q_out[:, lo + 128:lo + 256] = rope(qf[:, lo + 128:lo + 256]).astype(BF16)
        k_out[:, lo:lo + 128] = kvf[:, lo:lo + 128].astype(BF16)
        k_out[:, lo + 128:lo + 256] = k_rope
        v_out[:, h * DH_V:(h + 1) * DH_V] = kvf[:, lo + 128:lo + 256].astype(BF16)


def _mla_prep(p2, tab, g_cq, g_ckv, wq, wkv):
    n = p2.shape[0]
    tm = min(512, n)
    full = lambda shape: pl.BlockSpec(shape, lambda i: tuple(0 for _ in shape))
    return pl.pallas_call(
        _mla_prep_kernel, grid=(n // tm,),
        in_specs=[pl.BlockSpec((tm, LOWRANK_W), lambda i: (i, LOWRANK_OFF // LOWRANK_W)),
                  pl.BlockSpec((tm, 128), lambda i: (i, 0)),
                  full((1, Q_RANK)), full((1, KV_RANK)), full((Q_RANK, H_C * DK_C)), full((KV_RANK, H_C * DK_C))],
        out_specs=[pl.BlockSpec((tm, H_C * DK_C), lambda i: (i, 0)),
                   pl.BlockSpec((tm, H_C * DK_C), lambda i: (i, 0)),
                   pl.BlockSpec((tm, H_C * DH_V), lambda i: (i, 0))],
        out_shape=[jax.ShapeDtypeStruct((n, H_C * DK_C), BF16), jax.ShapeDtypeStruct((n, H_C * DK_C), BF16),
                   jax.ShapeDtypeStruct((n, H_C * DH_V), BF16)],
        compiler_params=_cparams(("parallel",)), name="mla_prep",
    )(p2, tab, g_cq.reshape(1, -1), g_ckv.reshape(1, -1), wq, wkv)


def _merge_kernel(ya_ref, yb_ref, yc_ref, za_ref, zb_ref, zc_ref, ga_ref, gb_ref, gc_ref, h_ref,
                  pa_ref, pb_ref, pc_ref, wo_ref, lg_ref, lb_ref, hf_ref, hb_ref, *, alpha):
    def front(rs):
        def branch(y_ref, z_ref, g_ref, p_ref):
            hz = z_ref[rs, :] * 0.5
            y = y_ref[rs, :] * (hz + hz * jnp.tanh(hz))
            gate = 0.5 + 0.5 * jnp.tanh(g_ref[rs, :] * 0.5)
            return gate.astype(F32) * jnp.dot(y, p_ref[...], preferred_element_type=F32)

        return (branch(ya_ref, za_ref, ga_ref, pa_ref) + branch(yb_ref, zb_ref, gb_ref, pb_ref)
                + branch(yc_ref, zc_ref, gc_ref, pc_ref)).astype(BF16)

    def back(rs, merged):
        out = jnp.dot(merged, wo_ref[...], preferred_element_type=F32)
        r = alpha * h_ref[rs, :] + out
        mu = jnp.mean(r, axis=-1, keepdims=True)
        rc = r - mu
        var = jnp.mean(rc * rc, axis=-1, keepdims=True)
        y = rc * lax.rsqrt(var + LN_EPS) * lg_ref[...] + lb_ref[...]
        hf_ref[rs, :] = y
        hb_ref[rs, :] = y.astype(BF16)

    tm = h_ref.shape[0]
    strips = [slice(r, r + tm // MERGE_STRIPS) for r in range(0, tm, tm // MERGE_STRIPS)]
    cur = front(strips[0])
    for n, rs in enumerate(strips):
        nxt = front(strips[n + 1]) if n + 1 < len(strips) else None
        back(rs, cur)
        cur = nxt


def _merge(ya, yb, yc, p2, hf, pa, pb, pc, wo, ln_g, ln_b, alpha):
    n = hf.shape[0]
    tm = min(512, n)
    row = pl.BlockSpec((tm, D_MODEL), lambda i: (i, 0))
    colblk = lambda c: pl.BlockSpec((tm, 1024), lambda i, c=c: (i, c))
    wfull = pl.BlockSpec((1024, D_MODEL), lambda i: (0, 0), pipeline_mode=pl.Buffered(1))
    vec = pl.BlockSpec((1, D_MODEL), lambda i: (0, 0))
    return pl.pallas_call(
        functools.partial(_merge_kernel, alpha=alpha), grid=(n // tm,),
        in_specs=[row, row, row, colblk(COL_ZA), colblk(COL_ZB), colblk(COL_ZC),
                  colblk(COL_GA), colblk(COL_GB), colblk(COL_GC), row,
                  wfull, wfull, wfull, wfull, vec, vec],
        out_specs=[row, row],
        out_shape=[jax.ShapeDtypeStruct((n, D_MODEL), F32), jax.ShapeDtypeStruct((n, D_MODEL), BF16)],
        compiler_params=_cparams(("parallel",)), name="merge",
    )(ya, yb, yc, p2, p2, p2, p2, p2, p2, hf, pa, pb, pc, wo, ln_g.reshape(1, -1), ln_b.reshape(1, -1))


def _rot_cols(w):
    half = w.shape[-1] // 2
    return jnp.concatenate([-w[..., half:], w[..., :half]], axis=-1)


def _layer_weights(w_in, b_ia, b_fa, b_fb, w_uq, w_ukv):
    depth = w_in.shape[0]
    idx = []
    acc = 0
    for s in IN_SPLITS[:-1]:
        acc += s
        idx.append(acc)
    (w_ca, w_va, w_oa, w_ia, w_fa, w_za, w_qb, w_kb, w_vb, w_fb, w_zb,
     w_cq, w_ckv, w_kr, w_zc, w_ga, w_gb, w_gc) = jnp.split(w_in, idx, axis=-1)
    w_qb = w_qb * (DH_B ** -0.5 * LOG2E)
    w_big = jnp.concatenate([w.astype(BF16) for w in (w_ca, w_va, w_oa, w_za, w_qb, w_kb, w_vb, w_zb, w_zc, w_ga,
                                                      w_gb, w_gc, w_ckv, w_cq, w_kr, _rot_cols(w_kr))], axis=-1)
    zpad = lambda k: jnp.zeros((depth, D_MODEL, k), F32)
    w_f = jnp.concatenate([w_fa, w_fa, w_fb, zpad(112)], axis=-1)
    w_i = jnp.concatenate([zpad(4), w_ia, zpad(120)], axis=-1)
    b_f = jnp.concatenate([b_fa, b_fa, b_fb, jnp.zeros((depth, 112), F32)], axis=-1)
    b_i = jnp.concatenate([jnp.zeros((depth, 4), F32), b_ia, jnp.zeros((depth, 120), F32)], axis=-1)
    gw = (jnp.concatenate([w_f, w_i], axis=-1).astype(BF16),
          jnp.swapaxes(jnp.concatenate([w_f[..., :16], w_i[..., :16]], axis=-1), -1, -2).astype(BF16),
          b_f[:, :16, None], b_i[:, :16, None], b_f[:, None, :], b_i[:, None, :])
    wq3 = w_uq.reshape(depth, Q_RANK, H_C, DH_NOPE + DH_ROPE) * ((DH_NOPE + DH_ROPE) ** -0.5 * LOG2E)
    wq_rope = wq3[..., DH_NOPE:]
    wq = jnp.concatenate([wq3, _rot_cols(wq_rope)], axis=-1).reshape(depth, Q_RANK, H_C * DK_C).astype(BF16)
    return w_big, gw, wq, w_ukv.astype(BF16)


def kernel(x, positions, ln_in_g, ln_in_b, w_in, conv_w, conv_b, w_qa, w_ka, b_ia, b_fa, norm_ga, b_fb,
           g_cq, w_uq, g_ckv, w_ukv, p_a, p_b, p_c, w_out, ln_g, ln_b):
    bsz, t, _ = x.shape
    n = bsz * t
    depth = w_in.shape[0]
    alpha = (2.0 * depth) ** 0.25

    w_big, gw, wq, wkv = _layer_weights(w_in, b_ia, b_fa, b_fb, w_uq, w_ukv)
    wqa, wka, pa, pb, pc, wo = (w.astype(BF16) for w in (w_qa, w_ka, p_a, p_b, p_c, w_out))
    tab = _rope_table(positions)
    hf, hb = _input_ln(x.reshape(n, D_MODEL), ln_in_g, ln_in_b)
    for l in range(depth):
        p2 = _in_proj(hb, w_big[l])
        p3 = p2.reshape(bsz, t, N_BIG)
        rows, cols, kb = _gates(hb.reshape(bsz, t, D_MODEL), tuple(w[l] for w in gw))
        ya = _mlstm(p3, rows, cols, conv_w[l], conv_b[l], wqa[l], wka[l], norm_ga[l])
        yb = _flash(p3, p3, kb, p3, n_heads=H_B, dk=DH_B, q_blk=COL_QB * 8, k_blk=COL_KB * 8, v_blk=COL_VB * 8)
        qc, kc, vc = _mla_prep(p2, tab, g_cq[l], g_ckv[l], wq[l], wkv[l])
        yc = _flash(qc.reshape(bsz, t, -1), kc.reshape(bsz, t, -1), None, vc.reshape(bsz, t, -1),
                    n_heads=H_C, dk=DK_C, q_blk=0, k_blk=0, v_blk=0)
        hf, hb = _merge(ya.reshape(n, -1), yb.reshape(n, -1), yc.reshape(n, -1), p2, hf,
                        pa[l], pb[l], pc[l], wo[l], ln_g[l], ln_b[l], alpha)
    return hf.reshape(bsz, t, D_MODEL)
```

```python
import functools
import math

import jax
import jax.numpy as jnp
from jax import lax
from jax.experimental import pallas as pl
from jax.experimental.pallas import tpu as pltpu

F32 = jnp.float32
BF16 = jnp.bfloat16

D_MODEL = 1024
H_A, DH_A, CONV_K = 4, 256, 4
H_B, DH_B = 8, 128
H_C, DH_NOPE, DH_ROPE, DH_V = 8, 128, 64, 128
Q_RANK, KV_RANK = 384, 256
ROPE_BASE = 10000.0
LN_EPS = 1e-5
RMS_EPS = 1e-6
IN_SPLITS = (1024, 1024, 1024, 4, 4, 1024,
             1024, 1024, 1024, 8, 1024,
             Q_RANK, KV_RANK, DH_ROPE, 1024,
             1024, 1024, 1024)

COL_CA, COL_VA, COL_OA, COL_ZA, COL_QB, COL_KB, COL_VB, COL_ZB, COL_ZC, COL_GA, COL_GB, COL_GC = range(12)
LOWRANK_OFF = 12 * 1024
LOWRANK_W = 768
N_BIG = LOWRANK_OFF + LOWRANK_W
DK_C = 256

MLSTM_CHUNK = 256
GATES_SEQS_PER_STEP = 4
FLASH_TQ = 1024
FLASH_TK = 512
FLASH_HEADS_PER_STEP = 4
MERGE_STRIPS = 2
VMEM_LIMIT = 56 * 1024 * 1024
LOG2E = math.log2(math.e)

NT_DIMS = (((1,), (1,)), ((), ()))
TN_DIMS = (((0,), (0,)), ((), ()))


def _cparams(sem):
    return pltpu.CompilerParams(dimension_semantics=sem, vmem_limit_bytes=VMEM_LIMIT)


def _sigmoid(x):
    return 0.5 + 0.5 * jnp.tanh(0.5 * x)


def _silu(x):
    hx = 0.5 * x
    return hx + hx * jnp.tanh(hx)


def _log_sigmoid(x):
    return jnp.minimum(x, 0.0) - jnp.log1p(jnp.exp(-jnp.abs(x)))


def _split3(x):
    hi = x.astype(BF16)
    r = x - hi.astype(F32)
    mid = r.astype(BF16)
    lo = (r - mid.astype(F32)).astype(BF16)
    return hi, mid, lo


def _ln_kernel(x_ref, g_ref, b_ref, hf_ref, hb_ref):
    x = x_ref[...]
    mu = jnp.mean(x, axis=-1, keepdims=True)
    xc = x - mu
    var = jnp.mean(xc * xc, axis=-1, keepdims=True)
    y = xc * lax.rsqrt(var + LN_EPS) * g_ref[...] + b_ref[...]
    hf_ref[...] = y
    hb_ref[...] = y.astype(BF16)


def _input_ln(x2, g, b):
    n = x2.shape[0]
    tm = min(512, n)
    row = pl.BlockSpec((tm, D_MODEL), lambda i: (i, 0))
    vec = pl.BlockSpec((1, D_MODEL), lambda i: (0, 0))
    return pl.pallas_call(
        _ln_kernel, grid=(n // tm,), in_specs=[row, vec, vec], out_specs=[row, row],
        out_shape=[jax.ShapeDtypeStruct((n, D_MODEL), F32), jax.ShapeDtypeStruct((n, D_MODEL), BF16)],
        compiler_params=_cparams(("parallel",)), name="input_ln",
    )(x2, g.reshape(1, -1), b.reshape(1, -1))


def _rope_kernel(pos_ref, freq_ref, tab_ref):
    ang = pos_ref[...].astype(F32) * freq_ref[...]
    lane = lax.broadcasted_iota(jnp.int32, ang.shape, 1)
    tab_ref[...] = jnp.where(lane < DH_ROPE, jnp.cos(ang), jnp.sin(ang))


def _rope_table(positions):
    n = positions.size
    tm = min(1024, n)
    inv_freq = ROPE_BASE ** (-jnp.arange(0, DH_ROPE, 2, dtype=F32) / DH_ROPE)
    freq = jnp.tile(inv_freq, 4).reshape(1, 128)
    return pl.pallas_call(
        _rope_kernel, grid=(n // tm,),
        in_specs=[pl.BlockSpec((tm, 1), lambda i: (i, 0)), pl.BlockSpec((1, 128), lambda i: (0, 0))],
        out_specs=pl.BlockSpec((tm, 128), lambda i: (i, 0)),
        out_shape=jax.ShapeDtypeStruct((n, 128), F32),
        compiler_params=_cparams(("parallel",)), name="rope_table",
    )(positions.reshape(n, 1), freq)


def _matmul_kernel(x_ref, w_ref, o_ref):
    o_ref[...] = jnp.dot(x_ref[...], w_ref[...], preferred_element_type=F32).astype(o_ref.dtype)


def _in_proj(hb, w_big):
    n = hb.shape[0]
    tm = min(1024, n)
    tn = N_BIG // 3
    return pl.pallas_call(
        _matmul_kernel, grid=(n // tm, N_BIG // tn),
        in_specs=[pl.BlockSpec((tm, D_MODEL), lambda i, j: (i, 0)),
                  pl.BlockSpec((D_MODEL, tn), lambda i, j: (0, j))],
        out_specs=pl.BlockSpec((tm, tn), lambda i, j: (i, j)),
        out_shape=jax.ShapeDtypeStruct((n, N_BIG), BF16),
        compiler_params=_cparams(("parallel", "arbitrary")), name="in_proj",
    )(hb, w_big)


def _gates_kernel(h_ref, wfi_ref, wfit_ref, bfc_ref, bic_ref, bfr_ref, bir_ref,
                  rows_ref, cols_ref, kb_ref, carry_ref, *, tg, chunk, nb):
    t = pl.program_id(1)

    @pl.when(t == 0)
    def _():
        carry_ref[...] = jnp.zeros_like(carry_ref)

    seq_rows = lambda a, i: a[i * tg:(i + 1) * tg]
    seq_lanes = lambda a, i, w: a[:, i * w:(i + 1) * w]
    hb = h_ref[...].reshape(nb * tg, D_MODEL)
    r_i = lax.broadcasted_iota(jnp.int32, (tg, tg), 0)
    c_i = lax.broadcasted_iota(jnp.int32, (tg, tg), 1)
    same_chunk = (r_i // chunk) == (c_i // chunk)
    ind = lambda mask: jnp.where(mask, 1.0, 0.0).astype(BF16)
    upper_bd = ind((r_i <= c_i) & same_chunk)
    lower = ind(r_i >= c_i)
    lower_bd = ind((r_i >= c_i) & same_chunk)

    g_rows = lax.dot_general(wfit_ref[...], hb, NT_DIMS, preferred_element_type=F32)
    gf = g_rows[0:16] + bfc_ref[...]
    gi = g_rows[16:32] + bic_ref[...]
    lf3 = jnp.concatenate(_split3(_log_sigmoid(gf)), axis=0)
    y3 = jnp.dot(jnp.concatenate([seq_lanes(lf3, i, tg) for i in range(nb)], axis=0), upper_bd,
                 preferred_element_type=F32)
    row = lax.broadcasted_iota(jnp.int32, (16, tg), 0)
    for i in range(nb):
        y_bd = y3[48 * i:48 * i + 16] + y3[48 * i + 16:48 * i + 32] + y3[48 * i + 32:48 * i + 48]
        rows_ref[i] = jnp.where(row < 4, y_bd, seq_lanes(gi, i, tg) - y_bd)[0:8]

    g_cols = jnp.dot(hb, wfi_ref[...], preferred_element_type=F32)
    gfc = g_cols[:, 0:128] + bfr_ref[...]
    gic = g_cols[:, 128:256] + bir_ref[...]
    lfc3 = jnp.concatenate(_split3(_log_sigmoid(gfc)), axis=1)
    lfc3 = jnp.concatenate([seq_rows(lfc3, i) for i in range(nb)], axis=1)
    y_bd_c = jnp.dot(lower_bd, lfc3, preferred_element_type=F32)
    y_full_c = jnp.dot(lower, lfc3, preferred_element_type=F32)
    sum3 = lambda a: a[:, 0:128] + a[:, 128:256] + a[:, 256:384]
    col = lax.broadcasted_iota(jnp.int32, (tg, 128), 1)
    parts = []
    for i in range(nb):
        yc = sum3(seq_lanes(y_bd_c, i, 384))
        cols_ref[i] = jnp.where(col < 4, yc, seq_rows(gic, i) - yc)
        cum = sum3(seq_lanes(y_full_c, i, 384)) + carry_ref[i, 0:1, :]
        carry_ref[i] = jnp.broadcast_to(cum[tg - 1:tg, :], carry_ref.shape[1:])
        parts.append(jnp.concatenate(_split3(cum * LOG2E), axis=1))
    sr = lax.broadcasted_iota(jnp.int32, (3 * 128, 128), 0)
    sc = lax.broadcasted_iota(jnp.int32, (3 * 128, 128), 1)
    scatter = ind((sr % 128 == 8 + sc // 3) & (sr // 128 == sc % 3) & (sc < 3 * H_B))
    kb = jnp.dot(jnp.concatenate(parts, axis=0), scatter, preferred_element_type=F32).astype(BF16)
    for i in range(nb):
        kb_ref[i] = seq_rows(kb, i)


def _gates(hb3, gw):
    bsz, t, _ = hb3.shape
    tg = min(512, t)
    nb = math.gcd(GATES_SEQS_PER_STEP, bsz)
    full = lambda shape: pl.BlockSpec(shape, lambda b, i: tuple(0 for _ in shape))
    return pl.pallas_call(
        functools.partial(_gates_kernel, tg=tg, chunk=min(MLSTM_CHUNK, t), nb=nb),
        grid=(bsz // nb, t // tg),
        in_specs=[pl.BlockSpec((nb, tg, D_MODEL), lambda b, i: (b, i, 0)),
                  full((D_MODEL, 256)), full((32, D_MODEL)),
                  full((16, 1)), full((16, 1)), full((1, 128)), full((1, 128))],
        out_specs=[pl.BlockSpec((nb, 8, tg), lambda b, i: (b, 0, i)),
                   pl.BlockSpec((nb, tg, 128), lambda b, i: (b, i, 0)),
                   pl.BlockSpec((nb, tg, 128), lambda b, i: (b, i, 0))],
        out_shape=[jax.ShapeDtypeStruct((bsz, 8, t), F32), jax.ShapeDtypeStruct((bsz, t, 128), F32),
                   jax.ShapeDtypeStruct((bsz, t, 128), BF16)],
        scratch_shapes=[pltpu.VMEM((nb, 8, 128), F32)],
        compiler_params=_cparams(("parallel", "arbitrary")), name="gates",
    )(hb3, *gw)


def _mlstm_kernel(c_ref, v_ref, o_ref, rows_ref, cols_ref, cw_ref, cb_ref, wq_ref, wk_ref, ng_ref,
                  y_ref, cbuf, c_state, m_state, *, chunk):
    L = chunk
    t = pl.program_id(1)

    @pl.when(t == 0)
    def _():
        cbuf[0:8, :] = jnp.zeros((8, D_MODEL), F32)
        c_state[...] = jnp.zeros_like(c_state)
        m_state[...] = jnp.zeros_like(m_state)

    cbuf[8:8 + L, :] = c_ref[0].astype(F32)

    nc = L // 128
    lanes = lambda c: slice(c * 128, (c + 1) * 128)
    r_i = lax.broadcasted_iota(jnp.int32, (L, 128), 0)
    c_i = lax.broadcasted_iota(jnp.int32, (L, 128), 1)
    causal = [c_i + c * 128 <= r_i for c in range(nc)]
    ones_blk = jnp.ones((L, 128), BF16)
    for h in range(H_A):
        hs = slice(h * DH_A, (h + 1) * DH_A)
        conv = cb_ref[:, hs]
        for j in range(CONV_K):
            conv = conv + cbuf[8 - (CONV_K - 1) + j:8 - (CONV_K - 1) + j + L, hs] * cw_ref[j:j + 1, hs]
        uh = _silu(conv).astype(BF16)
        q = jnp.dot(uh, wq_ref[h], preferred_element_type=F32)
        k = jnp.dot(uh, wk_ref[h], preferred_element_type=F32) * (DH_A ** -0.5)
        qb = q.astype(BF16)
        kb = k.astype(BF16)
        v_aug = jnp.concatenate([v_ref[0, :, hs], ones_blk], axis=1)
        b_rep = jnp.broadcast_to(cols_ref[0, :, h:h + 1], (L, 128))
        c_rep = jnp.broadcast_to(cols_ref[0, :, H_A + h:H_A + h + 1], (L, 128))
        b_row = rows_ref[0, h:h + 1, :]
        c_row = rows_ref[0, H_A + h:H_A + h + 1, :]
        m_prev = m_state[h, 0:1, :]

        dm = [jnp.where(causal[c], b_rep + c_row[:, lanes(c)], -jnp.inf) for c in range(nc)]
        dmax = dm[0]
        for c in range(1, nc):
            dmax = jnp.maximum(dmax, dm[c])
        inter = b_rep + m_prev
        m_t = jnp.maximum(jnp.max(dmax, axis=1, keepdims=True), inter)
        qk = lax.dot_general(qb, kb, NT_DIMS, preferred_element_type=F32)
        s = jnp.concatenate([(qk[:, lanes(c)] * jnp.exp(dm[c] - m_t)).astype(BF16) for c in range(nc)], axis=1)
        decay = jnp.exp(inter - m_t)
        c_old = c_state[h]
        inter_acc = jnp.dot(qb, c_old.astype(BF16), preferred_element_type=F32)
        acc = jnp.dot(s, v_aug, preferred_element_type=F32) + jnp.tile(decay, (1, DH_A // 128 + 1)) * inter_acc
        den = acc[:, DH_A:]
        inv = 1.0 / jnp.maximum(jnp.abs(den), jnp.exp(-m_t))
        hh = acc[:, :DH_A] * jnp.tile(inv, (1, DH_A // 128))

        g = b_row[:, L - 1:L]
        a_rep = g + c_rep
        m_new = jnp.maximum(g + m_prev, jnp.max(a_rep, axis=0, keepdims=True))
        kw = (k * jnp.tile(jnp.exp(a_rep - m_new), (1, DH_A // 128))).astype(BF16)
        sc = jnp.exp(g + m_prev - m_new)
        c_state[h] = (jnp.tile(sc, (1, DH_A // 128 + 1)) * c_old
                      + lax.dot_general(kw, v_aug, TN_DIMS, preferred_element_type=F32))
        m_state[h] = jnp.broadcast_to(m_new, m_state.shape[1:])

        mu = jnp.mean(hh, axis=1, keepdims=True)
        hc = hh - mu
        var = jnp.mean(hc * hc, axis=1, keepdims=True)
        hn = hc * lax.rsqrt(var + LN_EPS) * ng_ref[:, hs]
        og = _sigmoid(o_ref[0, :, hs])
        y_ref[0, :, hs] = (og.astype(F32) * hn).astype(BF16)
    cbuf[0:8, :] = cbuf[L:L + 8, :]


def _mlstm(p3, rows, cols, conv_w, conv_b, wq, wk, norm_g):
    bsz, t, _ = p3.shape
    L = min(MLSTM_CHUNK, t)
    colblk = lambda c: pl.BlockSpec((1, L, 1024), lambda b, i, c=c: (b, i, c))
    full = lambda shape: pl.BlockSpec(shape, lambda b, i: tuple(0 for _ in shape))
    return pl.pallas_call(
        functools.partial(_mlstm_kernel, chunk=L),
        grid=(bsz, t // L),
        in_specs=[colblk(COL_CA), colblk(COL_VA), colblk(COL_OA),
                  pl.BlockSpec((1, 8, L), lambda b, i: (b, 0, i)),
                  pl.BlockSpec((1, L, 128), lambda b, i: (b, i, 0)),
                  full((CONV_K, 1024)), full((1, 1024)), full((H_A, DH_A, DH_A)), full((H_A, DH_A, DH_A)),
                  full((1, 1024))],
        out_specs=pl.BlockSpec((1, L, 1024), lambda b, i: (b, i, 0)),
        out_shape=jax.ShapeDtypeStruct((bsz, t, 1024), BF16),
        scratch_shapes=[pltpu.VMEM((8 + L, 1024), F32), pltpu.VMEM((H_A, DH_A, DH_A + 128), F32),
                        pltpu.VMEM((H_A, 8, 128), F32)],
        compiler_params=_cparams(("parallel", "arbitrary")), name="mlstm",
    )(p3, p3, p3, rows, cols, conv_w, conv_b.reshape(1, -1), wq, wk, norm_g.reshape(1, -1))


def _flash_kernel(*refs, tq, tk, dk, heads, has_bias):
    if has_bias:
        q_ref, k_ref, kb_ref, v_ref, o_ref, m_sc, l_sc, acc_sc = refs
    else:
        q_ref, k_ref, v_ref, o_ref, m_sc, l_sc, acc_sc = refs
        kb_ref = None
    i = pl.program_id(2)
    qs = []
    for g in range(heads):
        q = q_ref[0, :, g * dk:(g + 1) * dk]
        if has_bias:
            lane = lax.broadcasted_iota(jnp.int32, (tq, 128), 1) - 3 * (pl.program_id(1) * heads + g)
            q = jnp.concatenate([q, jnp.where((lane >= 0) & (lane < 3), -1.0, 0.0).astype(BF16)], axis=1)
        qs.append(q)

    def scores(g, key0, nk, r0, nr, mask_off):
        ks = pl.ds(key0, nk)
        kj = k_ref[0, ks, g * dk:(g + 1) * dk]
        if has_bias:
            kj = jnp.concatenate([kj, kb_ref[0, ks, :]], axis=1)
        s = lax.dot_general(qs[g][r0:r0 + nr], kj, NT_DIMS, preferred_element_type=F32)
        if mask_off is not None:
            r_i = r0 + lax.broadcasted_iota(jnp.int32, (nr, nk), 0)
            c_i = mask_off + lax.broadcasted_iota(jnp.int32, (nr, nk), 1)
            s = jnp.where(c_i <= r_i, s, -jnp.inf)
        return s

    def update(g, s, key0, nk, r0, nr, first):
        rs = slice(r0, r0 + nr)
        vj = v_ref[0, pl.ds(key0, nk), g * 128:(g + 1) * 128]
        s_max = jnp.max(s, axis=1, keepdims=True)
        if first:
            m_new = jnp.broadcast_to(s_max, (nr, 128))
            l_part = jnp.zeros((nr, 128), F32)
        else:
            m_prev = m_sc[g, rs]
            m_new = jnp.maximum(m_prev, s_max)
            alpha = jnp.exp2(m_prev - m_new)
            l_part = alpha * l_sc[g, rs]
        ps = []
        for c in range(nk // 128):
            p_c = jnp.exp2(s[:, c * 128:(c + 1) * 128] - m_new)
            l_part = l_part + p_c
            ps.append(p_c.astype(BF16))
        pv = jnp.dot(jnp.concatenate(ps, axis=1), vj, preferred_element_type=F32)
        acc_sc[g, rs] = pv if first else alpha * acc_sc[g, rs] + pv
        l_sc[g, rs] = l_part
        m_sc[g, rs] = m_new

    def finish(g):
        l = jnp.sum(l_sc[g], axis=1, keepdims=True)
        o_ref[0, :, g * 128:(g + 1) * 128] = (acc_sc[g] / l).astype(o_ref.dtype)

    def sweep(key0, nk, r0, nr, mask_off, first=False, last=False):
        s_cur = scores(0, key0, nk, r0, nr, mask_off)
        for g in range(heads):
            s_next = scores(g + 1, key0, nk, r0, nr, mask_off) if g + 1 < heads else None
            update(g, s_cur, key0, nk, r0, nr, first)
            if last:
                finish(g)
            s_cur = s_next

    nsub = tq // tk

    def body(j, carry):
        sweep(pl.multiple_of(j * tk, tk), tk, 0, tq, None)
        return carry

    def diag(d, **kw):
        sweep(pl.multiple_of(i * tq + d * tk, tk), tk, d * tk, tq - d * tk, d * tk, **kw)

    diag(0, first=True)
    lax.fori_loop(0, i * nsub, body, 0)
    for d in range(1, nsub):
        diag(d, last=(d == nsub - 1))
    if nsub == 1:
        for g in range(heads):
            finish(g)


def _flash(q_arr, k_arr, kb_arr, v_arr, *, n_heads, dk, q_blk, k_blk, v_blk):
    bsz, t, _ = q_arr.shape
    tq = min(FLASH_TQ, t)
    tk = min(FLASH_TK, t)
    g = FLASH_HEADS_PER_STEP
    hp = n_heads // g
    in_specs = [pl.BlockSpec((1, tq, g * dk), lambda b, h, i: (b, i, q_blk // g + h)),
                pl.BlockSpec((1, t, g * dk), lambda b, h, i: (b, 0, k_blk // g + h))]
    args = [q_arr, k_arr]
    if kb_arr is not None:
        in_specs.append(pl.BlockSpec((1, t, 128), lambda b, h, i: (b, 0, 0)))
        args.append(kb_arr)
    in_specs.append(pl.BlockSpec((1, t, g * 128), lambda b, h, i: (b, 0, v_blk // g + h)))
    args.append(v_arr)
    return pl.pallas_call(
        functools.partial(_flash_kernel, tq=tq, tk=tk, dk=dk, heads=g, has_bias=kb_arr is not None),
        grid=(bsz, hp, t // tq),
        in_specs=in_specs,
        out_specs=pl.BlockSpec((1, tq, g * 128), lambda b, h, i: (b, i, h)),
        out_shape=jax.ShapeDtypeStruct((bsz, t, n_heads * 128), BF16),
        scratch_shapes=[pltpu.VMEM((g, tq, 128), F32), pltpu.VMEM((g, tq, 128), F32), pltpu.VMEM((g, tq, 128), F32)],
        compiler_params=_cparams(("parallel", "parallel", "arbitrary")),
        name="flash_fox" if kb_arr is not None else "flash_mla",
    )(*args)


def _mla_prep_kernel(c_ref, tab_ref, gq_ref, gkv_ref, wq_ref, wkv_ref, q_out, k_out, v_out):
    c = c_ref[...].astype(F32)
    ckv = c[:, 0:KV_RANK]
    cq = c[:, KV_RANK:KV_RANK + Q_RANK]
    kr = c[:, KV_RANK + Q_RANK:]

    def rms(x, g):
        ms = jnp.mean(x * x, axis=-1, keepdims=True)
        return (x * lax.rsqrt(ms + RMS_EPS) * g).astype(BF16)

    qf = jnp.dot(rms(cq, gq_ref[...]), wq_ref[...], preferred_element_type=F32)
    kvf = jnp.dot(rms(ckv, gkv_ref[...]), wkv_ref[...], preferred_element_type=F32)
    tab = tab_ref[...]

    def rope(xx):
        pr = xx * tab
        return pr + pltpu.roll(pr, DH_ROPE, 1)

    lane = lax.broadcasted_iota(jnp.int32, kr.shape, 1)
    k_rope = jnp.where(lane < DH_ROPE, rope(kr), 0.0).astype(BF16)
    for h in range(H_C):
        lo = h * DK_C
        q_out[:, lo:lo + 128] = qf[:, lo:lo + 128].astype(BF16)
        q_out[:, lo + 128:lo + 256] = rope(qf[:, lo + 128:lo + 256]).astype(BF16)
        k_out[:, lo:lo + 128] = kvf[:, lo:lo + 128].astype(BF16)
        k_out[:, lo + 128:lo + 256] = k_rope
        v_out[:, h * DH_V:(h + 1) * DH_V] = kvf[:, lo + 128:lo + 256].astype(BF16)


def _mla_prep(p2, tab, g_cq, g_ckv, wq, wkv):
    n = p2.shape[0]
    tm = min(512, n)
    full = lambda shape: pl.BlockSpec(shape, lambda i: tuple(0 for _ in shape))
    return pl.pallas_call(
        _mla_prep_kernel, grid=(n // tm,),
        in_specs=[pl.BlockSpec((tm, LOWRANK_W), lambda i: (i, LOWRANK_OFF // LOWRANK_W)),
                  pl.BlockSpec((tm, 128), lambda i: (i, 0)),
                  full((1, Q_RANK)), full((1, KV_RANK)), full((Q_RANK, H_C * DK_C)), full((KV_RANK, H_C * DK_C))],
        out_specs=[pl.BlockSpec((tm, H_C * DK_C), lambda i: (i, 0)),
                   pl.BlockSpec((tm, H_C * DK_C), lambda i: (i, 0)),
                   pl.BlockSpec((tm, H_C * DH_V), lambda i: (i, 0))],
        out_shape=[jax.ShapeDtypeStruct((n, H_C * DK_C), BF16), jax.ShapeDtypeStruct((n, H_C * DK_C), BF16),
                   jax.ShapeDtypeStruct((n, H_C * DH_V), BF16)],
        compiler_params=_cparams(("parallel",)), name="mla_prep",
    )(p2, tab, g_cq.reshape(1, -1), g_ckv.reshape(1, -1), wq, wkv)


def _merge_kernel(ya_ref, yb_ref, yc_ref, za_ref, zb_ref, zc_ref, ga_ref, gb_ref, gc_ref, h_ref,
                  pa_ref, pb_ref, pc_ref, wo_ref, lg_ref, lb_ref, hf_ref, hb_ref, *, alpha):
    def front(rs):
        def branch(y_ref, z_ref, g_ref, p_ref):
            hz = z_ref[rs, :] * 0.5
            y = y_ref[rs, :] * (hz + hz * jnp.tanh(hz))
            gate = 0.5 + 0.5 * jnp.tanh(g_ref[rs, :] * 0.5)
            return gate.astype(F32) * jnp.dot(y, p_ref[...], preferred_element_type=F32)

        return (branch(ya_ref, za_ref, ga_ref, pa_ref) + branch(yb_ref, zb_ref, gb_ref, pb_ref)
                + branch(yc_ref, zc_ref, gc_ref, pc_ref)).astype(BF16)

    def back(rs, merged):
        out = jnp.dot(merged, wo_ref[...], preferred_element_type=F32)
        r = alpha * h_ref[rs, :] + out
        mu = jnp.mean(r, axis=-1, keepdims=True)
        rc = r - mu
        var = jnp.mean(rc * rc, axis=-1, keepdims=True)
        y = rc * lax.rsqrt(var + LN_EPS) * lg_ref[...] + lb_ref[...]
        hf_ref[rs, :] = y
        hb_ref[rs, :] = y.astype(BF16)

    tm = h_ref.shape[0]
    strips = [slice(r, r + tm // MERGE_STRIPS) for r in range(0, tm, tm // MERGE_STRIPS)]
    cur = front(strips[0])
    for n, rs in enumerate(strips):
        nxt = front(strips[n + 1]) if n + 1 < len(strips) else None
        back(rs, cur)
        cur = nxt


def _merge(ya, yb, yc, p2, hf, pa, pb, pc, wo, ln_g, ln_b, alpha):
    n = hf.shape[0]
    tm = min(512, n)
    row = pl.BlockSpec((tm, D_MODEL), lambda i: (i, 0))
    colblk = lambda c: pl.BlockSpec((tm, 1024), lambda i, c=c: (i, c))
    wfull = pl.BlockSpec((1024, D_MODEL), lambda i: (0, 0), pipeline_mode=pl.Buffered(1))
    vec = pl.BlockSpec((1, D_MODEL), lambda i: (0, 0))
    return pl.pallas_call(
        functools.partial(_merge_kernel, alpha=alpha), grid=(n // tm,),
        in_specs=[row, row, row, colblk(COL_ZA), colblk(COL_ZB), colblk(COL_ZC),
                  colblk(COL_GA), colblk(COL_GB), colblk(COL_GC), row,
                  wfull, wfull, wfull, wfull, vec, vec],
        out_specs=[row, row],
        out_shape=[jax.ShapeDtypeStruct((n, D_MODEL), F32), jax.ShapeDtypeStruct((n, D_MODEL), BF16)],
        compiler_params=_cparams(("parallel",)), name="merge",
    )(ya, yb, yc, p2, p2, p2, p2, p2, p2, hf, pa, pb, pc, wo, ln_g.reshape(1, -1), ln_b.reshape(1, -1))


def _rot_cols(w):
    half = w.shape[-1] // 2
    return jnp.concatenate([-w[..., half:], w[..., :half]], axis=-1)


def _layer_weights(w_in, b_ia, b_fa, b_fb, w_uq, w_ukv):
    depth = w_in.shape[0]
    idx = []
    acc = 0
    for s in IN_SPLITS[:-1]:
        acc += s
        idx.append(acc)
    (w_ca, w_va, w_oa, w_ia, w_fa, w_za, w_qb, w_kb, w_vb, w_fb, w_zb,
     w_cq, w_ckv, w_kr, w_zc, w_ga, w_gb, w_gc) = jnp.split(w_in, idx, axis=-1)
    w_qb = w_qb * (DH_B ** -0.5 * LOG2E)
    w_big = jnp.concatenate([w.astype(BF16) for w in (w_ca, w_va, w_oa, w_za, w_qb, w_kb, w_vb, w_zb, w_zc, w_ga,
                                                      w_gb, w_gc, w_ckv, w_cq, w_kr, _rot_cols(w_kr))], axis=-1)
    zpad = lambda k: jnp.zeros((depth, D_MODEL, k), F32)
    w_f = jnp.concatenate([w_fa, w_fa, w_fb, zpad(112)], axis=-1)
    w_i = jnp.concatenate([zpad(4), w_ia, zpad(120)], axis=-1)
    b_f = jnp.concatenate([b_fa, b_fa, b_fb, jnp.zeros((depth, 112), F32)], axis=-1)
    b_i = jnp.concatenate([jnp.zeros((depth, 4), F32), b_ia, jnp.zeros((depth, 120), F32)], axis=-1)
    gw = (jnp.concatenate([w_f, w_i], axis=-1).astype(BF16),
          jnp.swapaxes(jnp.concatenate([w_f[..., :16], w_i[..., :16]], axis=-1), -1, -2).astype(BF16),
          b_f[:, :16, None], b_i[:, :16, None], b_f[:, None, :], b_i[:, None, :])
    wq3 = w_uq.reshape(depth, Q_RANK, H_C, DH_NOPE + DH_ROPE) * ((DH_NOPE + DH_ROPE) ** -0.5 * LOG2E)
    wq_rope = wq3[..., DH_NOPE:]
    wq = jnp.concatenate([wq3, _rot_cols(wq_rope)], axis=-1).reshape(depth, Q_RANK, H_C * DK_C).astype(BF16)
    return w_big, gw, wq, w_ukv.astype(BF16)


def kernel(x, positions, ln_in_g, ln_in_b, w_in, conv_w, conv_b, w_qa, w_ka, b_ia, b_fa, norm_ga, b_fb,
           g_cq, w_uq, g_ckv, w_ukv, p_a, p_b, p_c, w_out, ln_g, ln_b):
    bsz, t, _ = x.shape
    n = bsz * t
    depth = w_in.shape[0]
    alpha = (2.0 * depth) ** 0.25

    w_big, gw, wq, wkv = _layer_weights(w_in, b_ia, b_fa, b_fb, w_uq, w_ukv)
    wqa, wka, pa, pb, pc, wo = (w.astype(BF16) for w in (w_qa, w_ka, p_a, p_b, p_c, w_out))
    tab = _rope_table(positions)
    hf, hb = _input_ln(x.reshape(n, D_MODEL), ln_in_g, ln_in_b)
    for l in range(depth):
        p2 = _in_proj(hb, w_big[l])
        p3 = p2.reshape(bsz, t, N_BIG)
        rows, cols, kb = _gates(hb.reshape(bsz, t, D_MODEL), tuple(w[l] for w in gw))
        ya = _mlstm(p3, rows, cols, conv_w[l], conv_b[l], wqa[l], wka[l], norm_ga[l])
        yb = _flash(p3, p3, kb, p3, n_heads=H_B, dk=DH_B, q_blk=COL_QB * 8, k_blk=COL_KB * 8, v_blk=COL_VB * 8)
        qc, kc, vc = _mla_prep(p2, tab, g_cq[l], g_ckv[l], wq[l], wkv[l])
        yc = _flash(qc.reshape(bsz, t, -1), kc.reshape(bsz, t, -1), None, vc.reshape(bsz, t, -1),
                    n_heads=H_C, dk=DK_C, q_blk=0, k_blk=0, v_blk=0)
        hf, hb = _merge(ya.reshape(n, -1), yb.reshape(n, -1), yc.reshape(n, -1), p2, hf,
                        pa[l], pb[l], pc[l], wo[l], ln_g[l], ln_b[l], alpha)
    return hf.reshape(bsz, t, D_MODEL)
```

```python
import functools
import math

import jax
import jax.numpy as jnp
from jax import lax
from jax.experimental import pallas as pl
from jax.experimental.pallas import tpu as pltpu

F32 = jnp.float32
BF16 = jnp.bfloat16

D_MODEL = 1024
H_A, DH_A, CONV_K = 4, 256, 4
H_B, DH_B = 8, 128
H_C, DH_NOPE, DH_ROPE, DH_V = 8, 128, 64, 128
Q_RANK, KV_RANK = 384, 256
ROPE_BASE = 10000.0
LN_EPS = 1e-5
RMS_EPS = 1e-6
IN_SPLITS = (1024, 1024, 1024, 4, 4, 1024,
             1024, 1024, 1024, 8, 1024,
             Q_RANK, KV_RANK, DH_ROPE, 1024,
             1024, 1024, 1024)

COL_ZA, COL_ZB, COL_ZC, COL_GA, COL_GB, COL_GC, COL_CA, COL_VA, COL_OA, COL_QB, COL_KB, COL_VB = range(12)
N_GATE_COLS = 6
LOWRANK_OFF = 12 * 1024
LOWRANK_W = 768
N_BIG = LOWRANK_OFF + LOWRANK_W
DK_C = 256

MLSTM_CHUNK = 256
GATES_SEQS_PER_STEP = 4
FLASH_TQ = 1024
FLASH_TK = 512
FLASH_HEADS_PER_STEP = 4
MERGE_STRIPS = 2
VMEM_LIMIT = 56 * 1024 * 1024
LOG2E = math.log2(math.e)

NT_DIMS = (((1,), (1,)), ((), ()))
TN_DIMS = (((0,), (0,)), ((), ()))


def _cparams(sem):
    return pltpu.CompilerParams(dimension_semantics=sem, vmem_limit_bytes=VMEM_LIMIT)


def _sigmoid(x):
    return 0.5 + 0.5 * jnp.tanh(0.5 * x)


def _silu(x):
    hx = 0.5 * x
    return hx + hx * jnp.tanh(hx)


def _log_sigmoid(x):
    return jnp.minimum(x, 0.0) - jnp.log1p(jnp.exp(-jnp.abs(x)))


def _split3(x):
    hi = x.astype(BF16)
    r = x - hi.astype(F32)
    mid = r.astype(BF16)
    lo = (r - mid.astype(F32)).astype(BF16)
    return hi, mid, lo


def _ln_kernel(x_ref, g_ref, b_ref, hf_ref, hb_ref):
    x = x_ref[...]
    mu = jnp.mean(x, axis=-1, keepdims=True)
    xc = x - mu
    var = jnp.mean(xc * xc, axis=-1, keepdims=True)
    y = xc * lax.rsqrt(var + LN_EPS) * g_ref[...] + b_ref[...]
    hf_ref[...] = y
    hb_ref[...] = y.astype(BF16)


def _input_ln(x2, g, b):
    n = x2.shape[0]
    tm = min(512, n)
    row = pl.BlockSpec((tm, D_MODEL), lambda i: (i, 0))
    vec = pl.BlockSpec((1, D_MODEL), lambda i: (0, 0))
    return pl.pallas_call(
        _ln_kernel, grid=(n // tm,), in_specs=[row, vec, vec], out_specs=[row, row],
        out_shape=[jax.ShapeDtypeStruct((n, D_MODEL), F32), jax.ShapeDtypeStruct((n, D_MODEL), BF16)],
        compiler_params=_cparams(("parallel",)), name="input_ln",
    )(x2, g.reshape(1, -1), b.reshape(1, -1))


def _rope_kernel(pos_ref, freq_ref, tab_ref):
    ang = pos_ref[...].astype(F32) * freq_ref[...]
    lane = lax.broadcasted_iota(jnp.int32, ang.shape, 1)
    tab_ref[...] = jnp.where(lane < DH_ROPE, jnp.cos(ang), jnp.sin(ang))


def _rope_table(positions):
    n = positions.size
    tm = min(1024, n)
    inv_freq = ROPE_BASE ** (-jnp.arange(0, DH_ROPE, 2, dtype=F32) / DH_ROPE)
    freq = jnp.tile(inv_freq, 4).reshape(1, 128)
    return pl.pallas_call(
        _rope_kernel, grid=(n // tm,),
        in_specs=[pl.BlockSpec((tm, 1), lambda i: (i, 0)), pl.BlockSpec((1, 128), lambda i: (0, 0))],
        out_specs=pl.BlockSpec((tm, 128), lambda i: (i, 0)),
        out_shape=jax.ShapeDtypeStruct((n, 128), F32),
        compiler_params=_cparams(("parallel",)), name="rope_table",
    )(positions.reshape(n, 1), freq)


def _matmul_kernel(x_ref, w_ref, o_ref):
    o_ref[...] = jnp.dot(x_ref[...], w_ref[...], preferred_element_type=F32).astype(o_ref.dtype)


def _in_proj(hb, w_big):
    n = hb.shape[0]
    tm = min(1024, n)
    tn = N_BIG // 3
    return pl.pallas_call(
        _matmul_kernel, grid=(n // tm, N_BIG // tn),
        in_specs=[pl.BlockSpec((tm, D_MODEL), lambda i, j: (i, 0)),
                  pl.BlockSpec((D_MODEL, tn), lambda i, j: (0, j))],
        out_specs=pl.BlockSpec((tm, tn), lambda i, j: (i, j)),
        out_shape=jax.ShapeDtypeStruct((n, N_BIG), BF16),
        compiler_params=_cparams(("parallel", "arbitrary")), name="in_proj",
    )(hb, w_big)


def _gates_kernel(h_ref, wfi_ref, wfit_ref, bfc_ref, bic_ref, bfr_ref, bir_ref,
                  rows_ref, cols_ref, kb_ref, carry_ref, *, tg, chunk, nb):
    t = pl.program_id(1)

    @pl.when(t == 0)
    def _():
        carry_ref[...] = jnp.zeros_like(carry_ref)

    seq_rows = lambda a, i: a[i * tg:(i + 1) * tg]
    seq_lanes = lambda a, i, w: a[:, i * w:(i + 1) * w]
    hb = h_ref[...].reshape(nb * tg, D_MODEL)
    r_i = lax.broadcasted_iota(jnp.int32, (tg, tg), 0)
    c_i = lax.broadcasted_iota(jnp.int32, (tg, tg), 1)
    same_chunk = (r_i // chunk) == (c_i // chunk)
    ind = lambda mask: jnp.where(mask, 1.0, 0.0).astype(BF16)
    upper_bd = ind((r_i <= c_i) & same_chunk)
    lower = ind(r_i >= c_i)
    lower_bd = ind((r_i >= c_i) & same_chunk)

    g_rows = lax.dot_general(wfit_ref[...], hb, NT_DIMS, preferred_element_type=F32)
    gf = g_rows[0:16] + bfc_ref[...]
    gi = g_rows[16:32] + bic_ref[...]
    lf3 = jnp.concatenate(_split3(_log_sigmoid(gf)), axis=0)
    y3 = jnp.dot(jnp.concatenate([seq_lanes(lf3, i, tg) for i in range(nb)], axis=0), upper_bd,
                 preferred_element_type=F32)
    row = lax.broadcasted_iota(jnp.int32, (16, tg), 0)
    for i in range(nb):
        y_bd = y3[48 * i:48 * i + 16] + y3[48 * i + 16:48 * i + 32] + y3[48 * i + 32:48 * i + 48]
        rows_ref[i] = jnp.where(row < 4, y_bd, seq_lanes(gi, i, tg) - y_bd)[0:8]

    g_cols = jnp.dot(hb, wfi_ref[...], preferred_element_type=F32)
    gfc = g_cols[:, 0:128] + bfr_ref[...]
    gic = g_cols[:, 128:256] + bir_ref[...]
    lfc3 = jnp.concatenate(_split3(_log_sigmoid(gfc)), axis=1)
    lfc3 = jnp.concatenate([seq_rows(lfc3, i) for i in range(nb)], axis=1)
    y_bd_c = jnp.dot(lower_bd, lfc3, preferred_element_type=F32)
    y_full_c = jnp.dot(lower, lfc3, preferred_element_type=F32)
    sum3 = lambda a: a[:, 0:128] + a[:, 128:256] + a[:, 256:384]
    col = lax.broadcasted_iota(jnp.int32, (tg, 128), 1)
    parts = []
    for i in range(nb):
        yc = sum3(seq_lanes(y_bd_c, i, 384))
        cols_ref[i] = jnp.where(col < 4, yc, seq_rows(gic, i) - yc)
        cum = sum3(seq_lanes(y_full_c, i, 384)) + carry_ref[i, 0:1, :]
        carry_ref[i] = jnp.broadcast_to(cum[tg - 1:tg, :], carry_ref.shape[1:])
        parts.append(jnp.concatenate(_split3(cum * LOG2E), axis=1))
    sr = lax.broadcasted_iota(jnp.int32, (3 * 128, 128), 0)
    sc = lax.broadcasted_iota(jnp.int32, (3 * 128, 128), 1)
    scatter = ind((sr % 128 == 8 + sc // 3) & (sr // 128 == sc % 3) & (sc < 3 * H_B))
    kb = jnp.dot(jnp.concatenate(parts, axis=0), scatter, preferred_element_type=F32).astype(BF16)
    for i in range(nb):
        kb_ref[i] = seq_rows(kb, i)


def _gates(hb3, gw):
    bsz, t, _ = hb3.shape
    tg = min(512, t)
    nb = math.gcd(GATES_SEQS_PER_STEP, bsz)
    full = lambda shape: pl.BlockSpec(shape, lambda b, i: tuple(0 for _ in shape))
    return pl.pallas_call(
        functools.partial(_gates_kernel, tg=tg, chunk=min(MLSTM_CHUNK, t), nb=nb),
        grid=(bsz // nb, t // tg),
        in_specs=[pl.BlockSpec((nb, tg, D_MODEL), lambda b, i: (b, i, 0)),
                  full((D_MODEL, 256)), full((32, D_MODEL)),
                  full((16, 1)), full((16, 1)), full((1, 128)), full((1, 128))],
        out_specs=[pl.BlockSpec((nb, 8, tg), lambda b, i: (b, 0, i)),
                   pl.BlockSpec((nb, tg, 128), lambda b, i: (b, i, 0)),
                   pl.BlockSpec((nb, tg, 128), lambda b, i: (b, i, 0))],
        out_shape=[jax.ShapeDtypeStruct((bsz, 8, t), F32), jax.ShapeDtypeStruct((bsz, t, 128), F32),
                   jax.ShapeDtypeStruct((bsz, t, 128), BF16)],
        scratch_shapes=[pltpu.VMEM((nb, 8, 128), F32)],
        compiler_params=_cparams(("parallel", "arbitrary")), name="gates",
    )(hb3, *gw)


def _mlstm_kernel(c_ref, v_ref, o_ref, rows_ref, cols_ref, cw_ref, cb_ref, wq_ref, wk_ref, ng_ref,
                  y_ref, cbuf, c_state, m_state, *, chunk):
    L = chunk
    t = pl.program_id(1)

    @pl.when(t == 0)
    def _():
        cbuf[0:8, :] = jnp.zeros((8, D_MODEL), F32)
        c_state[...] = jnp.zeros_like(c_state)
        m_state[...] = jnp.zeros_like(m_state)

    cbuf[8:8 + L, :] = c_ref[0].astype(F32)

    nc = L // 128
    lanes = lambda c: slice(c * 128, (c + 1) * 128)
    r_i = lax.broadcasted_iota(jnp.int32, (L, 128), 0)
    c_i = lax.broadcasted_iota(jnp.int32, (L, 128), 1)
    causal = [c_i + c * 128 <= r_i for c in range(nc)]
    ones_blk = jnp.ones((L, 128), BF16)
    for h in range(H_A):
        hs = slice(h * DH_A, (h + 1) * DH_A)
        conv = cb_ref[:, hs]
        for j in range(CONV_K):
            conv = conv + cbuf[8 - (CONV_K - 1) + j:8 - (CONV_K - 1) + j + L, hs] * cw_ref[j:j + 1, hs]
        uh = _silu(conv).astype(BF16)
        q = jnp.dot(uh, wq_ref[h], preferred_element_type=F32)
        k = jnp.dot(uh, wk_ref[h], preferred_element_type=F32) * (DH_A ** -0.5)
        qb = q.astype(BF16)
        kb = k.astype(BF16)
        v_aug = jnp.concatenate([v_ref[0, :, hs], ones_blk], axis=1)
        b_rep = jnp.broadcast_to(cols_ref[0, :, h:h + 1], (L, 128))
        c_rep = jnp.broadcast_to(cols_ref[0, :, H_A + h:H_A + h + 1], (L, 128))
        b_row = rows_ref[0, h:h + 1, :]
        c_row = rows_ref[0, H_A + h:H_A + h + 1, :]
        m_prev = m_state[h, 0:1, :]

        dm = [jnp.where(causal[c], b_rep + c_row[:, lanes(c)], -jnp.inf) for c in range(nc)]
        dmax = dm[0]
        for c in range(1, nc):
            dmax = jnp.maximum(dmax, dm[c])
        inter = b_rep + m_prev
        m_t = jnp.maximum(jnp.max(dmax, axis=1, keepdims=True), inter)
        qk = lax.dot_general(qb, kb, NT_DIMS, preferred_element_type=F32)
        s = jnp.concatenate([(qk[:, lanes(c)] * jnp.exp(dm[c] - m_t)).astype(BF16) for c in range(nc)], axis=1)
        decay = jnp.exp(inter - m_t)
        c_old = c_state[h]
        inter_acc = jnp.dot(qb, c_old.astype(BF16), preferred_element_type=F32)
        acc = jnp.dot(s, v_aug, preferred_element_type=F32) + jnp.tile(decay, (1, DH_A // 128 + 1)) * inter_acc
        den = acc[:, DH_A:]
        inv = 1.0 / jnp.maximum(jnp.abs(den), jnp.exp(-m_t))
        hh = acc[:, :DH_A] * jnp.tile(inv, (1, DH_A // 128))

        g = b_row[:, L - 1:L]
        a_rep = g + c_rep
        m_new = jnp.maximum(g + m_prev, jnp.max(a_rep, axis=0, keepdims=True))
        kw = (k * jnp.tile(jnp.exp(a_rep - m_new), (1, DH_A // 128))).astype(BF16)
        sc = jnp.exp(g + m_prev - m_new)
        c_state[h] = (jnp.tile(sc, (1, DH_A // 128 + 1)) * c_old
                      + lax.dot_general(kw, v_aug, TN_DIMS, preferred_element_type=F32))
        m_state[h] = jnp.broadcast_to(m_new, m_state.shape[1:])

        mu = jnp.mean(hh, axis=1, keepdims=True)
        hc = hh - mu
        var = jnp.mean(hc * hc, axis=1, keepdims=True)
        hn = hc * lax.rsqrt(var + LN_EPS) * ng_ref[:, hs]
        og = _sigmoid(o_ref[0, :, hs])
        y_ref[0, :, hs] = (og.astype(F32) * hn).astype(BF16)
    cbuf[0:8, :] = cbuf[L:L + 8, :]


def _mlstm(p3, rows, cols, conv_w, conv_b, wq, wk, norm_g):
    bsz, t, _ = p3.shape
    L = min(MLSTM_CHUNK, t)
    colblk = lambda c: pl.BlockSpec((1, L, 1024), lambda b, i, c=c: (b, i, c))
    full = lambda shape: pl.BlockSpec(shape, lambda b, i: tuple(0 for _ in shape))
    return pl.pallas_call(
        functools.partial(_mlstm_kernel, chunk=L),
        grid=(bsz, t // L),
        in_specs=[colblk(COL_CA), colblk(COL_VA), colblk(COL_OA),
                  pl.BlockSpec((1, 8, L), lambda b, i: (b, 0, i)),
                  pl.BlockSpec((1, L, 128), lambda b, i: (b, i, 0)),
                  full((CONV_K, 1024)), full((1, 1024)), full((H_A, DH_A, DH_A)), full((H_A, DH_A, DH_A)),
                  full((1, 1024))],
        out_specs=pl.BlockSpec((1, L, 1024), lambda b, i: (b, i, 0)),
        out_shape=jax.ShapeDtypeStruct((bsz, t, 1024), BF16),
        scratch_shapes=[pltpu.VMEM((8 + L, 1024), F32), pltpu.VMEM((H_A, DH_A, DH_A + 128), F32),
                        pltpu.VMEM((H_A, 8, 128), F32)],
        compiler_params=_cparams(("parallel", "arbitrary")), name="mlstm",
    )(p3, p3, p3, rows, cols, conv_w, conv_b.reshape(1, -1), wq, wk, norm_g.reshape(1, -1))


def _flash_kernel(*refs, tq, tk, dk, heads, has_bias):
    if has_bias:
        q_ref, k_ref, kb_ref, v_ref, o_ref, m_sc, l_sc, acc_sc = refs
    else:
        q_ref, k_ref, v_ref, o_ref, m_sc, l_sc, acc_sc = refs
        kb_ref = None
    i = pl.program_id(2)
    qs = []
    for g in range(heads):
        q = q_ref[0, :, g * dk:(g + 1) * dk]
        if has_bias:
            lane = lax.broadcasted_iota(jnp.int32, (tq, 128), 1) - 3 * (pl.program_id(1) * heads + g)
            q = jnp.concatenate([q, jnp.where((lane >= 0) & (lane < 3), -1.0, 0.0).astype(BF16)], axis=1)
        qs.append(q)

    def scores(g, key0, nk, r0, nr, mask_off):
        ks = pl.ds(key0, nk)
        kj = k_ref[0, ks, g * dk:(g + 1) * dk]
        if has_bias:
            kj = jnp.concatenate([kj, kb_ref[0, ks, :]], axis=1)
        s = lax.dot_general(qs[g][r0:r0 + nr], kj, NT_DIMS, preferred_element_type=F32)
        if mask_off is not None:
            r_i = r0 + lax.broadcasted_iota(jnp.int32, (nr, nk), 0)
            c_i = mask_off + lax.broadcasted_iota(jnp.int32, (nr, nk), 1)
            s = jnp.where(c_i <= r_i, s, -jnp.inf)
        return s

    def update(g, s, key0, nk, r0, nr, first):
        rs = slice(r0, r0 + nr)
        vj = v_ref[0, pl.ds(key0, nk), g * 128:(g + 1) * 128]
        s_max = jnp.max(s, axis=1, keepdims=True)
        if first:
            m_new = jnp.broadcast_to(s_max, (nr, 128))
            l_part = jnp.zeros((nr, 128), F32)
        else:
            m_prev = m_sc[g, rs]
            m_new = jnp.maximum(m_prev, s_max)
            alpha = jnp.exp2(m_prev - m_new)
            l_part = alpha * l_sc[g, rs]
        ps = []
        for c in range(nk // 128):
            p_c = jnp.exp2(s[:, c * 128:(c + 1) * 128] - m_new)
            l_part = l_part + p_c
            ps.append(p_c.astype(BF16))
        pv = jnp.dot(jnp.concatenate(ps, axis=1), vj, preferred_element_type=F32)
        acc_sc[g, rs] = pv if first else alpha * acc_sc[g, rs] + pv
        l_sc[g, rs] = l_part
        m_sc[g, rs] = m_new

    def finish(g):
        l = jnp.sum(l_sc[g], axis=1, keepdims=True)
        o_ref[0, :, g * 128:(g + 1) * 128] = (acc_sc[g] / l).astype(o_ref.dtype)

    def sweep(key0, nk, r0, nr, mask_off, first=False, last=False):
        s_cur = scores(0, key0, nk, r0, nr, mask_off)
        for g in range(heads):
            s_next = scores(g + 1, key0, nk, r0, nr, mask_off) if g + 1 < heads else None
            update(g, s_cur, key0, nk, r0, nr, first)
            if last:
                finish(g)
            s_cur = s_next

    nsub = tq // tk

    def body(j, carry):
        sweep(pl.multiple_of(j * tk, tk), tk, 0, tq, None)
        return carry

    def diag(d, **kw):
        sweep(pl.multiple_of(i * tq + d * tk, tk), tk, d * tk, tq - d * tk, d * tk, **kw)

    diag(0, first=True)
    lax.fori_loop(0, i * nsub, body, 0)
    for d in range(1, nsub):
        diag(d, last=(d == nsub - 1))
    if nsub == 1:
        for g in range(heads):
            finish(g)


def _flash(q_arr, k_arr, kb_arr, v_arr, *, n_heads, dk, q_blk, k_blk, v_blk):
    bsz, t, _ = q_arr.shape
    tq = min(FLASH_TQ, t)
    tk = min(FLASH_TK, t)
    g = FLASH_HEADS_PER_STEP
    hp = n_heads // g
    in_specs = [pl.BlockSpec((1, tq, g * dk), lambda b, h, i: (b, i, q_blk // g + h)),
                pl.BlockSpec((1, t, g * dk), lambda b, h, i: (b, 0, k_blk // g + h))]
    args = [q_arr, k_arr]
    if kb_arr is not None:
        in_specs.append(pl.BlockSpec((1, t, 128), lambda b, h, i: (b, 0, 0)))
        args.append(kb_arr)
    in_specs.append(pl.BlockSpec((1, t, g * 128), lambda b, h, i: (b, 0, v_blk // g + h)))
    args.append(v_arr)
    return pl.pallas_call(
        functools.partial(_flash_kernel, tq=tq, tk=tk, dk=dk, heads=g, has_bias=kb_arr is not None),
        grid=(bsz, hp, t // tq),
        in_specs=in_specs,
        out_specs=pl.BlockSpec((1, tq, g * 128), lambda b, h, i: (b, i, h)),
        out_shape=jax.ShapeDtypeStruct((bsz, t, n_heads * 128), BF16),
        scratch_shapes=[pltpu.VMEM((g, tq, 128), F32), pltpu.VMEM((g, tq, 128), F32), pltpu.VMEM((g, tq, 128), F32)],
        compiler_params=_cparams(("parallel", "parallel", "arbitrary")),
        name="flash_fox" if kb_arr is not None else "flash_mla",
    )(*args)


def _mla_prep_kernel(c_ref, tab_ref, gq_ref, gkv_ref, wq_ref, wkv_ref, q_out, k_out, v_out):
    c = c_ref[...].astype(F32)
    ckv = c[:, 0:KV_RANK]
    cq = c[:, KV_RANK:KV_RANK + Q_RANK]
    kr = c[:, KV_RANK + Q_RANK:]

    def rms(x, g):
        ms = jnp.mean(x * x, axis=-1, keepdims=True)
        return (x * lax.rsqrt(ms + RMS_EPS) * g).astype(BF16)

    qf = jnp.dot(rms(cq, gq_ref[...]), wq_ref[...], preferred_element_type=F32)
    kvf = jnp.dot(rms(ckv, gkv_ref[...]), wkv_ref[...], preferred_element_type=F32)
    tab = tab_ref[...]

    def rope(xx):
        pr = xx * tab
        return pr + pltpu.roll(pr, DH_ROPE, 1)

    lane = lax.broadcasted_iota(jnp.int32, kr.shape, 1)
    k_rope = jnp.where(lane < DH_ROPE, rope(kr), 0.0).astype(BF16)
    for h in range(H_C):
        lo = h * DK_C
        q_out[:, lo:lo + 128] = qf[:, lo:lo + 128].astype(BF16)
        q_out[:, lo + 128:lo + 256] = rope(qf[:, lo + 128:lo + 256]).astype(BF16)
        k_out[:, lo:lo + 128] = kvf[:, lo:lo + 128].astype(BF16)
        k_out[:, lo + 128:lo + 256] = k_rope
        v_out[:, h * DH_V:(h + 1) * DH_V] = kvf[:, lo + 128:lo + 256].astype(BF16)


def _mla_prep(p2, tab, g_cq, g_ckv, wq, wkv):
    n = p2.shape[0]
    tm = min(512, n)
    full = lambda shape: pl.BlockSpec(shape, lambda i: tuple(0 for _ in shape))
    return pl.pallas_call(
        _mla_prep_kernel, grid=(n // tm,),
        in_specs=[pl.BlockSpec((tm, LOWRANK_W), lambda i: (i, LOWRANK_OFF // LOWRANK_W)),
                  pl.BlockSpec((tm, 128), lambda i: (i, 0)),
                  full((1, Q_RANK)), full((1, KV_RANK)), full((Q_RANK, H_C * DK_C)), full((KV_RANK, H_C * DK_C))],
        out_specs=[pl.BlockSpec((tm, H_C * DK_C), lambda i: (i, 0)),
                   pl.BlockSpec((tm, H_C * DK_C), lambda i: (i, 0)),
                   pl.BlockSpec((tm, H_C * DH_V), lambda i: (i, 0))],
        out_shape=[jax.ShapeDtypeStruct((n, H_C * DK_C), BF16), jax.ShapeDtypeStruct((n, H_C * DK_C), BF16),
                   jax.ShapeDtypeStruct((n, H_C * DH_V), BF16)],
        compiler_params=_cparams(("parallel",)), name="mla_prep",
    )(p2, tab, g_cq.reshape(1, -1), g_ckv.reshape(1, -1), wq, wkv)


def _merge_kernel(ya_ref, yb_ref, yc_ref, zg_ref, h_ref,
                  pa_ref, pb_ref, pc_ref, wo_ref, lg_ref, lb_ref, hf_ref, hb_ref, *, alpha):
    def front(rs):
        def branch(y_ref, col_z, col_g, p_ref):
            hz = zg_ref[rs, col_z * 1024:(col_z + 1) * 1024] * 0.5
            y = y_ref[rs, :] * (hz + hz * jnp.tanh(hz))
            gate = 0.5 + 0.5 * jnp.tanh(zg_ref[rs, col_g * 1024:(col_g + 1) * 1024] * 0.5)
            return gate.astype(F32) * jnp.dot(y, p_ref[...], preferred_element_type=F32)

        return (branch(ya_ref, COL_ZA, COL_GA, pa_ref) + branch(yb_ref, COL_ZB, COL_GB, pb_ref)
                + branch(yc_ref, COL_ZC, COL_GC, pc_ref)).astype(BF16)

    def back(rs, merged):
        out = jnp.dot(merged, wo_ref[...], preferred_element_type=F32)
        r = alpha * h_ref[rs, :] + out
        mu = jnp.mean(r, axis=-1, keepdims=True)
        rc = r - mu
        var = jnp.mean(rc * rc, axis=-1, keepdims=True)
        y = rc * lax.rsqrt(var + LN_EPS) * lg_ref[...] + lb_ref[...]
        hf_ref[rs, :] = y
        hb_ref[rs, :] = y.astype(BF16)

    tm = h_ref.shape[0]
    strips = [slice(r, r + tm // MERGE_STRIPS) for r in range(0, tm, tm // MERGE_STRIPS)]
    cur = front(strips[0])
    for n, rs in enumerate(strips):
        nxt = front(strips[n + 1]) if n + 1 < len(strips) else None
        back(rs, cur)
        cur = nxt


def _merge(ya, yb, yc, p2, hf, pa, pb, pc, wo, ln_g, ln_b, alpha):
    n = hf.shape[0]
    tm = min(512, n)
    row = pl.BlockSpec((tm, D_MODEL), lambda i: (i, 0))
    wfull =pl.BlockSpec((1024, D_MODEL), lambda i: (0, 0), pipeline_mode=pl.Buffered(1))
    vec = pl.BlockSpec((1, D_MODEL), lambda i: (0, 0))
    return pl.pallas_call(
        functools.partial(_merge_kernel, alpha=alpha), grid=(n // tm,),
        in_specs=[row, row, row, pl.BlockSpec((tm, N_GATE_COLS * 1024), lambda i: (i, 0)), row,
                  wfull, wfull, wfull, wfull, vec, vec],
        out_specs=[row, row],
        out_shape=[jax.ShapeDtypeStruct((n, D_MODEL), F32), jax.ShapeDtypeStruct((n, D_MODEL), BF16)],
        compiler_params=_cparams(("parallel",)), name="merge",
    )(ya, yb, yc, p2, hf, pa, pb, pc, wo, ln_g.reshape(1, -1), ln_b.reshape(1, -1))


def _rot_cols(w):
    half = w.shape[-1] // 2
    return jnp.concatenate([-w[..., half:], w[..., :half]], axis=-1)


def _layer_weights(w_in, b_ia, b_fa, b_fb, w_uq, w_ukv):
    depth = w_in.shape[0]
    idx = []
    acc = 0
    for s in IN_SPLITS[:-1]:
        acc += s
        idx.append(acc)
    (w_ca, w_va, w_oa, w_ia, w_fa, w_za, w_qb, w_kb, w_vb, w_fb, w_zb,
     w_cq, w_ckv, w_kr, w_zc, w_ga, w_gb, w_gc) = jnp.split(w_in, idx, axis=-1)
    w_qb = w_qb * (DH_B ** -0.5 * LOG2E)
    w_big = jnp.concatenate([w.astype(BF16) for w in (w_za, w_zb, w_zc, w_ga, w_gb, w_gc, w_ca, w_va, w_oa, w_qb,
                                                      w_kb, w_vb, w_ckv, w_cq, w_kr, _rot_cols(w_kr))], axis=-1)
    zpad = lambda k: jnp.zeros((depth, D_MODEL, k), F32)
    w_f = jnp.concatenate([w_fa, w_fa, w_fb, zpad(112)], axis=-1)
    w_i = jnp.concatenate([zpad(4), w_ia, zpad(120)], axis=-1)
    b_f = jnp.concatenate([b_fa, b_fa, b_fb, jnp.zeros((depth, 112), F32)], axis=-1)
    b_i = jnp.concatenate([jnp.zeros((depth, 4), F32), b_ia, jnp.zeros((depth, 120), F32)], axis=-1)
    gw = (jnp.concatenate([w_f, w_i], axis=-1).astype(BF16),
          jnp.swapaxes(jnp.concatenate([w_f[..., :16], w_i[..., :16]], axis=-1), -1, -2).astype(BF16),
          b_f[:, :16, None], b_i[:, :16, None], b_f[:, None, :], b_i[:, None, :])
    wq3 = w_uq.reshape(depth, Q_RANK, H_C, DH_NOPE + DH_ROPE) * ((DH_NOPE + DH_ROPE) ** -0.5 * LOG2E)
    wq_rope = wq3[..., DH_NOPE:]
    wq = jnp.concatenate([wq3, _rot_cols(wq_rope)], axis=-1).reshape(depth, Q_RANK, H_C * DK_C).astype(BF16)
    return w_big, gw, wq, w_ukv.astype(BF16)


def kernel(x, positions, ln_in_g, ln_in_b, w_in, conv_w, conv_b, w_qa, w_ka, b_ia, b_fa, norm_ga, b_fb,
           g_cq, w_uq, g_ckv, w_ukv, p_a, p_b, p_c, w_out, ln_g, ln_b):
    bsz, t, _ = x.shape
    n = bsz * t
    depth = w_in.shape[0]
    alpha = (2.0 * depth) ** 0.25

    w_big, gw, wq, wkv = _layer_weights(w_in, b_ia, b_fa, b_fb, w_uq, w_ukv)
    wqa, wka, pa, pb, pc, wo = (w.astype(BF16) for w in (w_qa, w_ka, p_a, p_b, p_c, w_out))
    tab = _rope_table(positions)
    hf, hb = _input_ln(x.reshape(n, D_MODEL), ln_in_g, ln_in_b)
    for l in range(depth):
        p2 = _in_proj(hb, w_big[l])
        p3 = p2.reshape(bsz, t, N_BIG)
        rows, cols, kb = _gates(hb.reshape(bsz, t, D_MODEL), tuple(w[l] for w in gw))
        ya = _mlstm(p3, rows, cols, conv_w[l], conv_b[l], wqa[l], wka[l], norm_ga[l])
        yb = _flash(p3, p3, kb, p3, n_heads=H_B, dk=DH_B, q_blk=COL_QB * 8, k_blk=COL_KB * 8, v_blk=COL_VB * 8)
        qc, kc, vc = _mla_prep(p2, tab, g_cq[l], g_ckv[l], wq[l], wkv[l])
        yc = _flash(qc.reshape(bsz, t, -1), kc.reshape(bsz, t, -1), None, vc.reshape(bsz, t, -1),
                    n_heads=H_C, dk=DK_C, q_blk=0, k_blk=0, v_blk=0)
        hf, hb = _merge(ya.reshape(n, -1), yb.reshape(n, -1), yc.reshape(n, -1), p2, hf,
                        pa[l], pb[l], pc[l], wo[l], ln_g[l], ln_b[l], alpha)
    return hf.reshape(bsz, t, D_MODEL)
```

```python
import functools
import math

import jax
import jax.numpy as jnp
from jax import lax
from jax.experimental import pallas as pl
from jax.experimental.pallas import tpu as pltpu

F32 = jnp.float32
BF16 = jnp.bfloat16

D_MODEL = 1024
H_A, DH_A, CONV_K = 4, 256, 4
H_B, DH_B = 8, 128
H_C, DH_NOPE, DH_ROPE, DH_V = 8, 128, 64, 128
Q_RANK, KV_RANK = 384, 256
ROPE_BASE = 10000.0
LN_EPS = 1e-5
RMS_EPS = 1e-6
IN_SPLITS = (1024, 1024, 1024, 4, 4, 1024,
             1024, 1024, 1024, 8, 1024,
             Q_RANK, KV_RANK, DH_ROPE, 1024,
             1024, 1024, 1024)

COL_ZA, COL_ZB, COL_ZC, COL_GA, COL_GB, COL_GC, COL_CA, COL_VA, COL_OA, COL_QB, COL_KB, COL_VB = range(12)
N_GATE_COLS = 6
LOWRANK_OFF = 12 * 1024
LOWRANK_W = 768
N_BIG = LOWRANK_OFF + LOWRANK_W
DK_C = 256

MLSTM_CHUNK = 256
GATES_SEQS_PER_STEP = 4
FLASH_TQ = 1024
FLASH_TK = 512
FLASH_HEADS_PER_STEP = 4
MERGE_STRIPS = 2
VMEM_LIMIT = 56 * 1024 * 1024
LOG2E = math.log2(math.e)

NT_DIMS = (((1,), (1,)), ((), ()))
TN_DIMS = (((0,), (0,)), ((), ()))


def _cparams(sem):
    return pltpu.CompilerParams(dimension_semantics=sem, vmem_limit_bytes=VMEM_LIMIT)


def _sigmoid(x):
    return 0.5 + 0.5 * jnp.tanh(0.5 * x)


def _silu(x):
    hx = 0.5 * x
    return hx + hx * jnp.tanh(hx)


def _log_sigmoid(x):
    return jnp.minimum(x, 0.0) - jnp.log1p(jnp.exp(-jnp.abs(x)))


def _split3(x):
    hi = x.astype(BF16)
    r = x - hi.astype(F32)
    mid = r.astype(BF16)
    lo = (r - mid.astype(F32)).astype(BF16)
    return hi, mid, lo


def _ln_kernel(x_ref, g_ref, b_ref, hf_ref, hb_ref):
    x = x_ref[...]
    mu = jnp.mean(x, axis=-1, keepdims=True)
    xc = x - mu
    var = jnp.mean(xc * xc, axis=-1, keepdims=True)
    y = xc * lax.rsqrt(var + LN_EPS) * g_ref[...] + b_ref[...]
    hf_ref[...] = y
    hb_ref[...] = y.astype(BF16)


def _input_ln(x2, g, b):
    n = x2.shape[0]
    tm = min(512, n)
    row = pl.BlockSpec((tm, D_MODEL), lambda i: (i, 0))
    vec = pl.BlockSpec((1, D_MODEL), lambda i: (0, 0))
    return pl.pallas_call(
        _ln_kernel, grid=(n // tm,), in_specs=[row, vec, vec], out_specs=[row, row],
        out_shape=[jax.ShapeDtypeStruct((n, D_MODEL), F32), jax.ShapeDtypeStruct((n, D_MODEL), BF16)],
        compiler_params=_cparams(("parallel",)), name="input_ln",
    )(x2, g.reshape(1, -1), b.reshape(1, -1))


def _rope_kernel(pos_ref, freq_ref, tab_ref):
    ang = pos_ref[...].astype(F32) * freq_ref[...]
    lane = lax.broadcasted_iota(jnp.int32, ang.shape, 1)
    tab_ref[...] = jnp.where(lane < DH_ROPE, jnp.cos(ang), jnp.sin(ang))


def _rope_table(positions):
    n = positions.size
    tm = min(1024, n)
    inv_freq = ROPE_BASE ** (-jnp.arange(0, DH_ROPE, 2, dtype=F32) / DH_ROPE)
    freq = jnp.tile(inv_freq, 4).reshape(1, 128)
    return pl.pallas_call(
        _rope_kernel, grid=(n // tm,),
        in_specs=[pl.BlockSpec((tm, 1), lambda i: (i, 0)), pl.BlockSpec((1, 128), lambda i: (0, 0))],
        out_specs=pl.BlockSpec((tm, 128), lambda i: (i, 0)),
        out_shape=jax.ShapeDtypeStruct((n, 128), F32),
        compiler_params=_cparams(("parallel",)), name="rope_table",
    )(positions.reshape(n, 1), freq)


def _matmul_kernel(x_ref, w_ref, o_ref):
    o_ref[...] = jnp.dot(x_ref[...], w_ref[...], preferred_element_type=F32).astype(o_ref.dtype)


def _in_proj(hb, w_big):
    n = hb.shape[0]
    tm = min(1024, n)
    tn = N_BIG // 3
    return pl.pallas_call(
        _matmul_kernel, grid=(N_BIG // tn, n // tm),
        in_specs=[pl.BlockSpec((tm, D_MODEL), lambda j, i: (i, 0)),
                  pl.BlockSpec((D_MODEL, tn), lambda j, i: (0, j))],
        out_specs=pl.BlockSpec((tm, tn), lambda j, i: (i, j)),
        out_shape=jax.ShapeDtypeStruct((n, N_BIG), BF16),
        compiler_params=_cparams(("arbitrary", "arbitrary")), name="in_proj",
    )(hb, w_big)


def _gates_kernel(h_ref, wfi_ref, wfit_ref, bfc_ref, bic_ref, bfr_ref, bir_ref,
                  rows_ref, cols_ref, kb_ref, carry_ref, *, tg, chunk, nb):
    t = pl.program_id(1)

    @pl.when(t == 0)
    def _():
        carry_ref[...] = jnp.zeros_like(carry_ref)

    seq_rows = lambda a, i: a[i * tg:(i + 1) * tg]
    seq_lanes = lambda a, i, w: a[:, i * w:(i + 1) * w]
    hb = h_ref[...].reshape(nb * tg, D_MODEL)
    r_i = lax.broadcasted_iota(jnp.int32, (tg, tg), 0)
    c_i = lax.broadcasted_iota(jnp.int32, (tg, tg), 1)
    same_chunk = (r_i // chunk) == (c_i // chunk)
    ind = lambda mask: jnp.where(mask, 1.0, 0.0).astype(BF16)
    upper_bd = ind((r_i <= c_i) & same_chunk)
    lower = ind(r_i >= c_i)
    lower_bd = ind((r_i >= c_i) & same_chunk)

    g_rows = lax.dot_general(wfit_ref[...], hb, NT_DIMS, preferred_element_type=F32)
    gf = g_rows[0:16] + bfc_ref[...]
    gi = g_rows[16:32] + bic_ref[...]
    lf3 = jnp.concatenate(_split3(_log_sigmoid(gf)), axis=0)
    y3 = jnp.dot(jnp.concatenate([seq_lanes(lf3, i, tg) for i in range(nb)], axis=0), upper_bd,
                 preferred_element_type=F32)
    row = lax.broadcasted_iota(jnp.int32, (16, tg), 0)
    for i in range(nb):
        y_bd = y3[48 * i:48 * i + 16] + y3[48 * i + 16:48 * i + 32] + y3[48 * i + 32:48 * i + 48]
        rows_ref[i] = jnp.where(row < 4, y_bd, seq_lanes(gi, i, tg) - y_bd)[0:8]

    g_cols = jnp.dot(hb, wfi_ref[...], preferred_element_type=F32)
    gfc = g_cols[:, 0:128] + bfr_ref[...]
    gic = g_cols[:, 128:256] + bir_ref[...]
    lfc3 = jnp.concatenate(_split3(_log_sigmoid(gfc)), axis=1)
    lfc3 = jnp.concatenate([seq_rows(lfc3, i) for i in range(nb)], axis=1)
    y_bd_c = jnp.dot(lower_bd, lfc3, preferred_element_type=F32)
    y_full_c = jnp.dot(lower, lfc3, preferred_element_type=F32)
    sum3 = lambda a: a[:, 0:128] + a[:, 128:256] + a[:, 256:384]
    col = lax.broadcasted_iota(jnp.int32, (tg, 128), 1)
    parts = []
    for i in range(nb):
        yc = sum3(seq_lanes(y_bd_c, i, 384))
        cols_ref[i] = jnp.where(col < 4, yc, seq_rows(gic, i) - yc)
        cum = sum3(seq_lanes(y_full_c, i, 384)) + carry_ref[i, 0:1, :]
        carry_ref[i] = jnp.broadcast_to(cum[tg - 1:tg, :], carry_ref.shape[1:])
        parts.append(jnp.concatenate(_split3(cum * LOG2E), axis=1))
    sr = lax.broadcasted_iota(jnp.int32, (3 * 128, 128), 0)
    sc = lax.broadcasted_iota(jnp.int32, (3 * 128, 128), 1)
    scatter = ind((sr % 128 == 8 + sc // 3) & (sr // 128 == sc % 3) & (sc < 3 * H_B))
    kb = jnp.dot(jnp.concatenate(parts, axis=0), scatter, preferred_element_type=F32).astype(BF16)
    for i in range(nb):
        kb_ref[i] = seq_rows(kb, i)


def _gates(hb3, gw):
    bsz, t, _ = hb3.shape
    tg = min(512, t)
    nb = math.gcd(GATES_SEQS_PER_STEP, bsz)
    full = lambda shape: pl.BlockSpec(shape, lambda b, i: tuple(0 for _ in shape))
    return pl.pallas_call(
        functools.partial(_gates_kernel, tg=tg, chunk=min(MLSTM_CHUNK, t), nb=nb),
        grid=(bsz // nb, t // tg),
        in_specs=[pl.BlockSpec((nb, tg, D_MODEL), lambda b, i: (b, i, 0)),
                  full((D_MODEL, 256)), full((32, D_MODEL)),
                  full((16, 1)), full((16, 1)), full((1, 128)), full((1, 128))],
        out_specs=[pl.BlockSpec((nb, 8, tg), lambda b, i: (b, 0, i)),
                   pl.BlockSpec((nb, tg, 128), lambda b, i: (b, i, 0)),
                   pl.BlockSpec((nb, tg, 128), lambda b, i: (b, i, 0))],
        out_shape=[jax.ShapeDtypeStruct((bsz, 8, t), F32), jax.ShapeDtypeStruct((bsz, t, 128), F32),
                   jax.ShapeDtypeStruct((bsz, t, 128), BF16)],
        scratch_shapes=[pltpu.VMEM((nb, 8, 128), F32)],
        compiler_params=_cparams(("parallel", "arbitrary")), name="gates",
    )(hb3, *gw)


def _mlstm_kernel(c_ref, v_ref, o_ref, rows_ref, cols_ref, cw_ref, cb_ref, wq_ref, wk_ref, ng_ref,
                  y_ref, cbuf, c_state, m_state, *, chunk):
    L = chunk
    t = pl.program_id(1)

    @pl.when(t == 0)
    def _():
        cbuf[0:8, :] = jnp.zeros((8, D_MODEL), F32)
        c_state[...] = jnp.zeros_like(c_state)
        m_state[...] = jnp.zeros_like(m_state)

    cbuf[8:8 + L, :] = c_ref[0].astype(F32)

    nc = L // 128
    lanes = lambda c: slice(c * 128, (c + 1) * 128)
    r_i = lax.broadcasted_iota(jnp.int32, (L, 128), 0)
    c_i = lax.broadcasted_iota(jnp.int32, (L, 128), 1)
    causal = [c_i + c * 128 <= r_i for c in range(nc)]
    ones_blk = jnp.ones((L, 128), BF16)
    for h in range(H_A):
        hs = slice(h * DH_A, (h + 1) * DH_A)
        conv = cb_ref[:, hs]
        for j in range(CONV_K):
            conv = conv + cbuf[8 - (CONV_K - 1) + j:8 - (CONV_K - 1) + j + L, hs] * cw_ref[j:j + 1, hs]
        uh = _silu(conv).astype(BF16)
        q = jnp.dot(uh, wq_ref[h], preferred_element_type=F32)
        k = jnp.dot(uh, wk_ref[h], preferred_element_type=F32) * (DH_A ** -0.5)
        qb = q.astype(BF16)
        kb = k.astype(BF16)
        v_aug = jnp.concatenate([v_ref[0, :, hs], ones_blk], axis=1)
        b_rep = jnp.broadcast_to(cols_ref[0, :, h:h + 1], (L, 128))
        c_rep = jnp.broadcast_to(cols_ref[0, :, H_A + h:H_A + h + 1], (L, 128))
        b_row = rows_ref[0, h:h + 1, :]
        c_row = rows_ref[0, H_A + h:H_A + h + 1, :]
        m_prev = m_state[h, 0:1, :]

        dm = [jnp.where(causal[c], b_rep + c_row[:, lanes(c)], -jnp.inf) for c in range(nc)]
        dmax = dm[0]
        for c in range(1, nc):
            dmax = jnp.maximum(dmax, dm[c])
        inter = b_rep + m_prev
        m_t = jnp.maximum(jnp.max(dmax, axis=1, keepdims=True), inter)
        qk = lax.dot_general(qb, kb, NT_DIMS, preferred_element_type=F32)
        s = jnp.concatenate([(qk[:, lanes(c)] * jnp.exp(dm[c] - m_t)).astype(BF16) for c in range(nc)], axis=1)
        decay = jnp.exp(inter - m_t)
        c_old = c_state[h]
        inter_acc = jnp.dot(qb, c_old.astype(BF16), preferred_element_type=F32)
        acc = jnp.dot(s, v_aug, preferred_element_type=F32) + jnp.tile(decay, (1, DH_A // 128 + 1)) * inter_acc
        den = acc[:, DH_A:]
        inv = 1.0 / jnp.maximum(jnp.abs(den), jnp.exp(-m_t))
        hh = acc[:, :DH_A] * jnp.tile(inv, (1, DH_A // 128))

        g = b_row[:, L - 1:L]
        a_rep = g + c_rep
        m_new = jnp.maximum(g + m_prev, jnp.max(a_rep, axis=0, keepdims=True))
        kw = (k * jnp.tile(jnp.exp(a_rep - m_new), (1, DH_A // 128))).astype(BF16)
        sc = jnp.exp(g + m_prev - m_new)
        c_state[h] = (jnp.tile(sc, (1, DH_A // 128 + 1)) * c_old
                      + lax.dot_general(kw, v_aug, TN_DIMS, preferred_element_type=F32))
        m_state[h] = jnp.broadcast_to(m_new, m_state.shape[1:])

        mu = jnp.mean(hh, axis=1, keepdims=True)
        hc = hh - mu
        var = jnp.mean(hc * hc, axis=1, keepdims=True)
        hn = hc * lax.rsqrt(var + LN_EPS) * ng_ref[:, hs]
        og = _sigmoid(o_ref[0, :, hs])
        y_ref[0, :, hs] = (og.astype(F32) * hn).astype(BF16)
    cbuf[0:8, :] = cbuf[L:L + 8, :]


def _mlstm(p3, rows, cols, conv_w, conv_b, wq, wk, norm_g):
    bsz, t, _ = p3.shape
    L = min(MLSTM_CHUNK, t)
    colblk = lambda c: pl.BlockSpec((1, L, 1024), lambda b, i, c=c: (b, i, c))
    full = lambda shape: pl.BlockSpec(shape, lambda b, i: tuple(0 for _ in shape))
    return pl.pallas_call(
        functools.partial(_mlstm_kernel, chunk=L),
        grid=(bsz, t // L),
        in_specs=[colblk(COL_CA), colblk(COL_VA), colblk(COL_OA),
                  pl.BlockSpec((1, 8, L), lambda b, i: (b, 0, i)),
                  pl.BlockSpec((1, L, 128), lambda b, i: (b, i, 0)),
                  full((CONV_K, 1024)), full((1, 1024)), full((H_A, DH_A, DH_A)), full((H_A, DH_A, DH_A)),
                  full((1, 1024))],
        out_specs=pl.BlockSpec((1, L, 1024), lambda b, i: (b, i, 0)),
        out_shape=jax.ShapeDtypeStruct((bsz, t, 1024), BF16),
        scratch_shapes=[pltpu.VMEM((8 + L, 1024), F32), pltpu.VMEM((H_A, DH_A, DH_A + 128), F32),
                        pltpu.VMEM((H_A, 8, 128), F32)],
        compiler_params=_cparams(("parallel", "arbitrary")), name="mlstm",
    )(p3, p3, p3, rows, cols, conv_w, conv_b.reshape(1, -1), wq, wk, norm_g.reshape(1, -1))


def _flash_kernel(*refs, tq, tk, dk, heads, has_bias):
    if has_bias:
        q_ref, k_ref, kb_ref, v_ref, o_ref, m_sc, l_sc, acc_sc = refs
    else:
        q_ref, k_ref, v_ref, o_ref, m_sc, l_sc, acc_sc = refs
        kb_ref = None
    i = pl.program_id(2)
    qs = []
    for g in range(heads):
        q = q_ref[0, :, g * dk:(g + 1) * dk]
        if has_bias:
            lane = lax.broadcasted_iota(jnp.int32, (tq, 128), 1) - 3 * (pl.program_id(1) * heads + g)
            q = jnp.concatenate([q, jnp.where((lane >= 0) & (lane < 3), -1.0, 0.0).astype(BF16)], axis=1)
        qs.append(q)

    def scores(g, key0, nk, r0, nr, mask_off):
        ks = pl.ds(key0, nk)
        kj = k_ref[0, ks, g * dk:(g + 1) * dk]
        if has_bias:
            kj = jnp.concatenate([kj, kb_ref[0, ks, :]], axis=1)
        s = lax.dot_general(qs[g][r0:r0 + nr], kj, NT_DIMS, preferred_element_type=F32)
        if mask_off is not None:
            r_i = r0 + lax.broadcasted_iota(jnp.int32, (nr, nk), 0)
            c_i = mask_off + lax.broadcasted_iota(jnp.int32, (nr, nk), 1)
            s = jnp.where(c_i <= r_i, s, -jnp.inf)
        return s

    def update(g, s, key0, nk, r0, nr, first):
        rs = slice(r0, r0 + nr)
        vj = v_ref[0, pl.ds(key0, nk), g * 128:(g + 1) * 128]
        s_max = jnp.max(s, axis=1, keepdims=True)
        if first:
            m_new = jnp.broadcast_to(s_max, (nr, 128))
            l_part = jnp.zeros((nr, 128), F32)
        else:
            m_prev = m_sc[g, rs]
            m_new = jnp.maximum(m_prev, s_max)
            alpha = jnp.exp2(m_prev - m_new)
            l_part = alpha * l_sc[g, rs]
        ps = []
        for c in range(nk // 128):
            p_c = jnp.exp2(s[:, c * 128:(c + 1) * 128] - m_new)
            l_part = l_part + p_c
            ps.append(p_c.astype(BF16))
        pv = jnp.dot(jnp.concatenate(ps, axis=1), vj, preferred_element_type=F32)
        acc_sc[g, rs] = pv if first else alpha * acc_sc[g, rs] + pv
        l_sc[g, rs] = l_part
        m_sc[g, rs] = m_new

    def finish(g):
        l = jnp.sum(l_sc[g], axis=1, keepdims=True)
        o_ref[0, :, g * 128:(g + 1) * 128] = (acc_sc[g] / l).astype(o_ref.dtype)

    def sweep(key0, nk, r0, nr, mask_off, first=False, last=False):
        s_cur = scores(0, key0, nk, r0, nr, mask_off)
        for g in range(heads):
            s_next = scores(g + 1, key0, nk, r0, nr, mask_off) if g + 1 < heads else None
            update(g, s_cur, key0, nk, r0, nr, first)
            if last:
                finish(g)
            s_cur = s_next

    nsub = tq // tk

    def body(j, carry):
        sweep(pl.multiple_of(j * tk, tk), tk, 0, tq, None)
        return carry

    def diag(d, **kw):
        sweep(pl.multiple_of(i * tq + d * tk, tk), tk, d * tk, tq - d * tk, d * tk, **kw)

    diag(0, first=True)
    lax.fori_loop(0, i * nsub, body, 0)
    for d in range(1, nsub):
        diag(d, last=(d == nsub - 1))
    if nsub == 1:
        for g in range(heads):
            finish(g)


def _flash(q_arr, k_arr, kb_arr, v_arr, *, n_heads, dk, q_blk, k_blk, v_blk):
    bsz, t, _ = q_arr.shape
    tq = min(FLASH_TQ, t)
    tk = min(FLASH_TK, t)
    g = FLASH_HEADS_PER_STEP
    hp = n_heads // g
    in_specs = [pl.BlockSpec((1, tq, g * dk), lambda b, h, i: (b, i, q_blk // g + h)),
                pl.BlockSpec((1, t, g * dk), lambda b, h, i: (b, 0, k_blk // g + h))]
    args = [q_arr, k_arr]
    if kb_arr is not None:
        in_specs.append(pl.BlockSpec((1, t, 128), lambda b, h, i: (b, 0, 0)))
        args.append(kb_arr)
    in_specs.append(pl.BlockSpec((1, t, g * 128), lambda b, h, i: (b, 0, v_blk // g + h)))
    args.append(v_arr)
    return pl.pallas_call(
        functools.partial(_flash_kernel, tq=tq, tk=tk, dk=dk, heads=g, has_bias=kb_arr is not None),
        grid=(bsz, hp, t // tq),
        in_specs=in_specs,
        out_specs=pl.BlockSpec((1, tq, g * 128), lambda b, h, i: (b, i, h)),
        out_shape=jax.ShapeDtypeStruct((bsz, t, n_heads * 128), BF16),
        scratch_shapes=[pltpu.VMEM((g, tq, 128), F32), pltpu.VMEM((g, tq, 128), F32), pltpu.VMEM((g, tq, 128), F32)],
        compiler_params=_cparams(("parallel", "parallel", "arbitrary")),
        name="flash_fox" if kb_arr is not None else "flash_mla",
    )(*args)


def _mla_prep_kernel(c_ref, tab_ref, gq_ref, gkv_ref, wq_ref, wkv_ref, q_out, k_out, v_out):
    c = c_ref[...].astype(F32)
    ckv = c[:, 0:KV_RANK]
    cq = c[:, KV_RANK:KV_RANK + Q_RANK]
    kr = c[:, KV_RANK + Q_RANK:]

    def rms(x, g):
        ms = jnp.mean(x * x, axis=-1, keepdims=True)
        return (x * lax.rsqrt(ms + RMS_EPS) * g).astype(BF16)

    qf = jnp.dot(rms(cq, gq_ref[...]), wq_ref[...], preferred_element_type=F32)
    kvf = jnp.dot(rms(ckv, gkv_ref[...]), wkv_ref[...], preferred_element_type=F32)
    tab = tab_ref[...]

    def rope(xx):
        pr = xx * tab
        return pr + pltpu.roll(pr, DH_ROPE, 1)

    lane = lax.broadcasted_iota(jnp.int32, kr.shape, 1)
    k_rope = jnp.where(lane < DH_ROPE, rope(kr), 0.0).astype(BF16)
    for h in range(H_C):
        lo = h * DK_C
        q_out[:, lo:lo + 128] = qf[:, lo:lo + 128].astype(BF16)
        q_out[:, lo + 128:lo + 256] = rope(qf[:, lo + 128:lo + 256]).astype(BF16)
        k_out[:, lo:lo + 128] = kvf[:, lo:lo + 128].astype(BF16)
        k_out[:, lo + 128:lo + 256] = k_rope
        v_out[:, h * DH_V:(h + 1) * DH_V] = kvf[:, lo + 128:lo + 256].astype(BF16)


def _mla_prep(p2, tab, g_cq, g_ckv, wq, wkv):
    n = p2.shape[0]
    tm = min(512, n)
    full = lambda shape: pl.BlockSpec(shape, lambda i: tuple(0 for _ in shape))
    return pl.pallas_call(
        _mla_prep_kernel, grid=(n // tm,),
        in_specs=[pl.BlockSpec((tm, LOWRANK_W), lambda i: (i, LOWRANK_OFF // LOWRANK_W)),
                  pl.BlockSpec((tm, 128), lambda i: (i, 0)),
                  full((1, Q_RANK)), full((1, KV_RANK)), full((Q_RANK, H_C * DK_C)), full((KV_RANK, H_C * DK_C))],
        out_specs=[pl.BlockSpec((tm, H_C * DK_C), lambda i: (i, 0)),
                   pl.BlockSpec((tm, H_C * DK_C), lambda i: (i, 0)),
                   pl.BlockSpec((tm, H_C * DH_V), lambda i: (i, 0))],
        out_shape=[jax.ShapeDtypeStruct((n, H_C * DK_C), BF16), jax.ShapeDtypeStruct((n, H_C * DK_C), BF16),
                   jax.ShapeDtypeStruct((n, H_C * DH_V), BF16)],
        compiler_params=_cparams(("parallel",)), name="mla_prep",
    )(p2, tab, g_cq.reshape(1, -1), g_ckv.reshape(1, -1), wq, wkv)


def _merge_kernel(ya_ref, yb_ref, yc_ref, zg_ref, h_ref,
                  pa_ref, pb_ref, pc_ref, wo_ref, lg_ref, lb_ref, hf_ref, hb_ref, *, alpha):
    def front(rs):
        def branch(y_ref, col_z, col_g, p_ref):
            hz = zg_ref[rs, col_z * 1024:(col_z + 1) * 1024] * 0.5
            y = y_ref[rs, :] * (hz + hz * jnp.tanh(hz))
            gate = 0.5 + 0.5 * jnp.tanh(zg_ref[rs, col_g * 1024:(col_g + 1) * 1024] * 0.5)
            return gate.astype(F32) * jnp.dot(y, p_ref[...], preferred_element_type=F32)

        return (branch(ya_ref, COL_ZA, COL_GA, pa_ref) + branch(yb_ref, COL_ZB, COL_GB, pb_ref)
                + branch(yc_ref, COL_ZC, COL_GC, pc_ref)).astype(BF16)

    def back(rs, merged):
        out = jnp.dot(merged, wo_ref[...], preferred_element_type=F32)
        r = alpha * h_ref[rs, :] + out
        mu = jnp.mean(r, axis=-1, keepdims=True)
        rc = r - mu
        var = jnp.mean(rc * rc, axis=-1, keepdims=True)
        y = rc * lax.rsqrt(var + LN_EPS) * lg_ref[...] + lb_ref[...]
        hf_ref[rs, :] = y
        hb_ref[rs, :] = y.astype(BF16)

    tm = h_ref.shape[0]
    strips = [slice(r, r + tm // MERGE_STRIPS) for r in range(0, tm, tm // MERGE_STRIPS)]
    cur = front(strips[0])
    for n, rs in enumerate(strips):
        nxt = front(strips[n + 1]) if n + 1 < len(strips) else None
        back(rs, cur)
        cur = nxt


def _merge(ya, yb, yc, p2, hf, pa, pb, pc, wo, ln_g, ln_b, alpha):
    n = hf.shape[0]
    tm = min(512, n)
    row = pl.BlockSpec((tm, D_MODEL), lambda i: (i, 0))
    wfull =pl.BlockSpec((1024, D_MODEL), lambda i: (0, 0), pipeline_mode=pl.Buffered(1))
    vec = pl.BlockSpec((1, D_MODEL), lambda i: (0, 0))
    return pl.pallas_call(
        functools.partial(_merge_kernel, alpha=alpha), grid=(n // tm,),
        in_specs=[row, row, row, pl.BlockSpec((tm, N_GATE_COLS * 1024), lambda i: (i, 0)), row,
                  wfull, wfull, wfull, wfull, vec, vec],
        out_specs=[row, row],
        out_shape=[jax.ShapeDtypeStruct((n, D_MODEL), F32), jax.ShapeDtypeStruct((n, D_MODEL), BF16)],
        compiler_params=_cparams(("parallel",)), name="merge",
    )(ya, yb, yc, p2, hf, pa, pb, pc, wo, ln_g.reshape(1, -1), ln_b.reshape(1, -1))


def _rot_cols(w):
    half = w.shape[-1] // 2
    return jnp.concatenate([-w[..., half:], w[..., :half]], axis=-1)


def _layer_weights(w_in, b_ia, b_fa, b_fb, w_uq, w_ukv):
    depth = w_in.shape[0]
    idx = []
    acc = 0
    for s in IN_SPLITS[:-1]:
        acc += s
        idx.append(acc)
    (w_ca, w_va, w_oa, w_ia, w_fa, w_za, w_qb, w_kb, w_vb, w_fb, w_zb,
     w_cq, w_ckv, w_kr, w_zc, w_ga, w_gb, w_gc) = jnp.split(w_in, idx, axis=-1)
    w_qb = w_qb * (DH_B ** -0.5 * LOG2E)
    w_big = jnp.concatenate([w.astype(BF16) for w in (w_za, w_zb, w_zc, w_ga, w_gb, w_gc, w_ca, w_va, w_oa, w_qb,
                                                      w_kb, w_vb, w_ckv, w_cq, w_kr, _rot_cols(w_kr))], axis=-1)
    zpad = lambda k: jnp.zeros((depth, D_MODEL, k), F32)
    w_f = jnp.concatenate([w_fa, w_fa, w_fb, zpad(112)], axis=-1)
    w_i = jnp.concatenate([zpad(4), w_ia, zpad(120)], axis=-1)
    b_f = jnp.concatenate([b_fa, b_fa, b_fb, jnp.zeros((depth, 112), F32)], axis=-1)
    b_i = jnp.concatenate([jnp.zeros((depth, 4), F32), b_ia, jnp.zeros((depth, 120), F32)], axis=-1)
    gw = (jnp.concatenate([w_f, w_i], axis=-1).astype(BF16),
          jnp.swapaxes(jnp.concatenate([w_f[..., :16], w_i[..., :16]], axis=-1), -1, -2).astype(BF16),
          b_f[:, :16, None], b_i[:, :16, None], b_f[:, None, :], b_i[:, None, :])
    wq3 = w_uq.reshape(depth, Q_RANK, H_C, DH_NOPE + DH_ROPE) * ((DH_NOPE + DH_ROPE) ** -0.5 * LOG2E)
    wq_rope = wq3[..., DH_NOPE:]
    wq = jnp.concatenate([wq3, _rot_cols(wq_rope)], axis=-1).reshape(depth, Q_RANK, H_C * DK_C).astype(BF16)
    return w_big, gw, wq, w_ukv.astype(BF16)


def kernel(x, positions, ln_in_g, ln_in_b, w_in, conv_w, conv_b, w_qa, w_ka, b_ia, b_fa, norm_ga, b_fb,
           g_cq, w_uq, g_ckv, w_ukv, p_a, p_b, p_c, w_out, ln_g, ln_b):
    bsz, t, _ = x.shape
    n = bsz * t
    depth = w_in.shape[0]
    alpha = (2.0 * depth) ** 0.25

    w_big, gw, wq, wkv = _layer_weights(w_in, b_ia, b_fa, b_fb, w_uq, w_ukv)
    wqa, wka, pa, pb, pc, wo = (w.astype(BF16) for w in (w_qa, w_ka, p_a, p_b, p_c, w_out))
    tab = _rope_table(positions)
    hf, hb = _input_ln(x.reshape(n, D_MODEL), ln_in_g, ln_in_b)
    for l in range(depth):
        p2 = _in_proj(hb, w_big[l])
        p3 = p2.reshape(bsz, t, N_BIG)
        rows, cols, kb = _gates(hb.reshape(bsz, t, D_MODEL), tuple(w[l] for w in gw))
        ya = _mlstm(p3, rows, cols, conv_w[l], conv_b[l], wqa[l], wka[l], norm_ga[l])
        yb = _flash(p3, p3, kb, p3, n_heads=H_B, dk=DH_B, q_blk=COL_QB * 8, k_blk=COL_KB * 8, v_blk=COL_VB * 8)
        qc, kc, vc = _mla_prep(p2, tab, g_cq[l], g_ckv[l], wq[l], wkv[l])
        yc = _flash(qc.reshape(bsz, t, -1), kc.reshape(bsz, t, -1), None, vc.reshape(bsz, t, -1),
                    n_heads=H_C, dk=DK_C, q_blk=0, k_blk=0, v_blk=0)
        hf, hb = _merge(ya.reshape(n, -1), yb.reshape(n, -1), yc.reshape(n, -1), p2, hf,
                        pa[l], pb[l], pc[l], wo[l], ln_g[l], ln_b[l], alpha)
    return hf.reshape(bsz, t, D_MODEL)
```

```python
import functools
import math

import jax
import jax.numpy as jnp
from jax import lax
from jax.experimental import pallas as pl
from jax.experimental.pallas import tpu as pltpu

F32 = jnp.float32
BF16 = jnp.bfloat16

D_MODEL = 1024
H_A, DH_A, CONV_K = 4, 256, 4
H_B, DH_B = 8, 128
H_C, DH_NOPE, DH_ROPE, DH_V = 8, 128, 64, 128
Q_RANK, KV_RANK = 384, 256
ROPE_BASE = 10000.0
LN_EPS = 1e-5
RMS_EPS = 1e-6
IN_SPLITS = (1024, 1024, 1024, 4, 4, 1024,
             1024, 1024, 1024, 8, 1024,
             Q_RANK, KV_RANK, DH_ROPE, 1024,
             1024, 1024, 1024)

COL_ZA, COL_ZB, COL_ZC, COL_GA, COL_GB, COL_GC, COL_CA, COL_VA, COL_OA, COL_QB, COL_KB, COL_VB = range(12)
N_GATE_COLS = 6
LOWRANK_OFF = 12 * 1024
LOWRANK_W = 768
N_BIG = LOWRANK_OFF + LOWRANK_W
DK_C = 256

MLSTM_CHUNK = 256
MLSTM_SEQS_PER_STEP = 2
GATES_SEQS_PER_STEP = 4
FLASH_TQ = 1024
FLASH_TK = 512
FLASH_HEADS_PER_STEP = 4
MERGE_STRIPS = 2
VMEM_LIMIT = 56 * 1024 * 1024
LOG2E = math.log2(math.e)

NT_DIMS = (((1,), (1,)), ((), ()))
TN_DIMS = (((0,), (0,)), ((), ()))


def _cparams(sem):
    return pltpu.CompilerParams(dimension_semantics=sem, vmem_limit_bytes=VMEM_LIMIT)


def _sigmoid(x):
    return 0.5 + 0.5 * jnp.tanh(0.5 * x)


def _silu(x):
    hx = 0.5 * x
    return hx + hx * jnp.tanh(hx)


def _log_sigmoid(x):
    return jnp.minimum(x, 0.0) - jnp.log1p(jnp.exp(-jnp.abs(x)))


def _split3(x):
    hi = x.astype(BF16)
    r = x - hi.astype(F32)
    mid = r.astype(BF16)
    lo = (r - mid.astype(F32)).astype(BF16)
    return hi, mid, lo


def _ln_kernel(x_ref, g_ref, b_ref, hf_ref, hb_ref):
    x = x_ref[...]
    mu = jnp.mean(x, axis=-1, keepdims=True)
    xc = x - mu
    var = jnp.mean(xc * xc, axis=-1, keepdims=True)
    y = xc * lax.rsqrt(var + LN_EPS) * g_ref[...] + b_ref[...]
    hf_ref[...] = y
    hb_ref[...] = y.astype(BF16)


def _input_ln(x2, g, b):
    n = x2.shape[0]
    tm = min(512, n)
    row = pl.BlockSpec((tm, D_MODEL), lambda i: (i, 0))
    vec = pl.BlockSpec((1, D_MODEL), lambda i: (0, 0))
    return pl.pallas_call(
        _ln_kernel, grid=(n // tm,), in_specs=[row, vec, vec], out_specs=[row, row],
        out_shape=[jax.ShapeDtypeStruct((n, D_MODEL), F32), jax.ShapeDtypeStruct((n, D_MODEL), BF16)],
        compiler_params=_cparams(("parallel",)), name="input_ln",
    )(x2, g.reshape(1, -1), b.reshape(1, -1))


def _rope_kernel(pos_ref, freq_ref, tab_ref):
    ang = pos_ref[...].astype(F32) * freq_ref[...]
    lane = lax.broadcasted_iota(jnp.int32, ang.shape, 1)
    tab_ref[...] = jnp.where(lane < DH_ROPE, jnp.cos(ang), jnp.sin(ang))


def _rope_table(positions):
    n = positions.size
    tm = min(1024, n)
    inv_freq = ROPE_BASE ** (-jnp.arange(0, DH_ROPE, 2, dtype=F32) / DH_ROPE)
    freq = jnp.tile(inv_freq, 4).reshape(1, 128)
    return pl.pallas_call(
        _rope_kernel, grid=(n // tm,),
        in_specs=[pl.BlockSpec((tm, 1), lambda i: (i, 0)), pl.BlockSpec((1, 128), lambda i: (0, 0))],
        out_specs=pl.BlockSpec((tm, 128), lambda i: (i, 0)),
        out_shape=jax.ShapeDtypeStruct((n, 128), F32),
        compiler_params=_cparams(("parallel",)), name="rope_table",
    )(positions.reshape(n, 1), freq)


def _matmul_kernel(x_ref, w_ref, o_ref):
    o_ref[...] = jnp.dot(x_ref[...], w_ref[...], preferred_element_type=F32).astype(o_ref.dtype)


def _in_proj(hb, w_big):
    n = hb.shape[0]
    tm = min(1024, n)
    tn = N_BIG // 3
    return pl.pallas_call(
        _matmul_kernel, grid=(n // tm, N_BIG // tn),
        in_specs=[pl.BlockSpec((tm, D_MODEL), lambda i, j: (i, 0)),
                  pl.BlockSpec((D_MODEL, tn), lambda i, j: (0, j))],
        out_specs=pl.BlockSpec((tm, tn), lambda i, j: (i, j)),
        out_shape=jax.ShapeDtypeStruct((n, N_BIG), BF16),
        compiler_params=_cparams(("parallel", "arbitrary")), name="in_proj",
    )(hb, w_big)


def _gates_kernel(h_ref, wfi_ref, wfit_ref, bfc_ref, bic_ref, bfr_ref, bir_ref,
                  rows_ref, cols_ref, kb_ref, carry_ref, *, tg, chunk, nb):
    t = pl.program_id(1)

    @pl.when(t == 0)
    def _():
        carry_ref[...] = jnp.zeros_like(carry_ref)

    seq_rows = lambda a, i: a[i * tg:(i + 1) * tg]
    seq_lanes = lambda a, i, w: a[:, i * w:(i + 1) * w]
    hb = h_ref[...].reshape(nb * tg, D_MODEL)
    r_i = lax.broadcasted_iota(jnp.int32, (tg, tg), 0)
    c_i = lax.broadcasted_iota(jnp.int32, (tg, tg), 1)
    same_chunk = (r_i // chunk) == (c_i // chunk)
    ind = lambda mask: jnp.where(mask, 1.0, 0.0).astype(BF16)
    upper_bd = ind((r_i <= c_i) & same_chunk)
    lower = ind(r_i >= c_i)
    lower_bd = ind((r_i >= c_i) & same_chunk)

    g_rows = lax.dot_general(wfit_ref[...], hb, NT_DIMS, preferred_element_type=F32)
    gf = g_rows[0:16] + bfc_ref[...]
    gi = g_rows[16:32] + bic_ref[...]
    lf3 = jnp.concatenate(_split3(_log_sigmoid(gf)), axis=0)
    y3 = jnp.dot(jnp.concatenate([seq_lanes(lf3, i, tg) for i in range(nb)], axis=0), upper_bd,
                 preferred_element_type=F32)
    row = lax.broadcasted_iota(jnp.int32, (16, tg), 0)
    for i in range(nb):
        y_bd = y3[48 * i:48 * i + 16] + y3[48 * i + 16:48 * i + 32] + y3[48 * i + 32:48 * i + 48]
        rows_ref[i] = jnp.where(row < 4, y_bd, seq_lanes(gi, i, tg) - y_bd)[0:8]

    g_cols = jnp.dot(hb, wfi_ref[...], preferred_element_type=F32)
    gfc = g_cols[:, 0:128] + bfr_ref[...]
    gic = g_cols[:, 128:256] + bir_ref[...]
    lfc3 = jnp.concatenate(_split3(_log_sigmoid(gfc)), axis=1)
    lfc3 = jnp.concatenate([seq_rows(lfc3, i) for i in range(nb)], axis=1)
    y_bd_c = jnp.dot(lower_bd, lfc3, preferred_element_type=F32)
    y_full_c = jnp.dot(lower, lfc3, preferred_element_type=F32)
    sum3 = lambda a: a[:, 0:128] + a[:, 128:256] + a[:, 256:384]
    col = lax.broadcasted_iota(jnp.int32, (tg, 128), 1)
    parts = []
    for i in range(nb):
        yc = sum3(seq_lanes(y_bd_c, i, 384))
        cols_ref[i] = jnp.where(col < 4, yc, seq_rows(gic, i) - yc)
        cum = sum3(seq_lanes(y_full_c, i, 384)) + carry_ref[i, 0:1, :]
        carry_ref[i] = jnp.broadcast_to(cum[tg - 1:tg, :], carry_ref.shape[1:])
        parts.append(jnp.concatenate(_split3(cum * LOG2E), axis=1))
    sr = lax.broadcasted_iota(jnp.int32, (3 * 128, 128), 0)
    sc = lax.broadcasted_iota(jnp.int32, (3 * 128, 128), 1)
    scatter = ind((sr % 128 == 8 + sc // 3) & (sr // 128 == sc % 3) & (sc < 3 * H_B))
    kb = jnp.dot(jnp.concatenate(parts, axis=0), scatter, preferred_element_type=F32).astype(BF16)
    for i in range(nb):
        kb_ref[i] = seq_rows(kb, i)


def _gates(hb3, gw):
    bsz, t, _ = hb3.shape
    tg = min(512, t)
    nb = math.gcd(GATES_SEQS_PER_STEP, bsz)
    full = lambda shape: pl.BlockSpec(shape, lambda b, i: tuple(0 for _ in shape))
    return pl.pallas_call(
        functools.partial(_gates_kernel, tg=tg, chunk=min(MLSTM_CHUNK, t), nb=nb),
        grid=(bsz // nb, t // tg),
        in_specs=[pl.BlockSpec((nb, tg, D_MODEL), lambda b, i: (b, i, 0)),
                  full((D_MODEL, 256)), full((32, D_MODEL)),
                  full((16, 1)), full((16, 1)), full((1, 128)), full((1, 128))],
        out_specs=[pl.BlockSpec((nb, 8, tg), lambda b, i: (b, 0, i)),
                   pl.BlockSpec((nb, tg, 128), lambda b, i: (b, i, 0)),
                   pl.BlockSpec((nb, tg, 128), lambda b, i: (b, i, 0))],
        out_shape=[jax.ShapeDtypeStruct((bsz, 8, t), F32), jax.ShapeDtypeStruct((bsz, t, 128), F32),
                   jax.ShapeDtypeStruct((bsz, t, 128), BF16)],
        scratch_shapes=[pltpu.VMEM((nb, 8, 128), F32)],
        compiler_params=_cparams(("parallel", "arbitrary")), name="gates",
    )(hb3, *gw)


def _mlstm_kernel(c_ref, v_ref, o_ref, rows_ref, cols_ref, cw_ref, cb_ref, wq_ref, wk_ref, ng_ref,
                  y_ref, cbuf, c_state, m_state, *, chunk, nb):
    for sq in range(nb):
        one = pl.ds(sq, 1)
        _mlstm_sequence(c_ref.at[one], v_ref.at[one], o_ref.at[one], rows_ref.at[one], cols_ref.at[one],
                        cw_ref, cb_ref, wq_ref, wk_ref, ng_ref, y_ref.at[one],
                        cbuf.at[sq], c_state.at[sq], m_state.at[sq], chunk=chunk)


def _mlstm_sequence(c_ref, v_ref, o_ref, rows_ref, cols_ref, cw_ref, cb_ref, wq_ref, wk_ref, ng_ref,
                    y_ref, cbuf, c_state, m_state, *, chunk):
    L = chunk
    t = pl.program_id(1)

    @pl.when(t == 0)
    def _():
        cbuf[0:8, :] = jnp.zeros((8, D_MODEL), F32)
        c_state[...] = jnp.zeros_like(c_state)
        m_state[...] = jnp.zeros_like(m_state)

    cbuf[8:8 + L, :] = c_ref[0].astype(F32)

    nc = L // 128
    lanes = lambda c: slice(c * 128, (c + 1) * 128)
    r_i = lax.broadcasted_iota(jnp.int32, (L, 128), 0)
    c_i = lax.broadcasted_iota(jnp.int32, (L, 128), 1)
    causal = [c_i + c * 128 <= r_i for c in range(nc)]
    ones_blk = jnp.ones((L, 128), BF16)
    for h in range(H_A):
        hs = slice(h * DH_A, (h + 1) * DH_A)
        conv = cb_ref[:, hs]
        for j in range(CONV_K):
            conv = conv + cbuf[8 - (CONV_K - 1) + j:8 - (CONV_K - 1) + j + L, hs] * cw_ref[j:j + 1, hs]
        uh = _silu(conv).astype(BF16)
        q = jnp.dot(uh, wq_ref[h], preferred_element_type=F32)
        k = jnp.dot(uh, wk_ref[h], preferred_element_type=F32) * (DH_A ** -0.5)
        qb = q.astype(BF16)
        kb = k.astype(BF16)
        v_aug = jnp.concatenate([v_ref[0, :, hs], ones_blk], axis=1)
        b_rep = jnp.broadcast_to(cols_ref[0, :, h:h + 1], (L, 128))
        c_rep = jnp.broadcast_to(cols_ref[0, :, H_A + h:H_A + h + 1], (L, 128))
        b_row = rows_ref[0, h:h + 1, :]
        c_row = rows_ref[0, H_A + h:H_A + h + 1, :]
        m_prev = m_state[h, 0:1, :]

        dm = [jnp.where(causal[c], b_rep + c_row[:, lanes(c)], -jnp.inf) for c in range(nc)]
        dmax = dm[0]
        for c in range(1, nc):
            dmax = jnp.maximum(dmax, dm[c])
        inter = b_rep + m_prev
        m_t = jnp.maximum(jnp.max(dmax, axis=1, keepdims=True), inter)
        qk = lax.dot_general(qb, kb, NT_DIMS, preferred_element_type=F32)
        s = jnp.concatenate([(qk[:, lanes(c)] * jnp.exp(dm[c] - m_t)).astype(BF16) for c in range(nc)], axis=1)
        decay = jnp.exp(inter - m_t)
        c_old = c_state[h]
        inter_acc = jnp.dot(qb, c_old.astype(BF16), preferred_element_type=F32)
        acc = jnp.dot(s, v_aug, preferred_element_type=F32) + jnp.tile(decay, (1, DH_A // 128 + 1)) * inter_acc
        den = acc[:, DH_A:]
        inv = 1.0 / jnp.maximum(jnp.abs(den), jnp.exp(-m_t))
        hh = acc[:, :DH_A] * jnp.tile(inv, (1, DH_A // 128))

        g = b_row[:, L - 1:L]
        a_rep = g + c_rep
        m_new = jnp.maximum(g + m_prev, jnp.max(a_rep, axis=0, keepdims=True))
        kw = (k * jnp.tile(jnp.exp(a_rep - m_new), (1, DH_A // 128))).astype(BF16)
        sc = jnp.exp(g + m_prev - m_new)
        c_state[h] = (jnp.tile(sc, (1, DH_A // 128 + 1)) * c_old
                      + lax.dot_general(kw, v_aug, TN_DIMS, preferred_element_type=F32))
        m_state[h] = jnp.broadcast_to(m_new, m_state.shape[1:])

        mu = jnp.mean(hh, axis=1, keepdims=True)
        hc = hh - mu
        var = jnp.mean(hc * hc, axis=1, keepdims=True)
        hn = hc * lax.rsqrt(var + LN_EPS) * ng_ref[:, hs]
        og = _sigmoid(o_ref[0, :, hs])
        y_ref[0, :, hs] = (og.astype(F32) * hn).astype(BF16)
    cbuf[0:8, :] = cbuf[L:L + 8, :]


def _mlstm(p3, rows, cols, conv_w, conv_b, wq, wk, norm_g):
    bsz, t, _ = p3.shape
    L = min(MLSTM_CHUNK, t)
    nb = math.gcd(MLSTM_SEQS_PER_STEP, bsz)
    colblk = lambda c: pl.BlockSpec((nb, L, 1024), lambda b, i, c=c: (b, i, c))
    full = lambda shape: pl.BlockSpec(shape, lambda b, i: tuple(0 for _ in shape))
    return pl.pallas_call(
        functools.partial(_mlstm_kernel, chunk=L, nb=nb),
        grid=(bsz // nb, t // L),
        in_specs=[colblk(COL_CA), colblk(COL_VA), colblk(COL_OA),
                  pl.BlockSpec((nb, 8, L), lambda b, i: (b, 0, i)),
                  pl.BlockSpec((nb, L, 128), lambda b, i: (b, i, 0)),
                  full((CONV_K, 1024)), full((1, 1024)), full((H_A, DH_A, DH_A)), full((H_A, DH_A, DH_A)),
                  full((1, 1024))],
        out_specs=pl.BlockSpec((nb, L, 1024), lambda b, i: (b, i, 0)),
        out_shape=jax.ShapeDtypeStruct((bsz, t, 1024), BF16),
        scratch_shapes=[pltpu.VMEM((nb, 8 + L, 1024), F32), pltpu.VMEM((nb, H_A, DH_A, DH_A + 128), F32),
                        pltpu.VMEM((nb, H_A, 8, 128), F32)],
        compiler_params=_cparams(("parallel", "arbitrary")), name="mlstm",
    )(p3, p3, p3, rows, cols, conv_w, conv_b.reshape(1, -1), wq, wk, norm_g.reshape(1, -1))


def _flash_kernel(*refs, tq, tk, dk, heads, has_bias):
    if has_bias:
        q_ref, k_ref, kb_ref, v_ref, o_ref, m_sc, l_sc, acc_sc = refs
    else:
        q_ref, k_ref, v_ref, o_ref, m_sc, l_sc, acc_sc = refs
        kb_ref = None
    i = pl.program_id(2)
    qs = []
    for g in range(heads):
        q = q_ref[0, :, g * dk:(g + 1) * dk]
        if has_bias:
            lane = lax.broadcasted_iota(jnp.int32, (tq, 128), 1) - 3 * (pl.program_id(1) * heads + g)
            q = jnp.concatenate([q, jnp.where((lane >= 0) & (lane < 3), -1.0, 0.0).astype(BF16)], axis=1)
        qs.append(q)

    def scores(g, key0, nk, r0, nr, mask_off):
        ks = pl.ds(key0, nk)
        kj = k_ref[0, ks, g * dk:(g + 1) * dk]
        if has_bias:
            kj = jnp.concatenate([kj, kb_ref[0, ks, :]], axis=1)
        s = lax.dot_general(qs[g][r0:r0 + nr], kj, NT_DIMS, preferred_element_type=F32)
        if mask_off is not None:
            r_i = r0 + lax.broadcasted_iota(jnp.int32, (nr, nk), 0)
            c_i = mask_off + lax.broadcasted_iota(jnp.int32, (nr, nk), 1)
            s = jnp.where(c_i <= r_i, s, -jnp.inf)
        return s

    def update(g, s, key0, nk, r0, nr, first):
        rs = slice(r0, r0 + nr)
        vj = v_ref[0, pl.ds(key0, nk), g * 128:(g + 1) * 128]
        s_max = jnp.max(s, axis=1, keepdims=True)
        if first:
            m_new = jnp.broadcast_to(s_max, (nr, 128))
            l_part = jnp.zeros((nr, 128), F32)
        else:
            m_prev = m_sc[g, rs]
            m_new = jnp.maximum(m_prev, s_max)
            alpha = jnp.exp2(m_prev - m_new)
            l_part = alpha * l_sc[g, rs]
        ps = []
        for c in range(nk // 128):
            p_c = jnp.exp2(s[:, c * 128:(c + 1) * 128] - m_new)
            l_part = l_part + p_c
            ps.append(p_c.astype(BF16))
        pv = jnp.dot(jnp.concatenate(ps, axis=1), vj, preferred_element_type=F32)
        acc_sc[g, rs] = pv if first else alpha * acc_sc[g, rs] + pv
        l_sc[g, rs] = l_part
        m_sc[g, rs] = m_new

    def finish(g):
        l = jnp.sum(l_sc[g], axis=1, keepdims=True)
        o_ref[0, :, g * 128:(g + 1) * 128] = (acc_sc[g] / l).astype(o_ref.dtype)

    def sweep(key0, nk, r0, nr, mask_off, first=False, last=False):
        s_cur = scores(0, key0, nk, r0, nr, mask_off)
        for g in range(heads):
            s_next = scores(g + 1, key0, nk, r0, nr, mask_off) if g + 1 < heads else None
            update(g, s_cur, key0, nk, r0, nr, first)
            if last:
                finish(g)
            s_cur = s_next

    nsub = tq // tk

    def body(j, carry):
        sweep(pl.multiple_of(j * tk, tk), tk, 0, tq, None)
        return carry

    def diag(d, **kw):
        sweep(pl.multiple_of(i * tq + d * tk, tk), tk, d * tk, tq - d * tk, d * tk, **kw)

    diag(0, first=True)
    lax.fori_loop(0, i * nsub, body, 0)
    for d in range(1, nsub):
        diag(d, last=(d == nsub - 1))
    if nsub == 1:
        for g in range(heads):
            finish(g)


def _flash(q_arr, k_arr, kb_arr, v_arr, *, n_heads, dk, q_blk, k_blk, v_blk):
    bsz, t, _ = q_arr.shape
    tq = min(FLASH_TQ, t)
    tk = min(FLASH_TK, t)
    g = FLASH_HEADS_PER_STEP
    hp = n_heads // g
    in_specs = [pl.BlockSpec((1, tq, g * dk), lambda b, h, i: (b, i, q_blk // g + h)),
                pl.BlockSpec((1, t, g * dk), lambda b, h, i: (b, 0, k_blk // g + h))]
    args = [q_arr, k_arr]
    if kb_arr is not None:
        in_specs.append(pl.BlockSpec((1, t, 128), lambda b, h, i: (b, 0, 0)))
        args.append(kb_arr)
    in_specs.append(pl.BlockSpec((1, t, g * 128), lambda b, h, i: (b, 0, v_blk // g + h)))
    args.append(v_arr)
    return pl.pallas_call(
        functools.partial(_flash_kernel, tq=tq, tk=tk, dk=dk, heads=g, has_bias=kb_arr is not None),
        grid=(bsz, hp, t // tq),
        in_specs=in_specs,
        out_specs=pl.BlockSpec((1, tq, g * 128), lambda b, h, i: (b, i, h)),
        out_shape=jax.ShapeDtypeStruct((bsz, t, n_heads * 128), BF16),
        scratch_shapes=[pltpu.VMEM((g, tq, 128), F32), pltpu.VMEM((g, tq, 128), F32), pltpu.VMEM((g, tq, 128), F32)],
        compiler_params=_cparams(("parallel", "parallel", "arbitrary")),
        name="flash_fox" if kb_arr is not None else "flash_mla",
    )(*args)


def _mla_prep_kernel(c_ref, tab_ref, gq_ref, gkv_ref, wq_ref, wkv_ref, q_out, k_out, v_out):
    c = c_ref[...].astype(F32)
    ckv = c[:, 0:KV_RANK]
    cq = c[:, KV_RANK:KV_RANK + Q_RANK]
    kr = c[:, KV_RANK + Q_RANK:]

    def rms(x, g):
        ms = jnp.mean(x * x, axis=-1, keepdims=True)
        return (x * lax.rsqrt(ms + RMS_EPS) * g).astype(BF16)

    qf = jnp.dot(rms(cq, gq_ref[...]), wq_ref[...], preferred_element_type=F32)
    kvf = jnp.dot(rms(ckv, gkv_ref[...]), wkv_ref[...], preferred_element_type=F32)
    tab = tab_ref[...]

    def rope(xx):
        pr = xx * tab
        return pr + pltpu.roll(pr, DH_ROPE, 1)

    lane = lax.broadcasted_iota(jnp.int32, kr.shape, 1)
    k_rope = jnp.where(lane < DH_ROPE, rope(kr), 0.0).astype(BF16)
    for h in range(H_C):
        lo = h * DK_C
        q_out[:, lo:lo + 128] = qf[:, lo:lo + 128].astype(BF16)
        q_out[:, lo + 128:lo + 256] = rope(qf[:, lo + 128:lo + 256]).astype(BF16)
        k_out[:, lo:lo + 128] = kvf[:, lo:lo + 128].astype(BF16)
        k_out[:, lo + 128:lo + 256] = k_rope
        v_out[:, h * DH_V:(h + 1) * DH_V] = kvf[:, lo + 128:lo + 256].astype(BF16)


def _mla_prep(p2, tab, g_cq, g_ckv, wq, wkv):
    n = p2.shape[0]
    tm = min(512, n)
    full = lambda shape: pl.BlockSpec(shape, lambda i: tuple(0 for _ in shape))
    return pl.pallas_call(
        _mla_prep_kernel, grid=(n // tm,),
        in_specs=[pl.BlockSpec((tm, LOWRANK_W), lambda i: (i, LOWRANK_OFF // LOWRANK_W)),
                  pl.BlockSpec((tm, 128), lambda i: (i, 0)),
                  full((1, Q_RANK)), full((1, KV_RANK)), full((Q_RANK, H_C * DK_C)), full((KV_RANK, H_C * DK_C))],
        out_specs=[pl.BlockSpec((tm, H_C * DK_C), lambda i: (i, 0)),
                   pl.BlockSpec((tm, H_C * DK_C), lambda i: (i, 0)),
                   pl.BlockSpec((tm, H_C * DH_V), lambda i: (i, 0))],
        out_shape=[jax.ShapeDtypeStruct((n, H_C * DK_C), BF16), jax.ShapeDtypeStruct((n, H_C * DK_C), BF16),
                   jax.ShapeDtypeStruct((n, H_C * DH_V), BF16)],
        compiler_params=_cparams(("parallel",)), name="mla_prep",
    )(p2, tab, g_cq.reshape(1, -1), g_ckv.reshape(1, -1), wq, wkv)


def _merge_kernel(ya_ref, yb_ref, yc_ref, zg_ref, h_ref,
                  pa_ref, pb_ref, pc_ref, wo_ref, lg_ref, lb_ref, hf_ref, hb_ref, *, alpha):
    def front(rs):
        def branch(y_ref, col_z, col_g, p_ref):
            hz = zg_ref[rs, col_z * 1024:(col_z + 1) * 1024] * 0.5
            y = y_ref[rs, :] * (hz + hz * jnp.tanh(hz))
            gate = 0.5 + 0.5 * jnp.tanh(zg_ref[rs, col_g * 1024:(col_g + 1) * 1024] * 0.5)
            return gate.astype(F32) * jnp.dot(y, p_ref[...], preferred_element_type=F32)

        return (branch(ya_ref, COL_ZA, COL_GA, pa_ref) + branch(yb_ref, COL_ZB, COL_GB, pb_ref)
                + branch(yc_ref, COL_ZC, COL_GC, pc_ref)).astype(BF16)

    def back(rs, merged):
        out = jnp.dot(merged, wo_ref[...], preferred_element_type=F32)
        r = alpha * h_ref[rs, :] + out
        mu = jnp.mean(r, axis=-1, keepdims=True)
        rc = r - mu
        var = jnp.mean(rc * rc, axis=-1, keepdims=True)
        y = rc * lax.rsqrt(var + LN_EPS) * lg_ref[...] + lb_ref[...]
        hf_ref[rs, :] = y
        hb_ref[rs, :] = y.astype(BF16)

    tm = h_ref.shape[0]
    strips = [slice(r, r + tm // MERGE_STRIPS) for r in range(0, tm, tm // MERGE_STRIPS)]
    cur = front(strips[0])
    for n, rs in enumerate(strips):
        nxt = front(strips[n + 1]) if n + 1 < len(strips) else None
        back(rs, cur)
        cur = nxt


def _merge(ya, yb, yc, p2, hf, pa, pb, pc, wo, ln_g, ln_b, alpha):
    n = hf.shape[0]
    tm = min(512, n)
    row = pl.BlockSpec((tm, D_MODEL), lambda i: (i, 0))
    wfull =pl.BlockSpec((1024, D_MODEL), lambda i: (0, 0), pipeline_mode=pl.Buffered(1))
    vec = pl.BlockSpec((1, D_MODEL), lambda i: (0, 0))
    return pl.pallas_call(
        functools.partial(_merge_kernel, alpha=alpha), grid=(n // tm,),
        in_specs=[row, row, row, pl.BlockSpec((tm, N_GATE_COLS * 1024), lambda i: (i, 0)), row,
                  wfull, wfull, wfull, wfull, vec, vec],
        out_specs=[row, row],
        out_shape=[jax.ShapeDtypeStruct((n, D_MODEL), F32), jax.ShapeDtypeStruct((n, D_MODEL), BF16)],
        compiler_params=_cparams(("parallel",)), name="merge",
    )(ya, yb, yc, p2, hf, pa, pb, pc, wo, ln_g.reshape(1, -1), ln_b.reshape(1, -1))


def _rot_cols(w):
    half = w.shape[-1] // 2
    return jnp.concatenate([-w[..., half:], w[..., :half]], axis=-1)


def _layer_weights(w_in, b_ia, b_fa, b_fb, w_uq, w_ukv):
    depth = w_in.shape[0]
    idx = []
    acc = 0
    for s in IN_SPLITS[:-1]:
        acc += s
        idx.append(acc)
    (w_ca, w_va, w_oa, w_ia, w_fa, w_za, w_qb, w_kb, w_vb, w_fb, w_zb,
     w_cq, w_ckv, w_kr, w_zc, w_ga, w_gb, w_gc) = jnp.split(w_in, idx, axis=-1)
    w_qb = w_qb * (DH_B ** -0.5 * LOG2E)
    w_big = jnp.concatenate([w.astype(BF16) for w in (w_za, w_zb, w_zc, w_ga, w_gb, w_gc, w_ca, w_va, w_oa, w_qb,
                                                      w_kb, w_vb, w_ckv, w_cq, w_kr, _rot_cols(w_kr))], axis=-1)
    zpad = lambda k: jnp.zeros((depth, D_MODEL, k), F32)
    w_f = jnp.concatenate([w_fa, w_fa, w_fb, zpad(112)], axis=-1)
    w_i = jnp.concatenate([zpad(4), w_ia, zpad(120)], axis=-1)
    b_f = jnp.concatenate([b_fa, b_fa, b_fb, jnp.zeros((depth, 112), F32)], axis=-1)
    b_i = jnp.concatenate([jnp.zeros((depth, 4), F32), b_ia, jnp.zeros((depth, 120), F32)], axis=-1)
    gw = (jnp.concatenate([w_f, w_i], axis=-1).astype(BF16),
          jnp.swapaxes(jnp.concatenate([w_f[..., :16], w_i[..., :16]], axis=-1), -1, -2).astype(BF16),
          b_f[:, :16, None], b_i[:, :16, None], b_f[:, None, :], b_i[:, None, :])
    wq3 = w_uq.reshape(depth, Q_RANK, H_C, DH_NOPE + DH_ROPE) * ((DH_NOPE + DH_ROPE) ** -0.5 * LOG2E)
    wq_rope = wq3[..., DH_NOPE:]
    wq = jnp.concatenate([wq3, _rot_cols(wq_rope)], axis=-1).reshape(depth, Q_RANK, H_C * DK_C).astype(BF16)
    return w_big, gw, wq, w_ukv.astype(BF16)


def kernel(x, positions, ln_in_g, ln_in_b, w_in, conv_w, conv_b, w_qa, w_ka, b_ia, b_fa, norm_ga, b_fb,
           g_cq, w_uq, g_ckv, w_ukv, p_a, p_b, p_c, w_out, ln_g, ln_b):
    bsz, t, _ = x.shape
    n = bsz * t
    depth = w_in.shape[0]
    alpha = (2.0 * depth) ** 0.25

    w_big, gw, wq, wkv = _layer_weights(w_in, b_ia, b_fa, b_fb, w_uq, w_ukv)
    wqa, wka, pa, pb, pc, wo = (w.astype(BF16) for w in (w_qa, w_ka, p_a, p_b, p_c, w_out))
    tab = _rope_table(positions)
    hf, hb = _input_ln(x.reshape(n, D_MODEL), ln_in_g, ln_in_b)
    for l in range(depth):
        p2 = _in_proj(hb, w_big[l])
        p3 = p2.reshape(bsz, t, N_BIG)
        rows, cols, kb = _gates(hb.reshape(bsz, t, D_MODEL), tuple(w[l] for w in gw))
        ya = _mlstm(p3, rows, cols, conv_w[l], conv_b[l], wqa[l], wka[l], norm_ga[l])
        yb = _flash(p3, p3, kb, p3, n_heads=H_B, dk=DH_B, q_blk=COL_QB * 8, k_blk=COL_KB * 8, v_blk=COL_VB * 8)
        qc, kc, vc = _mla_prep(p2, tab, g_cq[l], g_ckv[l], wq[l], wkv[l])
        yc = _flash(qc.reshape(bsz, t, -1), kc.reshape(bsz, t, -1), None, vc.reshape(bsz, t, -1),
                    n_heads=H_C, dk=DK_C, q_blk=0, k_blk=0, v_blk=0)
        hf, hb = _merge(ya.reshape(n, -1), yb.reshape(n, -1), yc.reshape(n, -1), p2, hf,
                        pa[l], pb[l], pc[l], wo[l], ln_g[l], ln_b[l], alpha)
    return hf.reshape(bsz, t, D_MODEL)
```
